```python
import jax, jax.numpy as jnp
from jax import lax
import numpy as np

D_MODEL = 1024
BATCH = 8
SEQ = 4096
DEPTH = 2

N_A_LAYERS = (DEPTH + 1) // 2
N_C_LAYERS = DEPTH // 2
D_FF = 4 * D_MODEL
EPS = 1e-6
NEG = -1e30

CONV_WIDTH = D_MODEL // 2
CONV_GROUPS = 8
CONV_K = 3
MLA_HEADS = 8
MLA_NOPE = 64
MLA_ROPE = 32
MLA_V = 64
MLA_KV_RANK = 256
MLA_Q_RANK = 3 * MLA_KV_RANK
ROPE_THETA = 10000.0
MLA_Q_BLOCK = 128
A_IN = 3 * CONV_WIDTH + MLA_Q_RANK + MLA_KV_RANK + MLA_ROPE
A_MIX = CONV_WIDTH + MLA_HEADS * MLA_V
NSA_HEADS = 16
NSA_GROUPS = 4
NSA_REP = NSA_HEADS // NSA_GROUPS
NSA_DH = 64
CMP_LEN = 32
CMP_STRIDE = 16
SLC_LEN = 64
N_SEL = 16
WINDOW = 512
NSA_Q_BLOCK = 32
FORCE_BONUS = 1e4
KV_W = NSA_GROUPS * NSA_DH
C_MIX = NSA_HEADS * NSA_DH
C_IN = C_MIX + 6 * KV_W + 3 * NSA_HEADS

kernel_name = "hybrid_shortconv_mla_nsa_trunk"


def rms_norm(x, g):
    xf = x.astype(jnp.float32)
    y = xf * lax.rsqrt(jnp.mean(xf * xf, axis=-1, keepdims=True) + EPS)
    return (y * g.astype(jnp.float32)).astype(x.dtype)


def alibi_slopes(n):
    return jnp.asarray(2.0 ** (-8.0 * np.arange(1, n + 1) / n), jnp.float32)


def rope(x, pos):
    half = x.shape[-1] // 2
    inv = ROPE_THETA ** (-jnp.arange(half, dtype=jnp.float32) / half)
    ang = pos.astype(jnp.float32)[:, None] * inv[None, :]
    cos, sin = jnp.cos(ang)[:, None, :], jnp.sin(ang)[:, None, :]
    x1 = x[..., :half].astype(jnp.float32)
    x2 = x[..., half:].astype(jnp.float32)
    return jnp.concatenate([x1 * cos - x2 * sin, x2 * cos + x1 * sin], axis=-1).astype(x.dtype)


def short_conv(u, w):
    c = u.shape[-1]
    return lax.conv_general_dilated(u, w[:, None, :].astype(u.dtype), window_strides=(1,),
                                    padding=[(CONV_K - 1, 0)],
                                    dimension_numbers=('NWC', 'WIO', 'NWC'),
                                    feature_group_count=c)


def dense_causal_attention(q, k, v):
    b, s, h, d = q.shape
    nb = s // MLA_Q_BLOCK
    scale = d ** -0.5
    qb = q.reshape(b, nb, MLA_Q_BLOCK, h, d).transpose(1, 0, 2, 3, 4)
    kpos = jnp.arange(s)

    def one(args):
        q_blk, i = args
        t = i * MLA_Q_BLOCK + jnp.arange(MLA_Q_BLOCK)
        sc = jnp.einsum('bqhd,bshd->bhqs', q_blk, k).astype(jnp.float32) * scale
        sc = jnp.where(kpos[None, :] <= t[:, None], sc, NEG)
        p = jax.nn.softmax(sc, axis=-1).astype(v.dtype)
        return jnp.einsum('bhqs,bshd->bqhd', p, v)

    out = lax.map(one, (qb, jnp.arange(nb)))
    return out.transpose(1, 0, 2, 3, 4).reshape(b, s, h, v.shape[-1])


def mixer_a(xn, w_in, conv_w, q_norm, w_q_up, kv_norm, w_kv_up, w_out):
    b, s, _ = xn.shape
    proj = xn @ w_in
    o1 = CONV_WIDTH; o2 = 2 * CONV_WIDTH; o3 = 3 * CONV_WIDTH
    o4 = o3 + MLA_Q_RANK; o5 = o4 + MLA_KV_RANK
    g_b, g_c, hv, c_q, c_kv, k_r = jnp.split(proj, [o1, o2, o3, o4, o5], axis=-1)
    y_conv = g_b * short_conv(g_c * hv, conv_w)
    pos = jnp.arange(s)
    q = (rms_norm(c_q, q_norm) @ w_q_up).reshape(b, s, MLA_HEADS, MLA_NOPE + MLA_ROPE)
    q = jnp.concatenate([q[..., :MLA_NOPE], rope(q[..., MLA_NOPE:], pos)], axis=-1)
    kv = (rms_norm(c_kv, kv_norm) @ w_kv_up).reshape(b, s, MLA_HEADS, MLA_NOPE + MLA_V)
    k_rope = jnp.broadcast_to(rope(k_r[:, :, None, :], pos), (b, s, MLA_HEADS, MLA_ROPE))
    k = jnp.concatenate([kv[..., :MLA_NOPE], k_rope], axis=-1)
    v = kv[..., MLA_NOPE:]
    y_mla = dense_causal_attention(q, k, v).reshape(b, s, MLA_HEADS * MLA_V)
    return jnp.concatenate([y_conv, y_mla], axis=-1) @ w_out


def mixer_c(xn, w_in, pe_k, w1_k, w2_k, pe_v, w1_v, w2_v, w_out):
    b, s, _ = xn.shape
    G, R, Dh, QB = NSA_GROUPS, NSA_REP, NSA_DH, NSA_Q_BLOCK
    proj = xn @ w_in
    q = proj[..., :C_MIX].reshape(b, s, G, R, Dh)
    kvs = jnp.moveaxis(proj[..., C_MIX:C_MIX + 6 * KV_W].reshape(b, s, 6, G, Dh), 2, 0)
    k_c, v_c, k_s, v_s, k_w, v_w = kvs
    gates = jax.nn.sigmoid(proj[..., C_MIX + 6 * KV_W:].astype(jnp.float32))
    gates = gates.reshape(b, s, 3, G, R).astype(xn.dtype)

    n_cmp = (s - CMP_LEN) // CMP_STRIDE + 1
    starts = np.arange(n_cmp) * CMP_STRIDE
    idx = starts[:, None] + np.arange(CMP_LEN)[None, :]

    def compress(z, pe, w1, w2):
        blk = z[:, idx] + pe[:, None, :]
        hdn = jax.nn.gelu(jnp.einsum('bnlgd,lde->bnge', blk, w1))
        return hdn @ w2

    kc = compress(k_c, pe_k, w1_k, w2_k)
    vc = compress(v_c, pe_v, w1_v, w2_v)
    cmp_centre = jnp.asarray(starts + (CMP_LEN - 1) / 2.0, jnp.float32)
    cmp_end = jnp.asarray(starts + CMP_LEN - 1, jnp.int32)

    n_slc = s // SLC_LEN
    n_sel = min(N_SEL, n_slc)
    ss = np.arange(n_slc) * SLC_LEN
    ov = np.clip(np.minimum(starts[:, None] + CMP_LEN, ss[None, :] + SLC_LEN)
                 - np.maximum(starts[:, None], ss[None, :]), 0, None) / CMP_LEN
    ov = jnp.asarray(ov, jnp.float32)
    kb = k_s.reshape(b, n_slc, SLC_LEN, G, Dh).transpose(0, 3, 1, 2, 4)
    vb = v_s.reshape(b, n_slc, SLC_LEN, G, Dh).transpose(0, 3, 1, 2, 4)

    kwp = jnp.pad(k_w, ((0, 0), (WINDOW, 0), (0, 0), (0, 0)))
    vwp = jnp.pad(v_w, ((0, 0), (WINDOW, 0), (0, 0), (0, 0)))

    slopes = alibi_slopes(NSA_HEADS).reshape(G, R)
    scale = Dh ** -0.5
    nb = s // QB
    qb = q.reshape(b, nb, QB, G, R, Dh).transpose(1, 0, 2, 3, 4, 5)
    gb = gates.reshape(b, nb, QB, 3, G, R).transpose(1, 0, 2, 3, 4, 5)
    bi = jnp.arange(b)[:, None, None, None]
    gi = jnp.arange(G)[None, :, None, None]
    jblk = jnp.arange(n_slc)

    def one(args):
        q_blk, g_blk, i = args
        q0 = i * QB
        t = q0 + jnp.arange(QB)
        tf = t.astype(jnp.float32)
        sc = jnp.einsum('bqgrd,bngd->bgrqn', q_blk, kc).astype(jnp.float32) * scale
        sc = sc - slopes[:, :, None, None] * (tf[:, None] - cmp_centre[None, :])
        valid = cmp_end[None, :] <= t[:, None]
        sc = jnp.where(valid, sc, NEG)
        has = jnp.any(valid, axis=-1).astype(jnp.float32)
        p_cmp = jax.nn.softmax(sc, axis=-1) * has[:, None]
        o_cmp = jnp.einsum('bgrqn,bngd->bqgrd', p_cmp.astype(vc.dtype), vc)
        imp = jnp.einsum('bgrqn,nj->bgqj', p_cmp, ov)
        cur = t // SLC_LEN
        forced = ((jblk[None, :] == 0) | (jblk[None, :] == cur[:, None])
                  | (jblk[None, :] == cur[:, None] - 1)).astype(jnp.float32)
        imp = jnp.where(jblk[None, :] > cur[:, None], NEG, imp + FORCE_BONUS * forced)
        _, sel = lax.top_k(imp, n_sel)
        ks = kb[bi, gi, sel]
        vs = vb[bi, gi, sel]
        kpos = sel[..., None] * SLC_LEN + jnp.arange(SLC_LEN)
        dist = (t[None, None, :, None, None] - kpos).astype(jnp.float32)
        sc = jnp.einsum('bqgrd,bgqnld->bgrqnl', q_blk, ks).astype(jnp.float32) * scale
        sc = sc - slopes[None, :, :, None, None, None] * dist[:, :, None]
        sc = jnp.where((dist >= 0)[:, :, None], sc, NEG)
        p = jax.nn.softmax(sc.reshape(b, G, R, QB, -1), axis=-1).reshape(sc.shape)
        o_slc = jnp.einsum('bgrqnl,bgqnld->bqgrd', p.astype(vs.dtype), vs)
        kw = lax.dynamic_slice_in_dim(kwp, q0, WINDOW + QB, axis=1)
        vw = lax.dynamic_slice_in_dim(vwp, q0, WINDOW + QB, axis=1)
        wpos = q0 - WINDOW + jnp.arange(WINDOW + QB)
        d = t[:, None] - wpos[None, :]
        sc = jnp.einsum('bqgrd,bsgd->bgrqs', q_blk, kw).astype(jnp.float32) * scale
        sc = sc - slopes[:, :, None, None] * d.astype(jnp.float32)
        sc = jnp.where((d >= 0) & (d < WINDOW) & (wpos[None, :] >= 0), sc, NEG)
        p = jax.nn.softmax(sc, axis=-1)
        o_win = jnp.einsum('bgrqs,bsgd->bqgrd', p.astype(vw.dtype), vw)
        return (g_blk[:, :, 0, :, :, None] * o_cmp + g_blk[:, :, 1, :, :, None] * o_slc
                + g_blk[:, :, 2, :, :, None] * o_win)

    out = lax.map(one, (qb, gb, jnp.arange(nb)))
    out = out.transpose(1, 0, 2, 3, 4, 5).reshape(b, s, C_MIX)
    return out @ w_out


def sq_relu_mlp(x, w1, w2):
    h = jax.nn.relu(x @ w1)
    return (h * h) @ w2


def _normal(k, shape, scale):
    return scale * jax.random.normal(k, shape, jnp.float32)


def setup_inputs(seed: int = 0) -> dict:
    key = jax.random.key(seed)
    ks = jax.random.split(key, 24)
    nA, nC, D = N_A_LAYERS, N_C_LAYERS, D_MODEL
    gain = lambda k, shape: 1.0 + _normal(k, shape, 0.05)
    return {
        "x": _normal(ks[0], (BATCH, SEQ, D), 1.0),
        "norm_mix_pre": gain(ks[1], (DEPTH, D)),
        "norm_mix_post": gain(ks[2], (DEPTH, D)),
        "norm_mlp_pre": gain(ks[3], (DEPTH, D)),
        "norm_mlp_post": gain(ks[4], (DEPTH, D)),
        "mlp_w1": _normal(ks[5], (DEPTH, D, D_FF), D ** -0.5),
        "mlp_w2": _normal(ks[6], (DEPTH, D_FF, D), D_FF ** -0.5),
        "a_w_in": _normal(ks[7], (nA, D, A_IN), D ** -0.5),
        "a_conv_w": _normal(ks[8], (nA, CONV_K, CONV_WIDTH), CONV_K ** -0.5),
        "a_q_norm": gain(ks[9], (nA, MLA_Q_RANK)),
        "a_w_q_up": _normal(ks[10], (nA, MLA_Q_RANK, MLA_HEADS * (MLA_NOPE + MLA_ROPE)), MLA_Q_RANK ** -0.5),
        "a_kv_norm": gain(ks[11], (nA, MLA_KV_RANK)),
        "a_w_kv_up": _normal(ks[12], (nA, MLA_KV_RANK, MLA_HEADS * (MLA_NOPE + MLA_V)), MLA_KV_RANK ** -0.5),
        "a_w_out": _normal(ks[13], (nA, A_MIX, D), A_MIX ** -0.5),
        "c_w_in": _normal(ks[14], (nC, D, C_IN), D ** -0.5),
        "c_cmp_pe_k": _normal(ks[15], (nC, CMP_LEN, NSA_DH), 0.1),
        "c_cmp_w1_k": _normal(ks[16], (nC, CMP_LEN, NSA_DH, NSA_DH), (CMP_LEN * NSA_DH) ** -0.5),
        "c_cmp_w2_k": _normal(ks[17], (nC, NSA_DH, NSA_DH), NSA_DH ** -0.5),
        "c_cmp_pe_v": _normal(ks[18], (nC, CMP_LEN, NSA_DH), 0.1),
        "c_cmp_w1_v": _normal(ks[19], (nC, CMP_LEN, NSA_DH, NSA_DH), (CMP_LEN * NSA_DH) ** -0.5),
        "c_cmp_w2_v": _normal(ks[20], (nC, NSA_DH, NSA_DH), NSA_DH ** -0.5),
        "c_w_out": _normal(ks[21], (nC, C_MIX, D), C_MIX ** -0.5),
    }


def reference(x, norm_mix_pre, norm_mix_post, norm_mlp_pre, norm_mlp_post, mlp_w1, mlp_w2,
              a_w_in, a_conv_w, a_q_norm, a_w_q_up, a_kv_norm, a_w_kv_up, a_w_out,
              c_w_in, c_cmp_pe_k, c_cmp_w1_k, c_cmp_w2_k, c_cmp_pe_v, c_cmp_w1_v, c_cmp_w2_v,
              c_w_out):
    for layer in range(DEPTH):
        i = layer // 2
        h = rms_norm(x, norm_mix_pre[layer])
        if layer % 2 == 0:
            m = mixer_a(h, a_w_in[i], a_conv_w[i], a_q_norm[i], a_w_q_up[i],
                        a_kv_norm[i], a_w_kv_up[i], a_w_out[i])
        else:
            m = mixer_c(h, c_w_in[i], c_cmp_pe_k[i], c_cmp_w1_k[i], c_cmp_w2_k[i],
                        c_cmp_pe_v[i], c_cmp_w1_v[i], c_cmp_w2_v[i], c_w_out[i])
        x = x + rms_norm(m, norm_mix_post[layer])
        h = rms_norm(x, norm_mlp_pre[layer])
        x = x + rms_norm(sq_relu_mlp(h, mlp_w1[layer], mlp_w2[layer]), norm_mlp_post[layer])
    return x
```

```python
import functools
import math

import numpy as np
import jax
import jax.numpy as jnp
from jax import lax
from jax.experimental import pallas as pl
from jax.experimental.pallas import tpu as pltpu

F32 = jnp.float32
BF16 = jnp.bfloat16

D_MODEL = 1024
D_FF = 4 * D_MODEL
EPS = 1e-6
NEG = -1e30
LOG2E = math.log2(math.e)

CONV_WIDTH = 512
CONV_K = 3
MLA_HEADS = 8
MLA_NOPE = 64
MLA_ROPE = 32
MLA_V = 64
MLA_KV_RANK = 256
MLA_Q_RANK = 768
MLA_SLOT = 128
ROPE_THETA = 10000.0
ROPE_HALF = MLA_ROPE // 2

NSA_HEADS = 16
NSA_GROUPS = 4
NSA_REP = 4
NSA_DH = 64
CMP_LEN = 32
CMP_STRIDE = 16
SLC_LEN = 64
SLC_SHIFT = 6
N_SEL = 16
WINDOW = 512
FORCE_BONUS = 1e4
KV_W = NSA_GROUPS * NSA_DH
C_MIX = NSA_HEADS * NSA_DH
GATE_ROWS = 16

TOKEN_TILE = 512
ATTN_TILE = 256
FF_CHUNK = 1024
VMEM_LIMIT = 56 * 1024 * 1024


def _cparams(sem):
    return pltpu.CompilerParams(dimension_semantics=sem, vmem_limit_bytes=VMEM_LIMIT)


def _const_spec(shape):
    nd = len(shape)
    return pl.BlockSpec(shape, lambda *_: (0,) * nd)


def _rms(x, g):
    ms = jnp.mean(x * x, axis=-1, keepdims=True)
    return x * lax.rsqrt(ms + EPS) * g


def _nt(a, b):
    return lax.dot_general(a, b, (((1,), (1,)), ((), ())), preferred_element_type=F32)


def _tn(a, b):
    return lax.dot_general(a, b, (((0,), (0,)), ((), ())), preferred_element_type=F32)


def _rope_rows(r1, r2, cos, sin):
    return r1 * cos - r2 * sin, r2 * cos + r1 * sin


def _a_pre_kernel(x_ref, g_ref, w1_ref, wkr_ref, convw_ref, qn_ref, wq_ref, kvn_ref, wk_ref, wv_ref,
                  cos_ref, sin_ref, yconv_ref, qT_ref, k_ref, vT_ref, carry_ref, *, tm, q_scale):
    s = pl.program_id(1)
    x = x_ref[0]
    xn = _rms(x, g_ref[...]).astype(BF16)
    p1 = jnp.dot(xn, w1_ref[...], preferred_element_type=F32)
    o1, o2, o3 = CONV_WIDTH, 2 * CONV_WIDTH, 3 * CONV_WIDTH
    o4 = o3 + MLA_Q_RANK
    g_b, g_c, hv = p1[:, :o1], p1[:, o1:o2], p1[:, o2:o3]
    c_q, c_kv = p1[:, o3:o4], p1[:, o4:]

    u = g_c * hv

    @pl.when(s == 0)
    def _():
        carry_ref[...] = jnp.zeros_like(carry_ref)

    w = convw_ref[...]
    w0, w1, w2 = w[0:1], w[1:2], w[2:3]
    y = w2 * u + w1 * pltpu.roll(u, 1, 0) + w0 * pltpu.roll(u, 2, 0)
    yconv_ref[0] = (g_b * y).astype(BF16)
    hd = 16
    c = carry_ref[...]
    uh = u[0:hd]
    row = lax.broadcasted_iota(jnp.int32, (hd, CONV_WIDTH), 0)
    h1 = jnp.where(row < 1, pltpu.roll(c, 1, 0), pltpu.roll(uh, 1, 0))
    h2 = jnp.where(row < 2, pltpu.roll(c, 2, 0), pltpu.roll(uh, 2, 0))
    yh = w2 * uh + w1 * h1 + w0 * h2
    yconv_ref[0, 0:hd, :] = (g_b[0:hd] * yh).astype(BF16)
    carry_ref[...] = u[tm - hd:tm]

    cos = cos_ref[...]
    sin = sin_ref[...]

    hq = _rms(c_q, qn_ref[...]).astype(BF16)
    qT = _nt(wq_ref[...], hq)
    z32 = jnp.zeros((MLA_SLOT - MLA_NOPE - MLA_ROPE, tm), F32)
    for h in range(MLA_HEADS):
        b0 = MLA_SLOT * h
        r1 = qT[b0 + MLA_NOPE:b0 + MLA_NOPE + ROPE_HALF]
        r2 = qT[b0 + MLA_NOPE + ROPE_HALF:b0 + MLA_NOPE + MLA_ROPE]
        e1, e2 = _rope_rows(r1, r2, cos, sin)
        slot = jnp.concatenate([qT[b0:b0 + MLA_NOPE], e1, e2, z32], axis=0) * q_scale
        qT_ref[0, b0:b0 + MLA_SLOT, :] = slot.astype(BF16)

    krT = _nt(wkr_ref[...], xn)
    r1 = krT[MLA_NOPE:MLA_NOPE + ROPE_HALF]
    r2 = krT[MLA_NOPE + ROPE_HALF:MLA_NOPE + MLA_ROPE]
    e1, e2 = _rope_rows(r1, r2, cos, sin)
    slotT = jnp.concatenate([jnp.zeros((MLA_NOPE, tm), F32), e1, e2, z32], axis=0)
    slot = slotT.T

    hkv = _rms(c_kv, kvn_ref[...]).astype(BF16)
    kn = jnp.dot(hkv, wk_ref[...], preferred_element_type=F32)
    k = kn + jnp.concatenate([slot] * MLA_HEADS, axis=1)
    k_ref[0] = k.astype(BF16)
    vT = _nt(wv_ref[...], hkv)
    for cidx in range(tm // ATTN_TILE):
        vT_ref[0, cidx] = vT[:, cidx * ATTN_TILE:(cidx + 1) * ATTN_TILE].astype(BF16)


def _a_pre(x, g, w1, wkr, convw, qn, wq, kvn, wk, wv, cos, sin):
    b, s, d = x.shape
    tm = min(TOKEN_TILE, s)
    hw = MLA_HEADS * MLA_SLOT
    vw = MLA_HEADS * MLA_V
    q_scale = (MLA_NOPE + MLA_ROPE) ** -0.5 * LOG2E
    kern = functools.partial(_a_pre_kernel, tm=tm, q_scale=q_scale)
    return pl.pallas_call(
        kern,
        grid=(b, s // tm),
        in_specs=[
            pl.BlockSpec((1, tm, d), lambda i, j: (i, j, 0)),
            _const_spec(g.shape), _const_spec(w1.shape), _const_spec(wkr.shape), _const_spec(convw.shape),
            _const_spec(qn.shape), _const_spec(wq.shape), _const_spec(kvn.shape), _const_spec(wk.shape),
            _const_spec(wv.shape),
            pl.BlockSpec((ROPE_HALF, tm), lambda i, j: (0, j)),
            pl.BlockSpec((ROPE_HALF, tm), lambda i, j: (0, j)),
        ],
        out_specs=[
            pl.BlockSpec((1, tm, CONV_WIDTH), lambda i, j: (i, j, 0)),
            pl.BlockSpec((1, hw, tm), lambda i, j: (i, 0, j)),
            pl.BlockSpec((1, tm, hw), lambda i, j: (i, j, 0)),
            pl.BlockSpec((1, tm // ATTN_TILE, vw, ATTN_TILE), lambda i, j: (i, j, 0, 0)),
        ],
        out_shape=[
            jax.ShapeDtypeStruct((b, s, CONV_WIDTH), BF16),
            jax.ShapeDtypeStruct((b, hw, s), BF16),
            jax.ShapeDtypeStruct((b, s, hw), BF16),
            jax.ShapeDtypeStruct((b, s // ATTN_TILE, vw, ATTN_TILE), BF16),
        ],
        scratch_shapes=[pltpu.VMEM((16, CONV_WIDTH), F32)],
        compiler_params=_cparams(("arbitrary", "arbitrary")),
        name="a_pre",
    )(x, g, w1, wkr, convw, qn, wq, kvn, wk, wv, cos, sin)


def _softmax_step(sT, vT, m, l, acc):
    m_new = jnp.maximum(m, jnp.max(sT, axis=0, keepdims=True))
    alpha = jnp.exp2(m - m_new)
    p = jnp.exp2(sT - m_new)
    l = alpha * l + jnp.sum(p, axis=0, keepdims=True)
    acc = alpha * acc + jnp.dot(vT, p.astype(BF16), preferred_element_type=F32)
    return m_new, l, acc


def _tile_pos(t, k0, q0):
    kpos = k0 + lax.broadcasted_iota(jnp.int32, (t, t), 0)
    qpos = q0 + lax.broadcasted_iota(jnp.int32, (t, t), 1)
    return kpos, qpos


def _mla_attn_kernel(qT_ref, k_ref, vT_ref, oT_ref, *, t):
    i = pl.program_id(2)
    qT = qT_ref[0]

    def scores(j):
        k = k_ref[0, pl.ds(pl.multiple_of(j * t, t), t), :]
        return jnp.dot(k, qT, preferred_element_type=F32)

    def body(j, carry):
        return _softmax_step(scores(j), vT_ref[0, j], *carry)

    init = (jnp.full((1, t), NEG, F32), jnp.zeros((1, t), F32), jnp.zeros((MLA_V, t), F32))
    m, l, acc = lax.fori_loop(0, i, body, init)
    kpos, qpos = _tile_pos(t, i * t, i * t)
    sT = jnp.where(kpos <= qpos, scores(i), NEG)
    m, l, acc = _softmax_step(sT, vT_ref[0, i], m, l, acc)
    oT_ref[0] = (acc / l).astype(BF16)


def _mla_attn(qT, k, vT):
    b, hw, s = qT.shape
    t = min(ATTN_TILE, s)
    nkv = s // t
    kern = functools.partial(_mla_attn_kernel, t=t)
    return pl.pallas_call(
        kern,
        grid=(b, MLA_HEADS, s // t),
        in_specs=[
            pl.BlockSpec((1, MLA_SLOT, t), lambda bi, h, i: (bi, h, i)),
            pl.BlockSpec((1, s, MLA_SLOT), lambda bi, h, i: (bi, 0, h)),
            pl.BlockSpec((1, nkv, MLA_V, t), lambda bi, h, i: (bi, 0, h, 0)),
        ],
        out_specs=pl.BlockSpec((1, MLA_V, t), lambda bi, h, i: (bi, h, i)),
        out_shape=jax.ShapeDtypeStruct((b, MLA_HEADS * MLA_V, s), BF16),
        compiler_params=_cparams(("arbitrary", "arbitrary", "arbitrary")),
        name="mla_attn",
    )(qT, k, vT)


def _post_kernel(*refs, has_tok):
    if has_tok:
        x_ref, ytok_ref, yT_ref, wo_tok_ref, wo_ref, gpost_ref, gpre_ref, w1_ref, w2_ref, gmlp_ref, o_ref = refs
    else:
        x_ref, yT_ref, wo_ref, gpost_ref, gpre_ref, w1_ref, w2_ref, gmlp_ref, o_ref = refs
    x = x_ref[0]
    mix = _tn(yT_ref[0], wo_ref[...])
    if has_tok:
        mix = mix + jnp.dot(ytok_ref[0], wo_tok_ref[...], preferred_element_type=F32)
    x1 = x + _rms(mix, gpost_ref[...])
    h = _rms(x1, gpre_ref[...]).astype(BF16)
    acc = jnp.zeros_like(x1)
    for c in range(D_FF // FF_CHUNK):
        a = jnp.dot(h, w1_ref[:, c * FF_CHUNK:(c + 1) * FF_CHUNK], preferred_element_type=F32)
        a = jnp.maximum(a, 0.0)
        a = (a * a).astype(BF16)
        acc = acc + jnp.dot(a, w2_ref[c * FF_CHUNK:(c + 1) * FF_CHUNK, :], preferred_element_type=F32)
    o_ref[0] = x1 + _rms(acc, gmlp_ref[...])


def _post(x, ytok, yT, wo_tok, wo, gpost, gpre, w1, w2, gmlp):
    b, s, d = x.shape
    tm = min(TOKEN_TILE, s)
    has_tok = ytok is not None
    kern = functools.partial(_post_kernel, has_tok=has_tok)
    xspec = pl.BlockSpec((1, tm, d), lambda i, j: (i, j, 0))
    ins, specs = [x], [xspec]
    if has_tok:
        ins.append(ytok)
        specs.append(pl.BlockSpec((1, tm, ytok.shape[-1]), lambda i, j: (i, j, 0)))
    ins.append(yT)
    specs.append(pl.BlockSpec((1, yT.shape[1], tm), lambda i, j: (i, 0, j)))
    if has_tok:
        ins.append(wo_tok)
        specs.append(_const_spec(wo_tok.shape))
    for a in (wo, gpost, gpre, w1, w2, gmlp):
        ins.append(a)
        specs.append(_const_spec(a.shape))
    return pl.pallas_call(
        kern,
        grid=(b, s // tm),
        in_specs=specs,
        out_specs=xspec,
        out_shape=jax.ShapeDtypeStruct(x.shape, x.dtype),
        compiler_params=_cparams(("arbitrary", "arbitrary")),
        name="post_tok" if has_tok else "post",
    )(*ins)


def _c_pre_kernel(x_ref, g_ref, wq_ref, wtok_ref, wf_ref, qT_ref, kcv_ref, ks_ref, kw_ref, vsT_ref, vwT_ref,
                  gT_ref, *, tm, q_scale):
    xn = _rms(x_ref[0], g_ref[...]).astype(BF16)
    qT_ref[0] = (_nt(wq_ref[...], xn) * q_scale).astype(BF16)
    tok = jnp.dot(xn, wtok_ref[...], preferred_element_type=F32)
    kcv_ref[0] = tok[:, :2 * KV_W].astype(BF16)
    for g in range(NSA_GROUPS):
        a = 2 * KV_W + g * NSA_DH
        ks_ref[0, g] = tok[:, a:a + NSA_DH].astype(BF16)
        kw_ref[0, g] = tok[:, a + KV_W:a + KV_W + NSA_DH].astype(BF16)
    fT = _nt(wf_ref[...], xn)
    for cidx in range(tm // ATTN_TILE):
        sl = slice(cidx * ATTN_TILE, (cidx + 1) * ATTN_TILE)
        vsT_ref[0, cidx] = fT[:KV_W, sl].astype(BF16)
        vwT_ref[0, cidx] = fT[KV_W:2 * KV_W, sl].astype(BF16)
    gT_ref[0] = 1.0 / (1.0 + jnp.exp(-fT[2 * KV_W:]))


def _c_pre(x, g, wq, wtok, wf):
    b, s, d = x.shape
    tm = min(TOKEN_TILE, s)
    q_scale = NSA_DH ** -0.5 * LOG2E
    kern = functools.partial(_c_pre_kernel, tm=tm, q_scale=q_scale)
    nt = tm // ATTN_TILE
    grows = NSA_GROUPS * GATE_ROWS
    return pl.pallas_call(
        kern,
        grid=(b, s // tm),
        in_specs=[pl.BlockSpec((1, tm, d), lambda i, j: (i, j, 0)),
                  _const_spec(g.shape), _const_spec(wq.shape), _const_spec(wtok.shape), _const_spec(wf.shape)],
        out_specs=[
            pl.BlockSpec((1, C_MIX, tm), lambda i, j: (i, 0, j)),
            pl.BlockSpec((1, tm, 2 * KV_W), lambda i, j: (i, j, 0)),
            pl.BlockSpec((1, NSA_GROUPS, tm, NSA_DH), lambda i, j: (i, 0, j, 0)),
            pl.BlockSpec((1, NSA_GROUPS, tm, NSA_DH), lambda i, j: (i, 0, j, 0)),
            pl.BlockSpec((1, nt, KV_W, ATTN_TILE), lambda i, j: (i, j, 0, 0)),
            pl.BlockSpec((1, nt, KV_W, ATTN_TILE), lambda i, j: (i, j, 0, 0)),
            pl.BlockSpec((1, grows, tm), lambda i, j: (i, 0, j)),
        ],
        out_shape=[
            jax.ShapeDtypeStruct((b, C_MIX, s), BF16),
            jax.ShapeDtypeStruct((b, s, 2 * KV_W), BF16),
            jax.ShapeDtypeStruct((b, NSA_GROUPS, s, NSA_DH), BF16),
            jax.ShapeDtypeStruct((b, NSA_GROUPS, s, NSA_DH), BF16),
            jax.ShapeDtypeStruct((b, s // ATTN_TILE, KV_W, ATTN_TILE), BF16),
            jax.ShapeDtypeStruct((b, s // ATTN_TILE, KV_W, ATTN_TILE), BF16),
            jax.ShapeDtypeStruct((b, grows, s), F32),
        ],
        compiler_params=_cparams(("arbitrary", "arbitrary")),
        name="c_pre",
    )(x, g, wq, wtok, wf)


def _gelu_tanh(x):
    return 0.5 * x * (1.0 + jnp.tanh(math.sqrt(2.0 / math.pi) * (x + 0.044715 * (x * x * x))))


def _compress_kernel(zk_ref, zv_ref, pek_ref, pev_ref, w1k_ref, w1v_ref, w2k_ref, w2vT_ref, kc_ref, vcT_ref, *, nch):
    def hidden(z_ref, pe_ref, w1_ref):
        z = z_ref[0, 0].astype(F32)
        za = (z + pe_ref[0:1]).astype(BF16)
        zb = (z + pe_ref[1:2]).astype(BF16)
        a = jnp.dot(za, w1_ref[0], preferred_element_type=F32)
        bm = jnp.dot(zb, w1_ref[1], preferred_element_type=F32)
        return _gelu_tanh(a + pltpu.roll(bm, nch - 1, 0)).astype(BF16)

    hk = hidden(zk_ref, pek_ref, w1k_ref)
    kc_ref[0, 0] = jnp.dot(hk, w2k_ref[...], preferred_element_type=F32).astype(BF16)
    hv = hidden(zv_ref, pev_ref, w1v_ref)
    vcT_ref[0, 0] = _nt(w2vT_ref[...], hv).astype(BF16)


def _compress(zk, zv, pek, pev, w1k, w1v, w2k, w2vT):
    b, g, nch, cw = zk.shape
    kern = functools.partial(_compress_kernel, nch=nch)
    zspec = pl.BlockSpec((1, 1, nch, cw), lambda i, j: (i, j, 0, 0))
    return pl.pallas_call(
        kern,
        grid=(b, g),
        in_specs=[zspec, zspec, _const_spec(pek.shape), _const_spec(pev.shape), _const_spec(w1k.shape),
                  _const_spec(w1v.shape), _const_spec(w2k.shape), _const_spec(w2vT.shape)],
        out_specs=[pl.BlockSpec((1, 1, nch, NSA_DH), lambda i, j: (i, j, 0, 0)),
                   pl.BlockSpec((1, 1, NSA_DH, nch), lambda i, j: (i, j, 0, 0))],
        out_shape=[jax.ShapeDtypeStruct((b, g, nch, NSA_DH), BF16),
                   jax.ShapeDtypeStruct((b, g, NSA_DH, nch), BF16)],
        compiler_params=_cparams(("arbitrary", "arbitrary")),
        name="compress",
    )(zk, zv, pek, pev, w1k, w1v, w2k, w2vT)


def _split3(x):
    hi = x.astype(BF16)
    r = x - hi.astype(F32)
    mid = r.astype(BF16)
    lo = (r - mid.astype(F32)).astype(BF16)
    return hi, mid, lo


def _cmp_topk_kernel(slopes_ref, qT_ref, kc_ref, vcT_ref, ovT_ref, gT_ref, ocT_ref, bias_ref, *, t, nch, nslc, nsel):
    g = pl.program_id(1)
    q0 = pl.program_id(2) * t
    kc = kc_ref[0, 0]
    vcT = vcT_ref[0, 0]
    n_i = lax.broadcasted_iota(jnp.int32, (nch, t), 0)
    t_i = q0 + lax.broadcasted_iota(jnp.int32, (nch, t), 1)
    valid = (n_i * CMP_STRIDE + (CMP_LEN - 1)) <= t_i
    dist = t_i.astype(F32) - (n_i.astype(F32) * CMP_STRIDE + (CMP_LEN - 1) / 2.0)
    psum = jnp.zeros((nch, t), F32)
    for r in range(NSA_REP):
        slope = slopes_ref[g * NSA_REP + r] * LOG2E
        sT = jnp.dot(kc, qT_ref[0, r * NSA_DH:(r + 1) * NSA_DH, :], preferred_element_type=F32)
        sT = jnp.where(valid, sT - slope * dist, NEG)
        m = jnp.max(sT, axis=0, keepdims=True)
        e = jnp.where(valid, jnp.exp2(sT - m), 0.0)
        l = jnp.sum(e, axis=0, keepdims=True)
        p = e / jnp.where(l > 0.0, l, 1.0)
        o = jnp.dot(vcT, p.astype(BF16), preferred_element_type=F32)
        ocT_ref[0, r * NSA_DH:(r + 1) * NSA_DH, :] = (gT_ref[0, r:r + 1, :] * o).astype(BF16)
        psum = psum + p

    ovT = ovT_ref[...]
    imp = sum(jnp.dot(ovT, part, preferred_element_type=F32) for part in _split3(psum))
    j_i = lax.broadcasted_iota(jnp.int32, (nslc, t), 0)
    cur = jnp.right_shift(q0 + lax.broadcasted_iota(jnp.int32, (nslc, t), 1), SLC_SHIFT)
    forced = (j_i == 0) | (j_i == cur) | (j_i == cur - 1)
    imp = jnp.where(j_i > cur, NEG, imp + jnp.where(forced, FORCE_BONUS, 0.0))
    rank = jnp.zeros((nslc, t), F32)
    for jp in range(nslc):
        row = imp[jp:jp + 1, :]
        ge = jnp.where(row >= imp, 1.0, 0.0)
        gt = jnp.where(row > imp, 1.0, 0.0)
        rank = rank + jnp.where(j_i > jp, ge, gt)
    bias_ref[0, 0] = jnp.where(rank < nsel, 0.0, NEG)


def _cmp_topk(slopes, qT, kc, vcT, ovT, gT, nsel):
    b, _, s = qT.shape
    t = min(ATTN_TILE, s)
    nch = kc.shape[2]
    nslc = ovT.shape[0]
    kern = functools.partial(_cmp_topk_kernel, t=t, nch=nch, nslc=nslc, nsel=nsel)
    gw = NSA_REP * NSA_DH
    return pl.pallas_call(
        kern,
        grid=(b, NSA_GROUPS, s // t),
        in_specs=[
            pl.BlockSpec(memory_space=pltpu.SMEM),
            pl.BlockSpec((1, gw, t), lambda bi, g, i: (bi, g, i)),
            pl.BlockSpec((1, 1, nch, NSA_DH), lambda bi, g, i: (bi, g, 0, 0)),
            pl.BlockSpec((1, 1, NSA_DH, nch), lambda bi, g, i: (bi, g, 0, 0)),
            _const_spec(ovT.shape),
            pl.BlockSpec((1, GATE_ROWS, t), lambda bi, g, i: (bi, g, i)),
        ],
        out_specs=[pl.BlockSpec((1, gw, t), lambda bi, g, i: (bi, g, i)),
                   pl.BlockSpec((1, 1, nslc, t), lambda bi, g, i: (bi, g, 0, i))],
        out_shape=[jax.ShapeDtypeStruct((b, C_MIX, s), BF16),
                   jax.ShapeDtypeStruct((b, NSA_GROUPS, nslc, s), F32)],
        compiler_params=_cparams(("arbitrary", "arbitrary", "arbitrary")),
        name="cmp_topk",
    )(slopes, qT, kc, vcT, ovT, gT)


def _slc_win_kernel(slopes_ref, qT_ref, ks_ref, vsT_ref, kw_ref, vwT_ref, bias_ref, ocT_ref, gT_ref, yT_ref,
                    m_ref, l_ref, acc_ref, *, t):
    g = pl.program_id(1)
    i = pl.program_id(2)
    q0 = i * t
    blocks_per_tile = t // SLC_LEN
    slopes = [slopes_ref[g * NSA_REP + r] * LOG2E for r in range(NSA_REP)]

    def reset():
        m_ref[...] = jnp.full_like(m_ref, NEG)
        l_ref[...] = jnp.zeros_like(l_ref)
        acc_ref[...] = jnp.zeros_like(acc_ref)

    def tile_update(j, k_ref, vT_ref, extra):
        k = k_ref[0, 0, pl.ds(pl.multiple_of(j * t, t), t), :]
        vT = vT_ref[0, j]
        kpos, qpos = _tile_pos(t, j * t, q0)
        dist = (qpos - kpos).astype(F32)
        for r in range(NSA_REP):
            sT = jnp.dot(k, qT_ref[0, r * NSA_DH:(r + 1) * NSA_DH, :], preferred_element_type=F32)
            sT = sT - slopes[r] * dist + extra
            m, l, acc = _softmax_step(sT, vT, m_ref[r], l_ref[r], acc_ref[r])
            m_ref[r] = m
            l_ref[r] = l
            acc_ref[r] = acc

    def sel_mask(j):
        rows = [jnp.broadcast_to(bias_ref[0, 0, pl.ds(j * blocks_per_tile + c, 1), :], (SLC_LEN, t))
                for c in range(blocks_per_tile)]
        return jnp.concatenate(rows, axis=0)

    reset()

    def slc_body(j, _):
        tile_update(j, ks_ref, vsT_ref, sel_mask(j))
        return 0

    lax.fori_loop(0, i, slc_body, 0)
    kpos, qpos = _tile_pos(t, q0, q0)
    tile_update(i, ks_ref, vsT_ref, sel_mask(i) + jnp.where(kpos <= qpos, 0.0, NEG))
    for r in range(NSA_REP):
        o = acc_ref[r] / l_ref[r]
        rs = slice(r * NSA_DH, (r + 1) * NSA_DH)
        yT_ref[0, rs, :] = (ocT_ref[0, rs, :].astype(F32)
                            + gT_ref[0, NSA_REP + r:NSA_REP + r + 1, :] * o).astype(BF16)

    reset()

    def win_body(j, _):
        kpos, qpos = _tile_pos(t, j * t, q0)
        d = qpos - kpos
        tile_update(j, kw_ref, vwT_ref, jnp.where((d >= 0) & (d < WINDOW), 0.0, NEG))
        return 0

    lax.fori_loop(jnp.maximum(i - WINDOW // t, 0), i + 1, win_body, 0)
    for r in range(NSA_REP):
        o = acc_ref[r] / l_ref[r]
        rs = slice(r * NSA_DH, (r + 1) * NSA_DH)
        yT_ref[0, rs, :] = (yT_ref[0, rs, :].astype(F32)
                            + gT_ref[0, 2 * NSA_REP + r:2 * NSA_REP + r + 1, :] * o).astype(BF16)


def _slc_win(slopes, qT, ks, vsT, kw, vwT, bias, ocT, gT):
    b, _, s = qT.shape
    t = min(ATTN_TILE, s)
    nkv = s // t
    nslc = bias.shape[2]
    gw = NSA_REP * NSA_DH
    kern = functools.partial(_slc_win_kernel, t=t)
    qspec = pl.BlockSpec((1, gw, t), lambda bi, g, i: (bi, g, i))
    kspec = pl.BlockSpec((1, 1, s, NSA_DH), lambda bi, g, i: (bi, g, 0, 0))
    vspec = pl.BlockSpec((1, nkv, NSA_DH, t), lambda bi, g, i: (bi, 0, g, 0))
    return pl.pallas_call(
        kern,
        grid=(b, NSA_GROUPS, s // t),
        in_specs=[
            pl.BlockSpec(memory_space=pltpu.SMEM),
            qspec, kspec, vspec, kspec, vspec,
            pl.BlockSpec((1, 1, nslc, t), lambda bi, g, i: (bi, g, 0, i)),
            qspec,
            pl.BlockSpec((1, GATE_ROWS, t), lambda bi, g, i: (bi, g, i)),
        ],
        out_specs=qspec,
        out_shape=jax.ShapeDtypeStruct((b, C_MIX, s), BF16),
        scratch_shapes=[pltpu.VMEM((NSA_REP, 1, t), F32), pltpu.VMEM((NSA_REP, 1, t), F32),
                        pltpu.VMEM((NSA_REP, NSA_DH, t), F32)],
        compiler_params=_cparams(("arbitrary", "arbitrary", "arbitrary")),
        name="slc_win",
    )(slopes, qT, ks, vsT, kw, vwT, bias, ocT, gT)


def _row(v):
    return v.reshape(1, -1).astype(F32)


def _layer_a(x, g_pre, w_in, conv_w, q_norm, w_q_up, kv_norm, w_kv_up, w_out, g_post, g_mlp_pre, w1, w2, g_mlp_post):
    b, s, d = x.shape
    o3 = 3 * CONV_WIDTH
    o5 = o3 + MLA_Q_RANK + MLA_KV_RANK
    pad = MLA_SLOT - MLA_NOPE - MLA_ROPE
    w_main = w_in[:, :o5].astype(BF16)
    wkr = jnp.pad(w_in[:, o5:].T, ((MLA_NOPE, pad), (0, 0))).astype(BF16)
    wq = w_q_up.reshape(MLA_Q_RANK, MLA_HEADS, MLA_NOPE + MLA_ROPE)
    wq = jnp.pad(wq, ((0, 0), (0, 0), (0, pad))).reshape(MLA_Q_RANK, MLA_HEADS * MLA_SLOT).T.astype(BF16)
    wkv = w_kv_up.reshape(MLA_KV_RANK, MLA_HEADS, MLA_NOPE + MLA_V)
    wk = jnp.pad(wkv[:, :, :MLA_NOPE], ((0, 0), (0, 0), (0, MLA_SLOT - MLA_NOPE)))
    wk = wk.reshape(MLA_KV_RANK, MLA_HEADS * MLA_SLOT).astype(BF16)
    wv = wkv[:, :, MLA_NOPE:].reshape(MLA_KV_RANK, MLA_HEADS * MLA_V).T.astype(BF16)

    inv = ROPE_THETA ** (-jnp.arange(ROPE_HALF, dtype=F32) / ROPE_HALF)
    ang = inv[:, None] * jnp.arange(s, dtype=F32)[None, :]
    cos, sin = jnp.cos(ang), jnp.sin(ang)

    yconv, qT, k, vT = _a_pre(x, _row(g_pre), w_main, wkr, conv_w.astype(F32), _row(q_norm), wq, _row(kv_norm),
                              wk, wv, cos, sin)
    yT = _mla_attn(qT, k, vT)
    wo = w_out.astype(BF16)
    return _post(x, yconv, yT, wo[:CONV_WIDTH], wo[CONV_WIDTH:], _row(g_post), _row(g_mlp_pre),
                 w1.astype(BF16), w2.astype(BF16), _row(g_mlp_post))


def _layer_c(x, g_pre, w_in, pe_k, w1_k, w2_k, pe_v, w1_v, w2_v, w_out, g_post, g_mlp_pre, w1, w2, g_mlp_post):
    b, s, d = x.shape
    G, R, Dh = NSA_GROUPS, NSA_REP, NSA_DH
    wq = w_in[:, :C_MIX].T.astype(BF16)
    kv = w_in[:, C_MIX:C_MIX + 6 * KV_W].reshape(d, 6, KV_W)
    wtok = jnp.concatenate([kv[:, 0], kv[:, 1], kv[:, 2], kv[:, 4]], axis=1).astype(BF16)
    wg = w_in[:, C_MIX + 6 * KV_W:].reshape(d, 3, G, R).transpose(0, 2, 1, 3)
    wg = jnp.pad(wg.reshape(d, G, 3 * R), ((0, 0), (0, 0), (0, GATE_ROWS - 3 * R))).reshape(d, G * GATE_ROWS)
    wf = jnp.concatenate([kv[:, 3], kv[:, 5], wg], axis=1).T.astype(BF16)

    qT, kcv, ks, kw, vsT, vwT, gT = _c_pre(x, _row(g_pre), wq, wtok, wf)

    nch = s // CMP_STRIDE
    half = CMP_LEN // 2
    z = kcv.reshape(b, nch, CMP_STRIDE, 2, G, Dh).transpose(3, 0, 4, 1, 2, 5).reshape(2, b, G, nch, CMP_STRIDE * Dh)

    def w1_halves(w):
        return w.reshape(2, half * Dh, Dh).astype(BF16)

    def pe_halves(pe):
        return pe.reshape(2, half * Dh).astype(F32)

    kc, vcT = _compress(z[0], z[1], pe_halves(pe_k), pe_halves(pe_v), w1_halves(w1_k), w1_halves(w1_v),
                        w2_k.astype(BF16), w2_v.T.astype(BF16))

    n_cmp = (s - CMP_LEN) // CMP_STRIDE + 1
    starts = np.arange(nch) * CMP_STRIDE
    n_slc = s // SLC_LEN
    ss = np.arange(n_slc) * SLC_LEN
    ov = np.clip(np.minimum(starts[:, None] + CMP_LEN, ss[None, :] + SLC_LEN)
                 - np.maximum(starts[:, None], ss[None, :]), 0, None) / CMP_LEN
    ov[n_cmp:] = 0.0
    ovT = jnp.asarray(ov.T, BF16)
    slopes = jnp.asarray(2.0 ** (-8.0 * np.arange(1, NSA_HEADS + 1) / NSA_HEADS), F32)

    ocT, bias = _cmp_topk(slopes, qT, kc, vcT, ovT, gT, min(N_SEL, n_slc))
    yT = _slc_win(slopes, qT, ks, vsT, kw, vwT, bias, ocT, gT)
    return _post(x, None, yT, None, w_out.astype(BF16), _row(g_post), _row(g_mlp_pre),
                 w1.astype(BF16), w2.astype(BF16), _row(g_mlp_post))


def kernel(x, norm_mix_pre, norm_mix_post, norm_mlp_pre, norm_mlp_post, mlp_w1, mlp_w2, a_w_in, a_conv_w, a_q_norm,
           a_w_q_up, a_kv_norm, a_w_kv_up, a_w_out, c_w_in, c_cmp_pe_k, c_cmp_w1_k, c_cmp_w2_k, c_cmp_pe_v,
           c_cmp_w1_v, c_cmp_w2_v, c_w_out):
    depth = norm_mix_pre.shape[0]
    for layer in range(depth):
        i = layer // 2
        common = (norm_mix_post[layer], norm_mlp_pre[layer], mlp_w1[layer], mlp_w2[layer], norm_mlp_post[layer])
        if layer % 2 == 0:
            x = _layer_a(x, norm_mix_pre[layer], a_w_in[i], a_conv_w[i], a_q_norm[i], a_w_q_up[i], a_kv_norm[i],
                         a_w_kv_up[i], a_w_out[i], *common)
        else:
            x = _layer_c(x, norm_mix_pre[layer], c_w_in[i], c_cmp_pe_k[i], c_cmp_w1_k[i], c_cmp_w2_k[i],
                         c_cmp_pe_v[i], c_cmp_w1_v[i], c_cmp_w2_v[i], c_w_out[i], *common)
    return x
```

```python
import functools
import math

import numpy as np
import jax
import jax.numpy as jnp
from jax import lax
from jax.experimental import pallas as pl
from jax.experimental.pallas import tpu as pltpu

F32 = jnp.float32
BF16 = jnp.bfloat16

D_MODEL = 1024
D_FF = 4 * D_MODEL
EPS = 1e-6
NEG = -1e30
LOG2E = math.log2(math.e)

CONV_WIDTH = 512
CONV_K = 3
MLA_HEADS = 8
MLA_NOPE = 64
MLA_ROPE = 32
MLA_V = 64
MLA_KV_RANK = 256
MLA_Q_RANK = 768
MLA_SLOT = 128
ROPE_THETA = 10000.0
ROPE_HALF = MLA_ROPE // 2

NSA_HEADS = 16
NSA_GROUPS = 4
NSA_REP = 4
NSA_DH = 64
CMP_LEN = 32
CMP_STRIDE = 16
SLC_LEN = 64
SLC_SHIFT = 6
N_SEL = 16
WINDOW = 512
FORCE_BONUS = 1e4
KV_W = NSA_GROUPS * NSA_DH
C_MIX = NSA_HEADS * NSA_DH
GATE_ROWS = 16
ONES_ROWS = 16
V_ROWS = 64 + ONES_ROWS
KFEAT = 16

TOKEN_TILE = 512
ATTN_TILE = 256
MLA_HEADS_PER_STEP = 8
NSA_GROUPS_PER_STEP = 2
FF_CHUNK = 1024
VMEM_LIMIT = 56 * 1024 * 1024


def _cparams(sem):
    return pltpu.CompilerParams(dimension_semantics=sem, vmem_limit_bytes=VMEM_LIMIT)


def _const_spec(shape):
    nd = len(shape)
    return pl.BlockSpec(shape, lambda *_: (0,) * nd)


def _rms(x, g):
    ms = jnp.mean(x * x, axis=-1, keepdims=True)
    return x * lax.rsqrt(ms + EPS) * g


def _nt(a, b):
    return lax.dot_general(a, b, (((1,), (1,)), ((), ())), preferred_element_type=F32)


def _tn(a, b):
    return lax.dot_general(a, b, (((0,), (0,)), ((), ())), preferred_element_type=F32)


def _ones_rows(n):
    return jnp.where(lax.broadcasted_iota(jnp.int32, (ONES_ROWS, n), 0) == 0, 1.0, 0.0).astype(F32)


def _rope_rows(r1, r2, cos, sin):
    return r1 * cos - r2 * sin, r2 * cos + r1 * sin


def _a_pre_kernel(x_ref, g_ref, w1_ref, wkr_ref, convw_ref, qn_ref, wq_ref, kvn_ref, wk_ref, wv_ref,
                  cos_ref, sin_ref, yconv_ref, qT_ref, k_ref, vT_ref, carry_ref, *, tm, q_scale):
    s = pl.program_id(1)
    x = x_ref[0]
    xn = _rms(x, g_ref[...]).astype(BF16)
    p1 = jnp.dot(xn, w1_ref[...], preferred_element_type=F32)
    o1, o2, o3 = CONV_WIDTH, 2 * CONV_WIDTH, 3 * CONV_WIDTH
    o4 = o3 + MLA_Q_RANK
    g_b, g_c, hv = p1[:, :o1], p1[:, o1:o2], p1[:, o2:o3]
    c_q, c_kv = p1[:, o3:o4], p1[:, o4:]

    u = g_c * hv

    @pl.when(s == 0)
    def _():
        carry_ref[...] = jnp.zeros_like(carry_ref)

    w = convw_ref[...]
    w0, w1, w2 = w[0:1], w[1:2], w[2:3]
    y = w2 * u + w1 * pltpu.roll(u, 1, 0) + w0 * pltpu.roll(u, 2, 0)
    yconv_ref[0] = (g_b * y).astype(BF16)
    hd = 16
    c = carry_ref[...]
    uh = u[0:hd]
    row = lax.broadcasted_iota(jnp.int32, (hd, CONV_WIDTH), 0)
    h1 = jnp.where(row < 1, pltpu.roll(c, 1, 0), pltpu.roll(uh, 1, 0))
    h2 = jnp.where(row < 2, pltpu.roll(c, 2, 0), pltpu.roll(uh, 2, 0))
    yh = w2 * uh + w1 * h1 + w0 * h2
    yconv_ref[0, 0:hd, :] = (g_b[0:hd] * yh).astype(BF16)
    carry_ref[...] = u[tm - hd:tm]

    cos = cos_ref[...]
    sin = sin_ref[...]

    hq = _rms(c_q, qn_ref[...]).astype(BF16)
    qT = _nt(wq_ref[...], hq)
    z32 = jnp.zeros((MLA_SLOT - MLA_NOPE - MLA_ROPE, tm), F32)
    for h in range(MLA_HEADS):
        b0 = MLA_SLOT * h
        r1 = qT[b0 + MLA_NOPE:b0 + MLA_NOPE + ROPE_HALF]
        r2 = qT[b0 + MLA_NOPE + ROPE_HALF:b0 + MLA_NOPE + MLA_ROPE]
        e1, e2 = _rope_rows(r1, r2, cos, sin)
        slot = jnp.concatenate([qT[b0:b0 + MLA_NOPE], e1, e2, z32], axis=0) * q_scale
        qT_ref[0, b0:b0 + MLA_SLOT, :] = slot.astype(BF16)

    krT = _nt(wkr_ref[...], xn)
    r1 = krT[MLA_NOPE:MLA_NOPE + ROPE_HALF]
    r2 = krT[MLA_NOPE + ROPE_HALF:MLA_NOPE + MLA_ROPE]
    e1, e2 = _rope_rows(r1, r2, cos, sin)
    slotT = jnp.concatenate([jnp.zeros((MLA_NOPE, tm), F32), e1, e2, z32], axis=0)
    slot = slotT.T

    hkv = _rms(c_kv, kvn_ref[...]).astype(BF16)
    kn = jnp.dot(hkv, wk_ref[...], preferred_element_type=F32)
    k = kn + jnp.concatenate([slot] * MLA_HEADS, axis=1)
    k_ref[0] = k.astype(BF16)
    vT = _nt(wv_ref[...], hkv)
    ones = _ones_rows(tm)
    vT = jnp.concatenate([x for h in range(MLA_HEADS) for x in (vT[h * MLA_V:(h + 1) * MLA_V], ones)], axis=0)
    for cidx in range(tm // ATTN_TILE):
        vT_ref[0, cidx] = vT[:, cidx * ATTN_TILE:(cidx + 1) * ATTN_TILE].astype(BF16)


def _a_pre(x, g, w1, wkr, convw, qn, wq, kvn, wk, wv, cos, sin):
    b, s, d = x.shape
    tm = min(TOKEN_TILE, s)
    hw = MLA_HEADS * MLA_SLOT
    vw = MLA_HEADS * V_ROWS
    q_scale = (MLA_NOPE + MLA_ROPE) ** -0.5 * LOG2E
    kern = functools.partial(_a_pre_kernel, tm=tm, q_scale=q_scale)
    return pl.pallas_call(
        kern,
        grid=(b, s // tm),
        in_specs=[
            pl.BlockSpec((1, tm, d), lambda i, j: (i, j, 0)),
            _const_spec(g.shape), _const_spec(w1.shape), _const_spec(wkr.shape), _const_spec(convw.shape),
            _const_spec(qn.shape), _const_spec(wq.shape), _const_spec(kvn.shape), _const_spec(wk.shape),
            _const_spec(wv.shape),
            pl.BlockSpec((ROPE_HALF, tm), lambda i, j: (0, j)),
            pl.BlockSpec((ROPE_HALF, tm), lambda i, j: (0, j)),
        ],
        out_specs=[
            pl.BlockSpec((1, tm, CONV_WIDTH), lambda i, j: (i, j, 0)),
            pl.BlockSpec((1, hw, tm), lambda i, j: (i, 0, j)),
            pl.BlockSpec((1, tm, hw), lambda i, j: (i, j, 0)),
            pl.BlockSpec((1, tm // ATTN_TILE, vw, ATTN_TILE), lambda i, j: (i, j, 0, 0)),
        ],
        out_shape=[
            jax.ShapeDtypeStruct((b, s, CONV_WIDTH), BF16),
            jax.ShapeDtypeStruct((b, hw, s), BF16),
            jax.ShapeDtypeStruct((b, s, hw), BF16),
            jax.ShapeDtypeStruct((b, s // ATTN_TILE, vw, ATTN_TILE), BF16),
        ],
        scratch_shapes=[pltpu.VMEM((16, CONV_WIDTH), F32)],
        compiler_params=_cparams(("arbitrary", "arbitrary")),
        name="a_pre",
    )(x, g, w1, wkr, convw, qn, wq, kvn, wk, wv, cos, sin)


def _tile_pos(t, k0, q0):
    kpos = k0 + lax.broadcasted_iota(jnp.int32, (t, t), 0)
    qpos = q0 + lax.broadcasted_iota(jnp.int32, (t, t), 1)
    return kpos, qpos


def _stage_scores(s, s_slot):
    s_slot[...] = s
    return jnp.max(s, axis=0, keepdims=True)


def _stage_softmax(s_slot, p_slot, m, mt):
    m_new = jnp.maximum(m, mt)
    p_slot[...] = jnp.exp2(s_slot[...] - m_new).astype(BF16)
    return m_new, jnp.exp2(m - m_new)


def _stage_pv(acc_slot, vT, p_slot, alpha):
    acc_slot[...] = alpha * acc_slot[...] + jnp.dot(vT, p_slot[...], preferred_element_type=F32)


def _causal_flash(i, t, chains, scores, vtile, diag_extra, s_ref, p_ref, acc_ref):
    mt0 = []
    for c in chains:
        mt0.append(_stage_scores(scores(c, i) + diag_extra, s_ref.at[c]))
        p_ref[c] = jnp.zeros((t, t), BF16)
        acc_ref[c] = jnp.zeros((V_ROWS, t), F32)
    init = (tuple(mt0),
            tuple(jnp.ones((1, t), F32) for _ in chains),
            tuple(jnp.full((1, t), NEG, F32) for _ in chains))

    def body(n, carry):
        mt, a_prev, m = carry
        j_prev = jnp.where(n == 1, i, jnp.maximum(n - 2, 0))
        j_next = jnp.minimum(n, jnp.maximum(i - 1, 0))
        for c in chains:
            _stage_pv(acc_ref.at[c], vtile(c, j_prev), p_ref.at[c], a_prev[c])
        m_new, alpha = [], []
        for c in chains:
            mn, al = _stage_softmax(s_ref.at[c], p_ref.at[c], m[c], mt[c])
            m_new.append(mn)
            alpha.append(al)
        mt_next = tuple(_stage_scores(scores(c, j_next), s_ref.at[c]) for c in chains)
        return mt_next, tuple(alpha), tuple(m_new)

    _, a_prev, _ = lax.fori_loop(0, i + 1, body, init)
    j_last = jnp.where(i == 0, 0, i - 1)
    for c in chains:
        _stage_pv(acc_ref.at[c], vtile(c, j_last), p_ref.at[c], a_prev[c])


def _normalized(acc_slot):
    acc = acc_slot[...]
    return acc[:V_ROWS - ONES_ROWS] / acc[V_ROWS - ONES_ROWS:V_ROWS - ONES_ROWS + 1]


def _mla_attn_kernel(qT_ref, k_ref, vT_ref, oT_ref, s_ref, p_ref, acc_ref, *, t, hb):
    i = pl.program_id(2)

    def scores(h, j):
        k = k_ref[0, pl.ds(pl.multiple_of(j * t, t), t), h * MLA_SLOT:(h + 1) * MLA_SLOT]
        return jnp.dot(k, qT_ref[0, h * MLA_SLOT:(h + 1) * MLA_SLOT, :], preferred_element_type=F32)

    def vtile(h, j):
        return vT_ref[0, j, h * V_ROWS:(h + 1) * V_ROWS, :]

    kpos, qpos = _tile_pos(t, 0, 0)
    causal = jnp.where(kpos <= qpos, 0.0, NEG)
    _causal_flash(i, t, range(hb), scores, vtile, causal, s_ref, p_ref, acc_ref)
    for h in range(hb):
        oT_ref[0, h * MLA_V:(h + 1) * MLA_V, :] = _normalized(acc_ref.at[h]).astype(BF16)


def _mla_attn(qT, k, vT):
    b, hw, s = qT.shape
    t = min(ATTN_TILE, s)
    nkv = s // t
    hb = MLA_HEADS_PER_STEP
    kern = functools.partial(_mla_attn_kernel, t=t, hb=hb)
    return pl.pallas_call(
        kern,
        grid=(b, MLA_HEADS // hb, s // t),
        in_specs=[
            pl.BlockSpec((1, hb * MLA_SLOT, t), lambda bi, h, i: (bi, h, i)),
            pl.BlockSpec((1, s, hb * MLA_SLOT), lambda bi, h, i: (bi, 0, h)),
            pl.BlockSpec((1, nkv, hb * V_ROWS, t), lambda bi, h, i: (bi, 0, h, 0)),
        ],
        out_specs=pl.BlockSpec((1, hb * MLA_V, t), lambda bi, h, i: (bi, h, i)),
        out_shape=jax.ShapeDtypeStruct((b, MLA_HEADS * MLA_V, s), BF16),
        scratch_shapes=[pltpu.VMEM((hb, t, t), F32), pltpu.VMEM((hb, t, t), BF16),
                        pltpu.VMEM((hb, V_ROWS, t), F32)],
        compiler_params=_cparams(("arbitrary", "arbitrary", "arbitrary")),
        name="mla_attn",
    )(qT, k, vT)


def _post_kernel(*refs, has_tok):
    if has_tok:
        x_ref, ytok_ref, yT_ref, wo_tok_ref, wo_ref, gpost_ref, gpre_ref, w1_ref, w2_ref, gmlp_ref, o_ref = refs
    else:
        x_ref, yT_ref, wo_ref, gpost_ref, gpre_ref, w1_ref, w2_ref, gmlp_ref, o_ref = refs
    x = x_ref[0]
    mix = _tn(yT_ref[0], wo_ref[...])
    if has_tok:
        mix = mix + jnp.dot(ytok_ref[0], wo_tok_ref[...], preferred_element_type=F32)
    x1 = x + _rms(mix, gpost_ref[...])
    h = _rms(x1, gpre_ref[...]).astype(BF16)
    acc = jnp.zeros_like(x1)
    for c in range(D_FF // FF_CHUNK):
        a = jnp.dot(h, w1_ref[:, c * FF_CHUNK:(c + 1) * FF_CHUNK], preferred_element_type=F32)
        a = jnp.maximum(a, 0.0)
        a = (a * a).astype(BF16)
        acc = acc + jnp.dot(a, w2_ref[c * FF_CHUNK:(c + 1) * FF_CHUNK, :], preferred_element_type=F32)
    o_ref[0] = x1 + _rms(acc, gmlp_ref[...])


def _post(x, ytok, yT, wo_tok, wo, gpost, gpre, w1, w2, gmlp):
    b, s, d = x.shape
    tm = min(TOKEN_TILE, s)
    has_tok = ytok is not None
    kern = functools.partial(_post_kernel, has_tok=has_tok)
    xspec = pl.BlockSpec((1, tm, d), lambda i, j: (i, j, 0))
    ins, specs = [x], [xspec]
    if has_tok:
        ins.append(ytok)
        specs.append(pl.BlockSpec((1, tm, ytok.shape[-1]), lambda i, j: (i, j, 0)))
    ins.append(yT)
    specs.append(pl.BlockSpec((1, yT.shape[1], tm), lambda i, j: (i, 0, j)))
    if has_tok:
        ins.append(wo_tok)
        specs.append(_const_spec(wo_tok.shape))
    for a in (wo, gpost, gpre, w1, w2, gmlp):
        ins.append(a)
        specs.append(_const_spec(a.shape))
    return pl.pallas_call(
        kern,
        grid=(b, s // tm),
        in_specs=specs,
        out_specs=xspec,
        out_shape=jax.ShapeDtypeStruct(x.shape, x.dtype),
        compiler_params=_cparams(("arbitrary", "arbitrary")),
        name="post_tok" if has_tok else "post",
    )(*ins)


def _c_pre_kernel(x_ref, g_ref, wq_ref, wtok_ref, wf_ref, kfs_ref, kfw_ref, qT_ref, kcv_ref, ks_ref, kw_ref,
                  vsT_ref, vwT_ref, gT_ref, *, tm, q_scale):
    xn = _rms(x_ref[0], g_ref[...]).astype(BF16)
    qT_ref[0] = (_nt(wq_ref[...], xn) * q_scale).astype(BF16)
    tok = jnp.dot(xn, wtok_ref[...], preferred_element_type=F32)
    kcv_ref[0] = tok[:, :2 * KV_W].astype(BF16)
    kfs = kfs_ref[...]
    kfw = kfw_ref[...]
    for g in range(NSA_GROUPS):
        a = 2 * KV_W + g * NSA_DH
        ks_ref[0, g] = jnp.concatenate([tok[:, a:a + NSA_DH], kfs], axis=1).astype(BF16)
        kw_ref[0, g] = jnp.concatenate([tok[:, a + KV_W:a + KV_W + NSA_DH], kfw], axis=1).astype(BF16)
    fT = _nt(wf_ref[...], xn)
    ones = _ones_rows(tm)

    def with_ones(v):
        return jnp.concatenate([x for g in range(NSA_GROUPS) for x in (v[g * NSA_DH:(g + 1) * NSA_DH], ones)],
                               axis=0)

    vs = with_ones(fT[:KV_W])
    vw = with_ones(fT[KV_W:2 * KV_W])
    for cidx in range(tm // ATTN_TILE):
        sl = slice(cidx * ATTN_TILE, (cidx + 1) * ATTN_TILE)
        vsT_ref[0, cidx] = vs[:, sl].astype(BF16)
        vwT_ref[0, cidx] = vw[:, sl].astype(BF16)
    gT_ref[0] = 1.0 / (1.0 + jnp.exp(-fT[2 * KV_W:]))


def _c_pre(x, g, wq, wtok, wf, kfs, kfw):
    b, s, d = x.shape
    tm = min(TOKEN_TILE, s)
    q_scale = NSA_DH ** -0.5 * LOG2E
    kern = functools.partial(_c_pre_kernel, tm=tm, q_scale=q_scale)
    nt = tm // ATTN_TILE
    grows = NSA_GROUPS * GATE_ROWS
    kaug = NSA_DH + kfs.shape[1]
    vrows = NSA_GROUPS * V_ROWS
    kf_spec = pl.BlockSpec((tm, kfs.shape[1]), lambda i, j: (j, 0))
    return pl.pallas_call(
        kern,
        grid=(b, s // tm),
        in_specs=[pl.BlockSpec((1, tm, d), lambda i, j: (i, j, 0)),
                  _const_spec(g.shape), _const_spec(wq.shape), _const_spec(wtok.shape), _const_spec(wf.shape),
                  kf_spec, kf_spec],
        out_specs=[
            pl.BlockSpec((1, C_MIX, tm), lambda i, j: (i, 0, j)),
            pl.BlockSpec((1, tm, 2 * KV_W), lambda i, j: (i, j, 0)),
            pl.BlockSpec((1, NSA_GROUPS, tm, kaug), lambda i, j: (i, 0, j, 0)),
            pl.BlockSpec((1, NSA_GROUPS, tm, kaug), lambda i, j: (i, 0, j, 0)),
            pl.BlockSpec((1, nt, vrows, ATTN_TILE), lambda i, j: (i, j, 0, 0)),
            pl.BlockSpec((1, nt, vrows, ATTN_TILE), lambda i, j: (i, j, 0, 0)),
            pl.BlockSpec((1, grows, tm), lambda i, j: (i, 0, j)),
        ],
        out_shape=[
            jax.ShapeDtypeStruct((b, C_MIX, s), BF16),
            jax.ShapeDtypeStruct((b, s, 2 * KV_W), BF16),
            jax.ShapeDtypeStruct((b, NSA_GROUPS, s, kaug), BF16),
            jax.ShapeDtypeStruct((b, NSA_GROUPS, s, kaug), BF16),
            jax.ShapeDtypeStruct((b, s // ATTN_TILE, vrows, ATTN_TILE), BF16),
            jax.ShapeDtypeStruct((b, s // ATTN_TILE, vrows, ATTN_TILE), BF16),
            jax.ShapeDtypeStruct((b, grows, s), F32),
        ],
        compiler_params=_cparams(("arbitrary", "arbitrary")),
        name="c_pre",
    )(x, g, wq, wtok, wf, kfs, kfw)


def _gelu_tanh(x):
    return 0.5 * x * (1.0 + jnp.tanh(math.sqrt(2.0 / math.pi) * (x + 0.044715 * (x * x * x))))


def _compress_kernel(zk_ref, zv_ref, pek_ref, pev_ref, w1k_ref, w1v_ref, w2k_ref, w2vT_ref, kc_ref, vcT_ref, *, nch):
    def hidden(z_ref, pe_ref, w1_ref):
        z = z_ref[0, 0].astype(F32)
        za = (z + pe_ref[0:1]).astype(BF16)
        zb = (z + pe_ref[1:2]).astype(BF16)
        a = jnp.dot(za, w1_ref[0], preferred_element_type=F32)
        bm = jnp.dot(zb, w1_ref[1], preferred_element_type=F32)
        return _gelu_tanh(a + pltpu.roll(bm, nch - 1, 0)).astype(BF16)

    hk = hidden(zk_ref, pek_ref, w1k_ref)
    kc_ref[0, 0] = jnp.dot(hk, w2k_ref[...], preferred_element_type=F32).astype(BF16)
    hv = hidden(zv_ref, pev_ref, w1v_ref)
    vcT_ref[0, 0] = _nt(w2vT_ref[...], hv).astype(BF16)


def _compress(zk, zv, pek, pev, w1k, w1v, w2k, w2vT):
    b, g, nch, cw = zk.shape
    kern = functools.partial(_compress_kernel, nch=nch)
    zspec = pl.BlockSpec((1, 1, nch, cw), lambda i, j: (i, j, 0, 0))
    return pl.pallas_call(
        kern,
        grid=(b, g),
        in_specs=[zspec, zspec, _const_spec(pek.shape), _const_spec(pev.shape), _const_spec(w1k.shape),
                  _const_spec(w1v.shape), _const_spec(w2k.shape), _const_spec(w2vT.shape)],
        out_specs=[pl.BlockSpec((1, 1, nch, NSA_DH), lambda i, j: (i, j, 0, 0)),
                   pl.BlockSpec((1, 1, NSA_DH, nch), lambda i, j: (i, j, 0, 0))],
        out_shape=[jax.ShapeDtypeStruct((b, g, nch, NSA_DH), BF16),
                   jax.ShapeDtypeStruct((b, g, NSA_DH, nch), BF16)],
        compiler_params=_cparams(("arbitrary", "arbitrary")),
        name="compress",
    )(zk, zv, pek, pev, w1k, w1v, w2k, w2vT)


def _split3(x):
    hi = x.astype(BF16)
    r = x - hi.astype(F32)
    mid = r.astype(BF16)
    lo = (r - mid.astype(F32)).astype(BF16)
    return hi, mid, lo


def _cmp_topk_kernel(slopes_ref, qT_ref, kc_ref, vcT_ref, ovT_ref, gT_ref, ocT_ref, bias_ref, *, t, nch, nslc, nsel):
    g = pl.program_id(1)
    q0 = pl.program_id(2) * t
    kc = kc_ref[0, 0]
    vcT = vcT_ref[0, 0]
    n_i = lax.broadcasted_iota(jnp.int32, (nch, t), 0)
    t_i = q0 + lax.broadcasted_iota(jnp.int32, (nch, t), 1)
    valid = (n_i * CMP_STRIDE + (CMP_LEN - 1)) <= t_i
    dist = t_i.astype(F32) - (n_i.astype(F32) * CMP_STRIDE + (CMP_LEN - 1) / 2.0)
    psum = jnp.zeros((nch, t), F32)
    for r in range(NSA_REP):
        slope = slopes_ref[g * NSA_REP + r] * LOG2E
        sT = jnp.dot(kc, qT_ref[0, r * NSA_DH:(r + 1) * NSA_DH, :], preferred_element_type=F32)
        sT = jnp.where(valid, sT - slope * dist, NEG)
        m = jnp.max(sT, axis=0, keepdims=True)
        e = jnp.where(valid, jnp.exp2(sT - m), 0.0)
        l = jnp.sum(e, axis=0, keepdims=True)
        p = e / jnp.where(l > 0.0, l, 1.0)
        o = jnp.dot(vcT, p.astype(BF16), preferred_element_type=F32)
        ocT_ref[0, r * NSA_DH:(r + 1) * NSA_DH, :] = (gT_ref[0, r:r + 1, :] * o).astype(BF16)
        psum = psum + p

    ovT = ovT_ref[...]
    imp = sum(jnp.dot(ovT, part, preferred_element_type=F32) for part in _split3(psum))
    j_i = lax.broadcasted_iota(jnp.int32, (nslc, t), 0)
    cur = jnp.right_shift(q0 + lax.broadcasted_iota(jnp.int32, (nslc, t), 1), SLC_SHIFT)
    forced = (j_i == 0) | (j_i == cur) | (j_i == cur - 1)
    imp = jnp.where(j_i > cur, NEG, imp + jnp.where(forced, FORCE_BONUS, 0.0))
    rank = jnp.zeros((nslc, t), F32)
    for jp in range(nslc):
        row = imp[jp:jp + 1, :]
        ge = jnp.where(row >= imp, 1.0, 0.0)
        gt = jnp.where(row > imp, 1.0, 0.0)
        rank = rank + jnp.where(j_i > jp, ge, gt)
    bias_ref[0, 0] = jnp.where(rank < nsel, 0.0, NEG)


def _cmp_topk(slopes, qT, kc, vcT, ovT, gT, nsel):
    b, _, s = qT.shape
    t = min(ATTN_TILE, s)
    nch = kc.shape[2]
    nslc = ovT.shape[0]
    kern = functools.partial(_cmp_topk_kernel, t=t, nch=nch, nslc=nslc, nsel=nsel)
    gw = NSA_REP * NSA_DH
    return pl.pallas_call(
        kern,
        grid=(b, NSA_GROUPS, s // t),
        in_specs=[
            pl.BlockSpec(memory_space=pltpu.SMEM),
            pl.BlockSpec((1, gw, t), lambda bi, g, i: (bi, g, i)),
            pl.BlockSpec((1, 1, nch, NSA_DH), lambda bi, g, i: (bi, g, 0, 0)),
            pl.BlockSpec((1, 1, NSA_DH, nch), lambda bi, g, i: (bi, g, 0, 0)),
            _const_spec(ovT.shape),
            pl.BlockSpec((1, GATE_ROWS, t), lambda bi, g, i: (bi, g, i)),
        ],
        out_specs=[pl.BlockSpec((1, gw, t), lambda bi, g, i: (bi, g, i)),
                   pl.BlockSpec((1, 1, nslc, t), lambda bi, g, i: (bi, g, 0, i))],
        out_shape=[jax.ShapeDtypeStruct((b, C_MIX, s), BF16),
                   jax.ShapeDtypeStruct((b, NSA_GROUPS, nslc, s), F32)],
        compiler_params=_cparams(("arbitrary", "arbitrary", "arbitrary")),
        name="cmp_topk",
    )(slopes, qT, kc, vcT, ovT, gT)


def _slc_win_kernel(tab_ref, qT_ref, ks_ref, vsT_ref, kw_ref, vwT_ref, bias_ref, ocT_ref, gT_ref, yT_ref,
                    qa_ref, s_ref, p_ref, acc_ref, sw_ref, pw_ref, accw_ref, *, t, gb, nslc):
    gp = pl.program_id(1)
    i = pl.program_id(2)
    chains = range(gb * NSA_REP)
    feat0 = NSA_DH + nslc

    row = lax.broadcasted_iota(jnp.int32, (KFEAT, t), 0)
    r3 = row - 3 * (row >= 3).astype(jnp.int32) - 3 * (row >= 6).astype(jnp.int32)
    q0f = jnp.full((KFEAT, t), i * t, jnp.int32).astype(F32)
    for c in chains:
        gl, r = divmod(c, NSA_REP)
        head = (gp * gb + gl) * NSA_REP + r
        s_hi, s_mid, s_lo, slope = (tab_ref[head * 4 + k] for k in range(4))
        c_hi, c_mid, c_lo = (x.astype(F32) for x in _split3(-slope * q0f))
        svals = jnp.where(r3 == 0, s_hi, jnp.where(r3 == 1, s_mid, s_lo))
        cvals = jnp.where(r3 == 0, c_hi, jnp.where(r3 == 1, c_mid, c_lo))
        feat = jnp.where(row < 6, svals, jnp.where(row < 9, cvals, 0.0))
        qa_ref[c, 0:NSA_DH, :] = qT_ref[0, c * NSA_DH:(c + 1) * NSA_DH, :]
        qa_ref[c, NSA_DH:feat0, :] = bias_ref[0, gl].astype(BF16)
        qa_ref[c, feat0:feat0 + KFEAT, :] = feat.astype(BF16)

    def scores_from(k_ref):
        def scores(c, j):
            k = k_ref[0, c // NSA_REP, pl.ds(pl.multiple_of(j * t, t), t), :]
            return jnp.dot(k, qa_ref[c], preferred_element_type=F32)
        return scores

    def vtile_from(vT_ref):
        def vtile(c, j):
            gl = c // NSA_REP
            return vT_ref[0, j, gl * V_ROWS:(gl + 1) * V_ROWS, :]
        return vtile

    kpos, qpos = _tile_pos(t, 0, 0)
    causal = jnp.where(kpos <= qpos, 0.0, NEG)

    w_scores = scores_from(kw_ref)
    w_vtile = vtile_from(vwT_ref)
    w_tiles = (i, jnp.maximum(i - 1, 0), jnp.maximum(i - 2, 0))
    in_seq1 = jnp.where((i - 1) * t + kpos >= 0, 0.0, NEG)
    in_seq2 = jnp.where((i - 2) * t + kpos >= 0, 0.0, NEG)
    w_extra = (causal, in_seq1, jnp.where(qpos < kpos, in_seq2, NEG))
    mt = [[_stage_scores(w_scores(c, w_tiles[w]) + w_extra[w], sw_ref.at[w, c]) for c in chains] for w in range(3)]
    m = [jnp.full((1, t), NEG, F32) for _ in chains]
    for c in chains:
        accw_ref[c] = jnp.zeros((V_ROWS, t), F32)
    for w in range(3):
        for c in chains:
            m[c], alpha = _stage_softmax(sw_ref.at[w, c], pw_ref.at[w, c], m[c], mt[w][c])
            _stage_pv(accw_ref.at[c], w_vtile(c, w_tiles[w]), pw_ref.at[w, c], alpha)

    _causal_flash(i, t, chains, scores_from(ks_ref), vtile_from(vsT_ref), causal, s_ref, p_ref, acc_ref)

    for c in chains:
        gl, r = divmod(c, NSA_REP)
        g0 = gl * GATE_ROWS
        rs = slice(c * NSA_DH, (c + 1) * NSA_DH)
        y = (ocT_ref[0, rs, :].astype(F32)
             + gT_ref[0, g0 + NSA_REP + r:g0 + NSA_REP + r + 1, :] * _normalized(acc_ref.at[c])
             + gT_ref[0, g0 + 2 * NSA_REP + r:g0 + 2 * NSA_REP + r + 1, :] * _normalized(accw_ref.at[c]))
        yT_ref[0, rs, :] = y.astype(BF16)


def _slc_win(tab, qT, ks, vsT, kw, vwT, bias, ocT, gT):
    b, _, s = qT.shape
    t = ATTN_TILE
    assert s % t == 0 and WINDOW == 2 * t
    nkv = s // t
    nslc = bias.shape[2]
    kaug = ks.shape[3]
    gb = NSA_GROUPS_PER_STEP
    nc = gb * NSA_REP
    gw = nc * NSA_DH
    kern = functools.partial(_slc_win_kernel, t=t, gb=gb, nslc=nslc)
    qspec = pl.BlockSpec((1, gw, t), lambda bi, g, i: (bi, g, i))
    kspec = pl.BlockSpec((1, gb, s, kaug), lambda bi, g, i: (bi, g, 0, 0))
    vspec = pl.BlockSpec((1, nkv, gb * V_ROWS, t), lambda bi, g, i: (bi, 0, g, 0))
    return pl.pallas_call(
        kern,
        grid=(b, NSA_GROUPS // gb, s // t),
        in_specs=[
            pl.BlockSpec(memory_space=pltpu.SMEM),
            qspec, kspec, vspec, kspec, vspec,
            pl.BlockSpec((1, gb, nslc, t), lambda bi, g, i: (bi, g, 0, i)),
            qspec,
            pl.BlockSpec((1, gb * GATE_ROWS, t), lambda bi, g, i: (bi, g, i)),
        ],
        out_specs=qspec,
        out_shape=jax.ShapeDtypeStruct((b, C_MIX, s), BF16),
        scratch_shapes=[pltpu.VMEM((nc, kaug, t), BF16),
                        pltpu.VMEM((nc, t, t), F32), pltpu.VMEM((nc, t, t), BF16), pltpu.VMEM((nc, V_ROWS, t), F32),
                        pltpu.VMEM((3, nc, t, t), F32), pltpu.VMEM((3, nc, t, t), BF16),
                        pltpu.VMEM((nc, V_ROWS, t), F32)],
        compiler_params=_cparams(("arbitrary", "arbitrary", "arbitrary")),
        name="slc_win",
    )(tab, qT, ks, vsT, kw, vwT, bias, ocT, gT)


def _row(v):
    return v.reshape(1, -1).astype(F32)


def _layer_a(x, g_pre, w_in, conv_w, q_norm, w_q_up, kv_norm, w_kv_up, w_out, g_post, g_mlp_pre, w1, w2, g_mlp_post):
    b, s, d = x.shape
    o3 = 3 * CONV_WIDTH
    o5 = o3 + MLA_Q_RANK + MLA_KV_RANK
    pad = MLA_SLOT - MLA_NOPE - MLA_ROPE
    w_main = w_in[:, :o5].astype(BF16)
    wkr = jnp.pad(w_in[:, o5:].T, ((MLA_NOPE, pad), (0, 0))).astype(BF16)
    wq = w_q_up.reshape(MLA_Q_RANK, MLA_HEADS, MLA_NOPE + MLA_ROPE)
    wq = jnp.pad(wq, ((0, 0), (0, 0), (0, pad))).reshape(MLA_Q_RANK, MLA_HEADS * MLA_SLOT).T.astype(BF16)
    wkv = w_kv_up.reshape(MLA_KV_RANK, MLA_HEADS, MLA_NOPE + MLA_V)
    wk = jnp.pad(wkv[:, :, :MLA_NOPE], ((0, 0), (0, 0), (0, MLA_SLOT - MLA_NOPE)))
    wk = wk.reshape(MLA_KV_RANK, MLA_HEADS * MLA_SLOT).astype(BF16)
    wv = wkv[:, :, MLA_NOPE:].reshape(MLA_KV_RANK, MLA_HEADS * MLA_V).T.astype(BF16)

    inv = ROPE_THETA ** (-jnp.arange(ROPE_HALF, dtype=F32) / ROPE_HALF)
    ang = inv[:, None] * jnp.arange(s, dtype=F32)[None, :]
    cos, sin = jnp.cos(ang), jnp.sin(ang)

    yconv, qT, k, vT = _a_pre(x, _row(g_pre), w_main, wkr, conv_w.astype(F32), _row(q_norm), wq, _row(kv_norm),
                              wk, wv, cos, sin)
    yT = _mla_attn(qT, k, vT)
    wo = w_out.astype(BF16)
    return _post(x, yconv, yT, wo[:CONV_WIDTH], wo[CONV_WIDTH:], _row(g_post), _row(g_mlp_pre),
                 w1.astype(BF16), w2.astype(BF16), _row(g_mlp_post))


def _layer_c(x, g_pre, w_in, pe_k, w1_k, w2_k, pe_v, w1_v, w2_v, w_out, g_post, g_mlp_pre, w1, w2, g_mlp_post):
    b, s, d = x.shape
    G, R, Dh = NSA_GROUPS, NSA_REP, NSA_DH
    wq = w_in[:, :C_MIX].T.astype(BF16)
    kv = w_in[:, C_MIX:C_MIX + 6 * KV_W].reshape(d, 6, KV_W)
    wtok = jnp.concatenate([kv[:, 0], kv[:, 1], kv[:, 2], kv[:, 4]], axis=1).astype(BF16)
    wg = w_in[:, C_MIX + 6 * KV_W:].reshape(d, 3, G, R).transpose(0, 2, 1, 3)
    wg = jnp.pad(wg.reshape(d, G, 3 * R), ((0, 0), (0, 0), (0, GATE_ROWS - 3 * R))).reshape(d, G * GATE_ROWS)
    wf = jnp.concatenate([kv[:, 3], kv[:, 5], wg], axis=1).T.astype(BF16)

    n_slc = s // SLC_LEN
    pos = np.arange(s)
    blk, within = pos // SLC_LEN, pos % SLC_LEN
    feats = np.zeros((s, KFEAT), np.float32)
    feats[:, 0:3] = (blk * SLC_LEN)[:, None]
    feats[:, 3:6] = within[:, None]
    feats[:, 6:9] = 1.0
    onehot = (blk[:, None] == np.arange(n_slc)[None, :]).astype(np.float32)
    kfs = jnp.asarray(np.concatenate([onehot, feats], axis=1), F32)
    kfw = jnp.asarray(np.concatenate([np.zeros_like(onehot), feats], axis=1), F32)

    qT, kcv, ks, kw, vsT, vwT, gT = _c_pre(x, _row(g_pre), wq, wtok, wf, kfs, kfw)

    nch = s // CMP_STRIDE
    half = CMP_LEN // 2
    z = kcv.reshape(b, nch, CMP_STRIDE, 2, G, Dh).transpose(3, 0, 4, 1, 2, 5).reshape(2, b, G, nch, CMP_STRIDE * Dh)

    def w1_halves(w):
        return w.reshape(2, half * Dh, Dh).astype(BF16)

    def pe_halves(pe):
        return pe.reshape(2, half * Dh).astype(F32)

    kc, vcT = _compress(z[0], z[1], pe_halves(pe_k), pe_halves(pe_v), w1_halves(w1_k), w1_halves(w1_v),
                        w2_k.astype(BF16), w2_v.T.astype(BF16))

    n_cmp = (s - CMP_LEN) // CMP_STRIDE + 1
    starts = np.arange(nch) * CMP_STRIDE
    ss = np.arange(n_slc) * SLC_LEN
    ov = np.clip(np.minimum(starts[:, None] + CMP_LEN, ss[None, :] + SLC_LEN)
                 - np.maximum(starts[:, None], ss[None, :]), 0, None) / CMP_LEN
    ov[n_cmp:] = 0.0
    ovT = jnp.asarray(ov.T, BF16)
    slopes_np = (2.0 ** (-8.0 * np.arange(1, NSA_HEADS + 1) / NSA_HEADS)).astype(np.float32)
    slopes = jnp.asarray(slopes_np, F32)
    full = (slopes_np * np.float32(LOG2E)).astype(np.float32)
    pieces, rest = [], full.copy()
    for _ in range(3):
        piece = rest.astype(BF16).astype(np.float32)
        pieces.append(piece)
        rest = (rest - piece).astype(np.float32)
    tab = jnp.asarray(np.stack(pieces + [full], axis=1).reshape(-1), F32)

    ocT, bias = _cmp_topk(slopes, qT, kc, vcT, ovT, gT, min(N_SEL, n_slc))
    yT = _slc_win(tab, qT, ks, vsT, kw, vwT, bias, ocT, gT)
    return _post(x, None, yT, None, w_out.astype(BF16), _row(g_post), _row(g_mlp_pre),
                 w1.astype(BF16), w2.astype(BF16), _row(g_mlp_post))


def kernel(x, norm_mix_pre, norm_mix_post, norm_mlp_pre, norm_mlp_post, mlp_w1, mlp_w2, a_w_in, a_conv_w, a_q_norm,
           a_w_q_up, a_kv_norm, a_w_kv_up, a_w_out, c_w_in, c_cmp_pe_k, c_cmp_w1_k, c_cmp_w2_k, c_cmp_pe_v,
           c_cmp_w1_v, c_cmp_w2_v, c_w_out):
    depth = norm_mix_pre.shape[0]
    for layer in range(depth):
        i = layer // 2
        common = (norm_mix_post[layer], norm_mlp_pre[layer], mlp_w1[layer], mlp_w2[layer], norm_mlp_post[layer])
        if layer % 2 == 0:
            x = _layer_a(x, norm_mix_pre[layer], a_w_in[i], a_conv_w[i], a_q_norm[i], a_w_q_up[i], a_kv_norm[i],
                         a_w_kv_up[i], a_w_out[i], *common)
        else:
            x = _layer_c(x, norm_mix_pre[layer], c_w_in[i], c_cmp_pe_k[i], c_cmp_w1_k[i], c_cmp_w2_k[i],
                         c_cmp_pe_v[i], c_cmp_w1_v[i], c_cmp_w2_v[i], c_w_out[i], *common)
    return x
```

```python
import functools
import math

import numpy as np
import jax
import jax.numpy as jnp
from jax import lax
from jax.experimental import pallas as pl
from jax.experimental.pallas import tpu as pltpu

F32 = jnp.float32
BF16 = jnp.bfloat16

D_MODEL = 1024
D_FF = 4 * D_MODEL
EPS = 1e-6
NEG = -1e30
LOG2E = math.log2(math.e)

CONV_WIDTH = 512
CONV_K = 3
MLA_HEADS = 8
MLA_NOPE = 64
MLA_ROPE = 32
MLA_V = 64
MLA_KV_RANK = 256
MLA_Q_RANK = 768
MLA_SLOT = 128
ROPE_THETA = 10000.0
ROPE_HALF = MLA_ROPE // 2

NSA_HEADS = 16
NSA_GROUPS = 4
NSA_REP = 4
NSA_DH = 64
CMP_LEN = 32
CMP_STRIDE = 16
SLC_LEN = 64
SLC_SHIFT = 6
N_SEL = 16
WINDOW = 512
FORCE_BONUS = 1e4
KV_W = NSA_GROUPS * NSA_DH
C_MIX = NSA_HEADS * NSA_DH
GATE_ROWS = 16
ONES_ROWS = 16
V_ROWS = 64 + ONES_ROWS
KFEAT = 16

TOKEN_TILE = 512
ATTN_TILE = 256
MLA_HEADS_PER_STEP = 8
NSA_GROUPS_PER_STEP = 2
FF_CHUNK = 1024
VMEM_LIMIT = 56 * 1024 * 1024


def _cparams(sem):
    return pltpu.CompilerParams(dimension_semantics=sem, vmem_limit_bytes=VMEM_LIMIT)


def _const_spec(shape):
    nd = len(shape)
    return pl.BlockSpec(shape, lambda *_: (0,) * nd)


def _rms(x, g):
    ms = jnp.mean(x * x, axis=-1, keepdims=True)
    return x * lax.rsqrt(ms + EPS) * g


def _nt(a, b):
    return lax.dot_general(a, b, (((1,), (1,)), ((), ())), preferred_element_type=F32)


def _tn(a, b):
    return lax.dot_general(a, b, (((0,), (0,)), ((), ())), preferred_element_type=F32)


def _ones_rows(n):
    return jnp.where(lax.broadcasted_iota(jnp.int32, (ONES_ROWS, n), 0) == 0, 1.0, 0.0).astype(F32)


def _rope_rows(r1, r2, cos, sin):
    return r1 * cos - r2 * sin, r2 * cos + r1 * sin


def _a_pre_kernel(x_ref, g_ref, w1_ref, wkr_ref, convw_ref, qn_ref, wq_ref, kvn_ref, wk_ref, wv_ref,
                  cos_ref, sin_ref, yconv_ref, qT_ref, k_ref, vT_ref, carry_ref, *, tm, q_scale):
    s = pl.program_id(1)
    x = x_ref[0]
    xn = _rms(x, g_ref[...]).astype(BF16)
    p1 = jnp.dot(xn, w1_ref[...], preferred_element_type=F32)
    o1, o2, o3 = CONV_WIDTH, 2 * CONV_WIDTH, 3 * CONV_WIDTH
    o4 = o3 + MLA_Q_RANK
    g_b, g_c, hv = p1[:, :o1], p1[:, o1:o2], p1[:, o2:o3]
    c_q, c_kv = p1[:, o3:o4], p1[:, o4:]

    u = g_c * hv

    @pl.when(s == 0)
    def _():
        carry_ref[...] = jnp.zeros_like(carry_ref)

    w = convw_ref[...]
    w0, w1, w2 = w[0:1], w[1:2], w[2:3]
    y = w2 * u + w1 * pltpu.roll(u, 1, 0) + w0 * pltpu.roll(u, 2, 0)
    yconv_ref[0] = (g_b * y).astype(BF16)
    hd = 16
    c = carry_ref[...]
    uh = u[0:hd]
    row = lax.broadcasted_iota(jnp.int32, (hd, CONV_WIDTH), 0)
    h1 = jnp.where(row < 1, pltpu.roll(c, 1, 0), pltpu.roll(uh, 1, 0))
    h2 = jnp.where(row < 2, pltpu.roll(c, 2, 0), pltpu.roll(uh, 2, 0))
    yh = w2 * uh + w1 * h1 + w0 * h2
    yconv_ref[0, 0:hd, :] = (g_b[0:hd] * yh).astype(BF16)
    carry_ref[...] = u[tm - hd:tm]

    cos = cos_ref[...]
    sin = sin_ref[...]

    hq = _rms(c_q, qn_ref[...]).astype(BF16)
    qT = _nt(wq_ref[...], hq)
    z32 = jnp.zeros((MLA_SLOT - MLA_NOPE - MLA_ROPE, tm), F32)
    for h in range(MLA_HEADS):
        b0 = MLA_SLOT * h
        r1 = qT[b0 + MLA_NOPE:b0 + MLA_NOPE + ROPE_HALF]
        r2 = qT[b0 + MLA_NOPE + ROPE_HALF:b0 + MLA_NOPE + MLA_ROPE]
        e1, e2 = _rope_rows(r1, r2, cos, sin)
        slot = jnp.concatenate([qT[b0:b0 + MLA_NOPE], e1, e2, z32], axis=0) * q_scale
        qT_ref[0, b0:b0 + MLA_SLOT, :] = slot.astype(BF16)

    krT = _nt(wkr_ref[...], xn)
    r1 = krT[MLA_NOPE:MLA_NOPE + ROPE_HALF]
    r2 = krT[MLA_NOPE + ROPE_HALF:MLA_NOPE + MLA_ROPE]
    e1, e2 = _rope_rows(r1, r2, cos, sin)
    slotT = jnp.concatenate([jnp.zeros((MLA_NOPE, tm), F32), e1, e2, z32], axis=0)
    slot = slotT.T

    hkv = _rms(c_kv, kvn_ref[...]).astype(BF16)
    kn = jnp.dot(hkv, wk_ref[...], preferred_element_type=F32)
    k = kn + jnp.concatenate([slot] * MLA_HEADS, axis=1)
    k_ref[0] = k.astype(BF16)
    vT = _nt(wv_ref[...], hkv)
    ones = _ones_rows(tm)
    vT = jnp.concatenate([x for h in range(MLA_HEADS) for x in (vT[h * MLA_V:(h + 1) * MLA_V], ones)], axis=0)
    for cidx in range(tm // ATTN_TILE):
        vT_ref[0, cidx] = vT[:, cidx * ATTN_TILE:(cidx + 1) * ATTN_TILE].astype(BF16)


def _a_pre(x, g, w1, wkr, convw, qn, wq, kvn, wk, wv, cos, sin):
    b, s, d = x.shape
    tm = min(TOKEN_TILE, s)
    hw = MLA_HEADS * MLA_SLOT
    vw = MLA_HEADS * V_ROWS
    q_scale = (MLA_NOPE + MLA_ROPE) ** -0.5 * LOG2E
    kern = functools.partial(_a_pre_kernel, tm=tm, q_scale=q_scale)
    return pl.pallas_call(
        kern,
        grid=(b, s // tm),
        in_specs=[
            pl.BlockSpec((1, tm, d), lambda i, j: (i, j, 0)),
            _const_spec(g.shape), _const_spec(w1.shape), _const_spec(wkr.shape), _const_spec(convw.shape),
            _const_spec(qn.shape), _const_spec(wq.shape), _const_spec(kvn.shape), _const_spec(wk.shape),
            _const_spec(wv.shape),
            pl.BlockSpec((ROPE_HALF, tm), lambda i, j: (0, j)),
            pl.BlockSpec((ROPE_HALF, tm), lambda i, j: (0, j)),
        ],
        out_specs=[
            pl.BlockSpec((1, tm, CONV_WIDTH), lambda i, j: (i, j, 0)),
            pl.BlockSpec((1, hw, tm), lambda i, j: (i, 0, j)),
            pl.BlockSpec((1, tm, hw), lambda i, j: (i, j, 0)),
            pl.BlockSpec((1, tm // ATTN_TILE, vw, ATTN_TILE), lambda i, j: (i, j, 0, 0)),
        ],
        out_shape=[
            jax.ShapeDtypeStruct((b, s, CONV_WIDTH), BF16),
            jax.ShapeDtypeStruct((b, hw, s), BF16),
            jax.ShapeDtypeStruct((b, s, hw), BF16),
            jax.ShapeDtypeStruct((b, s // ATTN_TILE, vw, ATTN_TILE), BF16),
        ],
        scratch_shapes=[pltpu.VMEM((16, CONV_WIDTH), F32)],
        compiler_params=_cparams(("arbitrary", "arbitrary")),
        name="a_pre",
    )(x, g, w1, wkr, convw, qn, wq, kvn, wk, wv, cos, sin)


def _tile_pos(t, k0, q0):
    kpos = k0 + lax.broadcasted_iota(jnp.int32, (t, t), 0)
    qpos = q0 + lax.broadcasted_iota(jnp.int32, (t, t), 1)
    return kpos, qpos


def _stage_scores(s, s_slot):
    s_slot[...] = s
    return jnp.max(s, axis=0, keepdims=True)


def _stage_softmax(s_slot, p_slot, m, mt):
    m_new = jnp.maximum(m, mt)
    p_slot[...] = jnp.exp2(s_slot[...] - m_new).astype(BF16)
    return m_new, jnp.exp2(m - m_new)


def _stage_pv(acc_slot, vT, p_slot, alpha):
    acc_slot[...] = alpha * acc_slot[...] + jnp.dot(vT, p_slot[...], preferred_element_type=F32)


def _causal_flash(i, n_rest, tile_at, t, chains, scores, vtile, diag_extra, s_ref, p_ref, acc_ref):
    mt0 = []
    for c in chains:
        mt0.append(_stage_scores(scores(c, i) + diag_extra, s_ref.at[c]))
        p_ref[c] = jnp.zeros((t, t), BF16)
        acc_ref[c] = jnp.zeros((V_ROWS, t), F32)
    init = (tuple(mt0),
            tuple(jnp.ones((1, t), F32) for _ in chains),
            tuple(jnp.full((1, t), NEG, F32) for _ in chains))
    last = jnp.maximum(n_rest - 1, 0)

    def body(n, carry):
        mt, a_prev, m = carry
        j_prev = jnp.where(n == 1, i, tile_at(jnp.maximum(n - 2, 0)))
        j_next = tile_at(jnp.minimum(n, last))
        for c in chains:
            _stage_pv(acc_ref.at[c], vtile(c, j_prev), p_ref.at[c], a_prev[c])
        m_new, alpha = [], []
        for c in chains:
            mn, al = _stage_softmax(s_ref.at[c], p_ref.at[c], m[c], mt[c])
            m_new.append(mn)
            alpha.append(al)
        mt_next = tuple(_stage_scores(scores(c, j_next), s_ref.at[c]) for c in chains)
        return mt_next, tuple(alpha), tuple(m_new)

    _, a_prev, _ = lax.fori_loop(0, n_rest + 1, body, init)
    j_last = jnp.where(n_rest == 0, i, tile_at(last))
    for c in chains:
        _stage_pv(acc_ref.at[c], vtile(c, j_last), p_ref.at[c], a_prev[c])


def _normalized(acc_slot):
    acc = acc_slot[...]
    return acc[:V_ROWS - ONES_ROWS] / acc[V_ROWS - ONES_ROWS:V_ROWS - ONES_ROWS + 1]


def _mla_attn_kernel(qT_ref, k_ref, vT_ref, oT_ref, s_ref, p_ref, acc_ref, *, t, hb):
    i = pl.program_id(2)

    def scores(h, j):
        k = k_ref[0, pl.ds(pl.multiple_of(j * t, t), t), h * MLA_SLOT:(h + 1) * MLA_SLOT]
        return jnp.dot(k, qT_ref[0, h * MLA_SLOT:(h + 1) * MLA_SLOT, :], preferred_element_type=F32)

    def vtile(h, j):
        return vT_ref[0, j, h * V_ROWS:(h + 1) * V_ROWS, :]

    kpos, qpos = _tile_pos(t, 0, 0)
    causal = jnp.where(kpos <= qpos, 0.0, NEG)
    _causal_flash(i, i, lambda n: n, t, range(hb), scores, vtile, causal, s_ref, p_ref, acc_ref)
    for h in range(hb):
        oT_ref[0, h * MLA_V:(h + 1) * MLA_V, :] = _normalized(acc_ref.at[h]).astype(BF16)


def _mla_attn(qT, k, vT):
    b, hw, s = qT.shape
    t = min(ATTN_TILE, s)
    nkv = s // t
    hb = MLA_HEADS_PER_STEP
    kern = functools.partial(_mla_attn_kernel, t=t, hb=hb)
    return pl.pallas_call(
        kern,
        grid=(b, MLA_HEADS // hb, s // t),
        in_specs=[
            pl.BlockSpec((1, hb * MLA_SLOT, t), lambda bi, h, i: (bi, h, i)),
            pl.BlockSpec((1, s, hb * MLA_SLOT), lambda bi, h, i: (bi, 0, h)),
            pl.BlockSpec((1, nkv, hb * V_ROWS, t), lambda bi, h, i: (bi, 0, h, 0)),
        ],
        out_specs=pl.BlockSpec((1, hb * MLA_V, t), lambda bi, h, i: (bi, h, i)),
        out_shape=jax.ShapeDtypeStruct((b, MLA_HEADS * MLA_V, s), BF16),
        scratch_shapes=[pltpu.VMEM((hb, t, t), F32), pltpu.VMEM((hb, t, t), BF16),
                        pltpu.VMEM((hb, V_ROWS, t), F32)],
        compiler_params=_cparams(("arbitrary", "arbitrary", "arbitrary")),
        name="mla_attn",
    )(qT, k, vT)


def _post_kernel(*refs, has_tok):
    if has_tok:
        x_ref, ytok_ref, yT_ref, wo_tok_ref, wo_ref, gpost_ref, gpre_ref, w1_ref, w2_ref, gmlp_ref, o_ref = refs
    else:
        x_ref, yT_ref, wo_ref, gpost_ref, gpre_ref, w1_ref, w2_ref, gmlp_ref, o_ref = refs
    x = x_ref[0]
    mix = _tn(yT_ref[0], wo_ref[...])
    if has_tok:
        mix = mix + jnp.dot(ytok_ref[0], wo_tok_ref[...], preferred_element_type=F32)
    x1 = x + _rms(mix, gpost_ref[...])
    h = _rms(x1, gpre_ref[...]).astype(BF16)
    acc = jnp.zeros_like(x1)
    for c in range(D_FF // FF_CHUNK):
        a = jnp.dot(h, w1_ref[:, c * FF_CHUNK:(c + 1) * FF_CHUNK], preferred_element_type=F32)
        a = jnp.maximum(a, 0.0)
        a = (a * a).astype(BF16)
        acc = acc + jnp.dot(a, w2_ref[c * FF_CHUNK:(c + 1) * FF_CHUNK, :], preferred_element_type=F32)
    o_ref[0] = x1 + _rms(acc, gmlp_ref[...])


def _post(x, ytok, yT, wo_tok, wo, gpost, gpre, w1, w2, gmlp):
    b, s, d = x.shape
    tm = min(TOKEN_TILE, s)
    has_tok = ytok is not None
    kern = functools.partial(_post_kernel, has_tok=has_tok)
    xspec = pl.BlockSpec((1, tm, d), lambda i, j: (i, j, 0))
    ins, specs = [x], [xspec]
    if has_tok:
        ins.append(ytok)
        specs.append(pl.BlockSpec((1, tm, ytok.shape[-1]), lambda i, j: (i, j, 0)))
    ins.append(yT)
    specs.append(pl.BlockSpec((1, yT.shape[1], tm), lambda i, j: (i, 0, j)))
    if has_tok:
        ins.append(wo_tok)
        specs.append(_const_spec(wo_tok.shape))
    for a in (wo, gpost, gpre, w1, w2, gmlp):
        ins.append(a)
        specs.append(_const_spec(a.shape))
    return pl.pallas_call(
        kern,
        grid=(b, s // tm),
        in_specs=specs,
        out_specs=xspec,
        out_shape=jax.ShapeDtypeStruct(x.shape, x.dtype),
        compiler_params=_cparams(("arbitrary", "arbitrary")),
        name="post_tok" if has_tok else "post",
    )(*ins)


def _c_pre_kernel(x_ref, g_ref, wq_ref, wtok_ref, wf_ref, kfs_ref, kfw_ref, qT_ref, kc_ref, vc_ref, ks_ref, kw_ref,
                  vsT_ref, vwT_ref, gT_ref, *, tm, q_scale):
    xn = _rms(x_ref[0], g_ref[...]).astype(BF16)
    qT_ref[0] = (_nt(wq_ref[...], xn) * q_scale).astype(BF16)
    tok = jnp.dot(xn, wtok_ref[...], preferred_element_type=F32)
    kc_ref[0] = tok[:, :KV_W].astype(BF16)
    vc_ref[0] = tok[:, KV_W:2 * KV_W].astype(BF16)
    kfs = kfs_ref[...]
    kfw = kfw_ref[...]
    for g in range(NSA_GROUPS):
        a = 2 * KV_W + g * NSA_DH
        ks_ref[0, g] = jnp.concatenate([tok[:, a:a + NSA_DH], kfs], axis=1).astype(BF16)
        kw_ref[0, g] = jnp.concatenate([tok[:, a + KV_W:a + KV_W + NSA_DH], kfw], axis=1).astype(BF16)
    fT = _nt(wf_ref[...], xn)
    ones = _ones_rows(tm)

    def with_ones(v):
        return jnp.concatenate([x for g in range(NSA_GROUPS) for x in (v[g * NSA_DH:(g + 1) * NSA_DH], ones)],
                               axis=0)

    vs = with_ones(fT[:KV_W])
    vw = with_ones(fT[KV_W:2 * KV_W])
    for cidx in range(tm // ATTN_TILE):
        sl = slice(cidx * ATTN_TILE, (cidx + 1) * ATTN_TILE)
        vsT_ref[0, cidx] = vs[:, sl].astype(BF16)
        vwT_ref[0, cidx] = vw[:, sl].astype(BF16)
    gT_ref[0] = 1.0 / (1.0 + jnp.exp(-fT[2 * KV_W:]))


def _c_pre(x, g, wq, wtok, wf, kfs, kfw):
    b, s, d = x.shape
    tm = min(TOKEN_TILE, s)
    q_scale = NSA_DH ** -0.5 * LOG2E
    kern = functools.partial(_c_pre_kernel, tm=tm, q_scale=q_scale)
    nt = tm // ATTN_TILE
    grows = NSA_GROUPS * GATE_ROWS
    kaug = NSA_DH + kfs.shape[1]
    vrows = NSA_GROUPS * V_ROWS
    kf_spec = pl.BlockSpec((tm, kfs.shape[1]), lambda i, j: (j, 0))
    return pl.pallas_call(
        kern,
        grid=(b, s // tm),
        in_specs=[pl.BlockSpec((1, tm, d), lambda i, j: (i, j, 0)),
                  _const_spec(g.shape), _const_spec(wq.shape), _const_spec(wtok.shape), _const_spec(wf.shape),
                  kf_spec, kf_spec],
        out_specs=[
            pl.BlockSpec((1, C_MIX, tm), lambda i, j: (i, 0, j)),
            pl.BlockSpec((1, tm, KV_W), lambda i, j: (i, j, 0)),
            pl.BlockSpec((1, tm, KV_W), lambda i, j: (i, j, 0)),
            pl.BlockSpec((1, NSA_GROUPS, tm, kaug), lambda i, j: (i, 0, j, 0)),
            pl.BlockSpec((1, NSA_GROUPS, tm, kaug), lambda i, j: (i, 0, j, 0)),
            pl.BlockSpec((1, nt, vrows, ATTN_TILE), lambda i, j: (i, j, 0, 0)),
            pl.BlockSpec((1, nt, vrows, ATTN_TILE), lambda i, j: (i, j, 0, 0)),
            pl.BlockSpec((1, grows, tm), lambda i, j: (i, 0, j)),
        ],
        out_shape=[
            jax.ShapeDtypeStruct((b, C_MIX, s), BF16),
            jax.ShapeDtypeStruct((b, s, KV_W), BF16),
            jax.ShapeDtypeStruct((b, s, KV_W), BF16),
            jax.ShapeDtypeStruct((b, NSA_GROUPS, s, kaug), BF16),
            jax.ShapeDtypeStruct((b, NSA_GROUPS, s, kaug), BF16),
            jax.ShapeDtypeStruct((b, s // ATTN_TILE, vrows, ATTN_TILE), BF16),
            jax.ShapeDtypeStruct((b, s // ATTN_TILE, vrows, ATTN_TILE), BF16),
            jax.ShapeDtypeStruct((b, grows, s), F32),
        ],
        compiler_params=_cparams(("arbitrary", "arbitrary")),
        name="c_pre",
    )(x, g, wq, wtok, wf, kfs, kfw)


def _gelu_tanh(x):
    return 0.5 * x * (1.0 + jnp.tanh(math.sqrt(2.0 / math.pi) * (x + 0.044715 * (x * x * x))))


def _compress_kernel(zk_ref, zv_ref, pek_ref, pev_ref, wak_ref, wbk_ref, wav_ref, wbv_ref, w2k_ref, w2vT_ref, kf_ref,
                     kc_ref, vcT_ref, *, nch):
    def hidden(z_ref, pe_ref, wa_ref, wb_ref):
        z = z_ref[0].astype(F32)
        za = (z + pe_ref[0:1]).astype(BF16)
        zb = (z + pe_ref[1:2]).astype(BF16)
        a = jnp.dot(za, wa_ref[...], preferred_element_type=F32)
        bm = jnp.dot(zb, wb_ref[...], preferred_element_type=F32)
        return _gelu_tanh(a + pltpu.roll(bm, nch - 1, 0)).astype(BF16)

    hk = hidden(zk_ref, pek_ref, wak_ref, wbk_ref)
    kc = jnp.dot(hk, w2k_ref[...], preferred_element_type=F32)
    kf = kf_ref[...]
    hv = hidden(zv_ref, pev_ref, wav_ref, wbv_ref)
    vcT = _nt(w2vT_ref[...], hv)
    ones = _ones_rows(nch)
    for g in range(NSA_GROUPS):
        gs = slice(g * NSA_DH, (g + 1) * NSA_DH)
        kc_ref[0, g] = jnp.concatenate([kc[:, gs], kf], axis=1).astype(BF16)
        vcT_ref[0, g] = jnp.concatenate([vcT[gs], ones], axis=0).astype(BF16)


def _compress(zk, zv, pek, pev, wak, wbk, wav, wbv, w2k, w2vT, kf):
    b, nch, cw = zk.shape
    kern = functools.partial(_compress_kernel, nch=nch)
    zspec = pl.BlockSpec((1, nch, cw), lambda i: (i, 0, 0))
    consts = (pek, pev, wak, wbk, wav, wbv, w2k, w2vT, kf)
    return pl.pallas_call(
        kern,
        grid=(b,),
        in_specs=[zspec, zspec] + [_const_spec(a.shape) for a in consts],
        out_specs=[pl.BlockSpec((1, NSA_GROUPS, nch, NSA_DH + KFEAT), lambda i: (i, 0, 0, 0)),
                   pl.BlockSpec((1, NSA_GROUPS, V_ROWS, nch), lambda i: (i, 0, 0, 0))],
        out_shape=[jax.ShapeDtypeStruct((b, NSA_GROUPS, nch, NSA_DH + KFEAT), BF16),
                   jax.ShapeDtypeStruct((b, NSA_GROUPS, V_ROWS, nch), BF16)],
        compiler_params=_cparams(("arbitrary",)),
        name="compress",
    )(zk, zv, *consts)


def _split3(x):
    hi = x.astype(BF16)
    r = x - hi.astype(F32)
    mid = r.astype(BF16)
    lo = (r - mid.astype(F32)).astype(BF16)
    return hi, mid, lo


def _feature_rows(tab_ref, head, const, t):
    row = lax.broadcasted_iota(jnp.int32, (KFEAT, t), 0)
    r3 = row - 3 * (row >= 3).astype(jnp.int32) - 3 * (row >= 6).astype(jnp.int32)
    s_hi, s_mid, s_lo, slope = (tab_ref[head * 4 + k] for k in range(4))
    c_hi, c_mid, c_lo = (x.astype(F32) for x in _split3(slope * const))
    svals = jnp.where(r3 == 0, s_hi, jnp.where(r3 == 1, s_mid, s_lo))
    cvals = jnp.where(r3 == 0, c_hi, jnp.where(r3 == 1, c_mid, c_lo))
    return jnp.where(row < 6, svals, jnp.where(row < 9, cvals, 0.0)).astype(BF16)


def _cmp_topk_kernel(tab_ref, qT_ref, kc_ref, vcT_ref, ovT_ref, pool_ref, gT_ref, ocT_ref, bias_ref, act_ref,
                     qa_ref, e_ref, p_ref, acc_ref, imp_ref, rank_ref, *, t, gb, nch, nslc, nsel):
    gp = pl.program_id(1)
    i = pl.program_id(2)
    q0 = i * t
    chains = range(gb * NSA_REP)
    bpt = t // SLC_LEN

    const = (CMP_LEN - 1) / 2.0 - jnp.full((KFEAT, t), q0, jnp.int32).astype(F32)
    for c in chains:
        head = (gp * gb + c // NSA_REP) * NSA_REP + c % NSA_REP
        qa_ref[c, 0:NSA_DH, :] = qT_ref[0, c * NSA_DH:(c + 1) * NSA_DH, :]
        qa_ref[c, NSA_DH:NSA_DH + KFEAT, :] = _feature_rows(tab_ref, head, const, t)

    n_i = lax.broadcasted_iota(jnp.int32, (nch, t), 0)
    t_i = q0 + lax.broadcasted_iota(jnp.int32, (nch, t), 1)
    ended = jnp.where(n_i * CMP_STRIDE + (CMP_LEN - 1) <= t_i, 0.0, NEG)
    mt = [_stage_scores(jnp.dot(kc_ref[0, c // NSA_REP], qa_ref[c], preferred_element_type=F32) + ended,
                        e_ref.at[c]) for c in chains]
    for c in chains:
        m_eff = jnp.where(mt[c] > 0.5 * NEG, mt[c], 0.0)
        e = jnp.exp2(e_ref[c] - m_eff)
        e_ref[c] = e
        p_ref[c] = e.astype(BF16)
    for c in chains:
        acc_ref[c] = jnp.dot(vcT_ref[0, c // NSA_REP], p_ref[c], preferred_element_type=F32)

    j_i = lax.broadcasted_iota(jnp.int32, (nslc, t), 0)
    cur = jnp.right_shift(q0 + lax.broadcasted_iota(jnp.int32, (nslc, t), 1), SLC_SHIFT)
    forced = (j_i == 0) | (j_i == cur) | (j_i == cur - 1)
    ovT = ovT_ref[...]
    for gl in range(gb):
        psum = jnp.zeros((nch, t), F32)
        for r in range(NSA_REP):
            c = gl * NSA_REP + r
            acc = acc_ref[c]
            l = acc[NSA_DH:NSA_DH + 1]
            inv = 1.0 / jnp.where(l > 0.0, l, 1.0)
            gate = gT_ref[0, gl * GATE_ROWS + r:gl * GATE_ROWS + r + 1, :]
            ocT_ref[0, c * NSA_DH:(c + 1) * NSA_DH, :] = (gate * (acc[:NSA_DH] * inv)).astype(BF16)
            psum = psum + e_ref[c] * inv
        imp = sum(jnp.dot(ovT, part, preferred_element_type=F32) for part in _split3(psum))
        imp_ref[gl] = jnp.where(j_i > cur, NEG, imp + jnp.where(forced, FORCE_BONUS, 0.0))
        rank_ref[gl] = jnp.zeros((nslc, t), F32)

    for chunk in range(nslc // bpt):
        @pl.when(chunk <= i)
        def _():
            for gl in range(gb):
                imp = imp_ref[gl]
                rank = rank_ref[gl]
                for jp in range(chunk * bpt, (chunk + 1) * bpt):
                    row = imp[jp:jp + 1, :]
                    incs = []
                    for rg in range(nslc // 8):
                        blk = imp[8 * rg:8 * rg + 8]
                        if 8 * rg > jp:
                            incs.append(jnp.where(row >= blk, 1.0, 0.0))
                        elif 8 * rg + 7 <= jp:
                            incs.append(jnp.where(row > blk, 1.0, 0.0))
                        else:
                            above = lax.broadcasted_iota(jnp.int32, (8, t), 0) > jp - 8 * rg
                            incs.append(jnp.where(above, jnp.where(row >= blk, 1.0, 0.0),
                                                  jnp.where(row > blk, 1.0, 0.0)))
                    rank = rank + jnp.concatenate(incs, axis=0)
                rank_ref[gl] = rank

    sel = jnp.zeros((nslc, t), F32)
    for gl in range(gb):
        bias = jnp.where(j_i > cur, NEG, jnp.where(rank_ref[gl] < nsel, 0.0, NEG))
        bias_ref[0, gl] = bias
        sel = jnp.maximum(sel, jnp.where(bias == 0.0, 1.0, 0.0))
    counts = _nt(jnp.ones((8, t), BF16), sel.astype(BF16))
    hit = jnp.where(counts > 0.0, 1.0, 0.0).astype(BF16)
    tiles = jnp.dot(hit, pool_ref[...], preferred_element_type=F32)
    act_ref[0, 0, 0] = jnp.where(tiles > 0.0, 1, 0).astype(jnp.int32)


def _cmp_topk(tab, qT, kc, vcT, ovT, gT, nsel):
    b, _, s = qT.shape
    t = ATTN_TILE
    assert s % t == 0 and t % SLC_LEN == 0
    nch = kc.shape[2]
    nslc = ovT.shape[0]
    gb = NSA_GROUPS_PER_STEP
    nc = gb * NSA_REP
    kern = functools.partial(_cmp_topk_kernel, t=t, gb=gb, nch=nch, nslc=nslc, nsel=nsel)
    gw = nc * NSA_DH
    pool = np.zeros((nslc, 128), np.float32)
    pool[np.arange(nslc), np.arange(nslc) // (t // SLC_LEN)] = 1.0
    pool = jnp.asarray(pool, BF16)
    return pl.pallas_call(
        kern,
        grid=(b, NSA_GROUPS // gb, s // t),
        in_specs=[
            pl.BlockSpec(memory_space=pltpu.SMEM),
            pl.BlockSpec((1, gw, t), lambda bi, g, i: (bi, g, i)),
            pl.BlockSpec((1, gb, nch, NSA_DH + KFEAT), lambda bi, g, i: (bi, g, 0, 0)),
            pl.BlockSpec((1, gb, V_ROWS, nch), lambda bi, g, i: (bi, g, 0, 0)),
            _const_spec(ovT.shape),
            _const_spec(pool.shape),
            pl.BlockSpec((1, gb * GATE_ROWS, t), lambda bi, g, i: (bi, g, i)),
        ],
        out_specs=[pl.BlockSpec((1, gw, t), lambda bi, g, i: (bi, g, i)),
                   pl.BlockSpec((1, gb, nslc, t), lambda bi, g, i: (bi, g, 0, i)),
                   pl.BlockSpec((1, 1, 1, 8, 128), lambda bi, g, i: (bi, g, i, 0, 0))],
        out_shape=[jax.ShapeDtypeStruct((b, C_MIX, s), BF16),
                   jax.ShapeDtypeStruct((b, NSA_GROUPS, nslc, s), F32),
                   jax.ShapeDtypeStruct((b, NSA_GROUPS // gb, s // t, 8, 128), jnp.int32)],
        scratch_shapes=[pltpu.VMEM((nc, NSA_DH + KFEAT, t), BF16),
                        pltpu.VMEM((nc, nch, t), F32), pltpu.VMEM((nc, nch, t), BF16),
                        pltpu.VMEM((nc, V_ROWS, t), F32),
                        pltpu.VMEM((gb, nslc, t), F32), pltpu.VMEM((gb, nslc, t), F32)],
        compiler_params=_cparams(("arbitrary", "arbitrary", "arbitrary")),
        name="cmp_topk",
    )(tab, qT, kc, vcT, ovT, pool, gT)


def _slc_win_kernel(act_ref, tab_ref, qT_ref, ks_ref, vsT_ref, kw_ref, vwT_ref, bias_ref, ocT_ref, gT_ref, yT_ref,
                    qa_ref, s_ref, p_ref, acc_ref, sw_ref, pw_ref, accw_ref, idx_ref, *, t, gb, nslc, nkv):
    gp = pl.program_id(1)
    i = pl.program_id(2)
    chains = range(gb * NSA_REP)
    feat0 = NSA_DH + nslc

    const = -jnp.full((KFEAT, t), i * t, jnp.int32).astype(F32)
    for c in chains:
        gl, r = divmod(c, NSA_REP)
        head = (gp * gb + gl) * NSA_REP + r
        qa_ref[c, 0:NSA_DH, :] = qT_ref[0, c * NSA_DH:(c + 1) * NSA_DH, :]
        qa_ref[c, NSA_DH:feat0, :] = bias_ref[0, gl].astype(BF16)
        qa_ref[c, feat0:feat0 + KFEAT, :] = _feature_rows(tab_ref, head, const, t)

    def scores_from(k_ref):
        def scores(c, j):
            k = k_ref[0, c // NSA_REP, pl.ds(pl.multiple_of(j * t, t), t), :]
            return jnp.dot(k, qa_ref[c], preferred_element_type=F32)
        return scores

    def vtile_from(vT_ref):
        def vtile(c, j):
            gl = c // NSA_REP
            return vT_ref[0, j, gl * V_ROWS:(gl + 1) * V_ROWS, :]
        return vtile

    kpos, qpos = _tile_pos(t, 0, 0)
    causal = jnp.where(kpos <= qpos, 0.0, NEG)

    w_scores = scores_from(kw_ref)
    w_vtile = vtile_from(vwT_ref)
    w_tiles = (i, jnp.maximum(i - 1, 0), jnp.maximum(i - 2, 0))
    in_seq1 = jnp.where((i - 1) * t + kpos >= 0, 0.0, NEG)
    in_seq2 = jnp.where((i - 2) * t + kpos >= 0, 0.0, NEG)
    w_extra = (causal, in_seq1, jnp.where(qpos < kpos, in_seq2, NEG))
    mt = [[_stage_scores(w_scores(c, w_tiles[w]) + w_extra[w], sw_ref.at[w, c]) for c in chains] for w in range(3)]
    m = [jnp.full((1, t), NEG, F32) for _ in chains]
    for c in chains:
        accw_ref[c] = jnp.zeros((V_ROWS, t), F32)
    for w in range(3):
        for c in chains:
            m[c], alpha = _stage_softmax(sw_ref.at[w, c], pw_ref.at[w, c], m[c], mt[w][c])
            _stage_pv(accw_ref.at[c], w_vtile(c, w_tiles[w]), pw_ref.at[w, c], alpha)

    base = ((pl.program_id(0) * pl.num_programs(1) + gp) * pl.num_programs(2) + i) * nkv
    n_act = jnp.int32(0)
    for j in range(nkv):
        idx_ref[n_act] = j
        n_act = n_act + jnp.where((j < i) & (act_ref[base + j] != 0), 1, 0)
    _causal_flash(i, n_act, lambda n: idx_ref[n], t, chains, scores_from(ks_ref), vtile_from(vsT_ref), causal,
                  s_ref, p_ref, acc_ref)

    for c in chains:
        gl, r = divmod(c, NSA_REP)
        g0 = gl * GATE_ROWS
        rs = slice(c * NSA_DH, (c + 1) * NSA_DH)
        y = (ocT_ref[0, rs, :].astype(F32)
             + gT_ref[0, g0 + NSA_REP + r:g0 + NSA_REP + r + 1, :] * _normalized(acc_ref.at[c])
             + gT_ref[0, g0 + 2 * NSA_REP + r:g0 + 2 * NSA_REP + r + 1, :] * _normalized(accw_ref.at[c]))
        yT_ref[0, rs, :] = y.astype(BF16)


def _slc_win(act, tab, qT, ks, vsT, kw, vwT, bias, ocT, gT):
    b, _, s = qT.shape
    t = ATTN_TILE
    assert s % t == 0 and WINDOW == 2 * t
    nkv = s // t
    nslc = bias.shape[2]
    kaug = ks.shape[3]
    gb = NSA_GROUPS_PER_STEP
    nc = gb * NSA_REP
    gw = nc * NSA_DH
    kern = functools.partial(_slc_win_kernel, t=t, gb=gb, nslc=nslc, nkv=nkv)
    qspec = pl.BlockSpec((1, gw, t), lambda bi, g, i, *_: (bi, g, i))
    kspec = pl.BlockSpec((1, gb, s, kaug), lambda bi, g, i, *_: (bi, g, 0, 0))
    vspec = pl.BlockSpec((1, nkv, gb * V_ROWS, t), lambda bi, g, i, *_: (bi, 0, g, 0))
    grid_spec = pltpu.PrefetchScalarGridSpec(
        num_scalar_prefetch=2,
        grid=(b, NSA_GROUPS // gb, s // t),
        in_specs=[
            qspec, kspec, vspec, kspec, vspec,
            pl.BlockSpec((1, gb, nslc, t), lambda bi, g, i, *_: (bi, g, 0, i)),
            qspec,
            pl.BlockSpec((1, gb * GATE_ROWS, t), lambda bi, g, i, *_: (bi, g, i)),
        ],
        out_specs=qspec,
        scratch_shapes=[pltpu.VMEM((nc, kaug, t), BF16),
                        pltpu.VMEM((nc, t, t), F32), pltpu.VMEM((nc, t, t), BF16), pltpu.VMEM((nc, V_ROWS, t), F32),
                        pltpu.VMEM((3, nc, t, t), F32), pltpu.VMEM((3, nc, t, t), BF16),
                        pltpu.VMEM((nc, V_ROWS, t), F32),
                        pltpu.SMEM((nkv + 1,), jnp.int32)],
    )
    return pl.pallas_call(
        kern,
        grid_spec=grid_spec,
        out_shape=jax.ShapeDtypeStruct((b, C_MIX, s), BF16),
        compiler_params=_cparams(("arbitrary", "arbitrary", "arbitrary")),
        name="slc_win",
    )(act, tab, qT, ks, vsT, kw, vwT, bias, ocT, gT)


def _row(v):
    return v.reshape(1, -1).astype(F32)


def _layer_a(x, g_pre, w_in, conv_w, q_norm, w_q_up, kv_norm, w_kv_up, w_out, g_post, g_mlp_pre, w1, w2, g_mlp_post):
    b, s, d = x.shape
    o3 = 3 * CONV_WIDTH
    o5 = o3 + MLA_Q_RANK + MLA_KV_RANK
    pad = MLA_SLOT - MLA_NOPE - MLA_ROPE
    w_main = w_in[:, :o5].astype(BF16)
    wkr = jnp.pad(w_in[:, o5:].T, ((MLA_NOPE, pad), (0, 0))).astype(BF16)
    wq = w_q_up.reshape(MLA_Q_RANK, MLA_HEADS, MLA_NOPE + MLA_ROPE)
    wq = jnp.pad(wq, ((0, 0), (0, 0), (0, pad))).reshape(MLA_Q_RANK, MLA_HEADS * MLA_SLOT).T.astype(BF16)
    wkv = w_kv_up.reshape(MLA_KV_RANK, MLA_HEADS, MLA_NOPE + MLA_V)
    wk = jnp.pad(wkv[:, :, :MLA_NOPE], ((0, 0), (0, 0), (0, MLA_SLOT - MLA_NOPE)))
    wk = wk.reshape(MLA_KV_RANK, MLA_HEADS * MLA_SLOT).astype(BF16)
    wv = wkv[:, :, MLA_NOPE:].reshape(MLA_KV_RANK, MLA_HEADS * MLA_V).T.astype(BF16)

    inv = ROPE_THETA ** (-jnp.arange(ROPE_HALF, dtype=F32) / ROPE_HALF)
    ang = inv[:, None] * jnp.arange(s, dtype=F32)[None, :]
    cos, sin = jnp.cos(ang), jnp.sin(ang)

    yconv, qT, k, vT = _a_pre(x, _row(g_pre), w_main, wkr, conv_w.astype(F32), _row(q_norm), wq, _row(kv_norm),
                              wk, wv, cos, sin)
    yT = _mla_attn(qT, k, vT)
    wo = w_out.astype(BF16)
    return _post(x, yconv, yT, wo[:CONV_WIDTH], wo[CONV_WIDTH:], _row(g_post), _row(g_mlp_pre),
                 w1.astype(BF16), w2.astype(BF16), _row(g_mlp_post))


def _layer_c(x, g_pre, w_in, pe_k, w1_k, w2_k, pe_v, w1_v, w2_v, w_out, g_post, g_mlp_pre, w1, w2, g_mlp_post):
    b, s, d = x.shape
    G, R, Dh = NSA_GROUPS, NSA_REP, NSA_DH
    wq = w_in[:, :C_MIX].T.astype(BF16)
    kv = w_in[:, C_MIX:C_MIX + 6 * KV_W].reshape(d, 6, KV_W)
    wtok = jnp.concatenate([kv[:, 0], kv[:, 1], kv[:, 2], kv[:, 4]], axis=1).astype(BF16)
    wg = w_in[:, C_MIX + 6 * KV_W:].reshape(d, 3, G, R).transpose(0, 2, 1, 3)
    wg = jnp.pad(wg.reshape(d, G, 3 * R), ((0, 0), (0, 0), (0, GATE_ROWS - 3 * R))).reshape(d, G * GATE_ROWS)
    wf = jnp.concatenate([kv[:, 3], kv[:, 5], wg], axis=1).T.astype(BF16)

    n_slc = s // SLC_LEN
    pos = np.arange(s)
    blk, within = pos // SLC_LEN, pos % SLC_LEN
    feats = np.zeros((s, KFEAT), np.float32)
    feats[:, 0:3] = (blk * SLC_LEN)[:, None]
    feats[:, 3:6] = within[:, None]
    feats[:, 6:9] = 1.0
    onehot = (blk[:, None] == np.arange(n_slc)[None, :]).astype(np.float32)
    kfs = jnp.asarray(np.concatenate([onehot, feats], axis=1), F32)
    kfw = jnp.asarray(np.concatenate([np.zeros_like(onehot), feats], axis=1), F32)

    qT, kc_tok, vc_tok, ks, kw, vsT, vwT, gT = _c_pre(x, _row(g_pre), wq, wtok, wf, kfs, kfw)

    nch = s // CMP_STRIDE
    half = CMP_LEN // 2
    eye = jnp.eye(G, dtype=F32)

    def w1_halves(w):
        bd = jnp.einsum('hlde,gk->hlgdke', w.reshape(2, half, Dh, Dh), eye)
        bd = bd.reshape(2, half * KV_W, KV_W).astype(BF16)
        return bd[0], bd[1]

    def pe_halves(pe):
        return jnp.tile(pe.reshape(2, half, 1, Dh), (1, 1, G, 1)).reshape(2, half * KV_W).astype(F32)

    def w2_bd(w):
        return jnp.einsum('de,gk->gdke', w, eye).reshape(KV_W, KV_W)

    n_idx = np.arange(nch)
    cfeat = np.zeros((nch, KFEAT), np.float32)
    cfeat[:, 0:3] = (1024 * (n_idx // 64))[:, None]
    cfeat[:, 3:6] = (CMP_STRIDE * (n_idx % 64))[:, None]
    cfeat[:, 6:9] = 1.0
    kc, vcT = _compress(kc_tok.reshape(b, nch, CMP_STRIDE * KV_W), vc_tok.reshape(b, nch, CMP_STRIDE * KV_W),
                        pe_halves(pe_k), pe_halves(pe_v), *w1_halves(w1_k), *w1_halves(w1_v),
                        w2_bd(w2_k).astype(BF16), w2_bd(w2_v).T.astype(BF16), jnp.asarray(cfeat, F32))

    n_cmp = (s - CMP_LEN) // CMP_STRIDE + 1
    starts = np.arange(nch) * CMP_STRIDE
    ss = np.arange(n_slc) * SLC_LEN
    ov = np.clip(np.minimum(starts[:, None] + CMP_LEN, ss[None, :] + SLC_LEN)
                 - np.maximum(starts[:, None], ss[None, :]), 0, None) / CMP_LEN
    ov[n_cmp:] = 0.0
    ovT = jnp.asarray(ov.T, BF16)
    slopes_np = (2.0 ** (-8.0 * np.arange(1, NSA_HEADS + 1) / NSA_HEADS)).astype(np.float32)
    full = (slopes_np * np.float32(LOG2E)).astype(np.float32)
    pieces, rest = [], full.copy()
    for _ in range(3):
        piece = rest.astype(BF16).astype(np.float32)
        pieces.append(piece)
        rest = (rest - piece).astype(np.float32)
    tab = jnp.asarray(np.stack(pieces + [full], axis=1).reshape(-1), F32)

    ocT, bias, act = _cmp_topk(tab, qT, kc, vcT, ovT, gT, min(N_SEL, n_slc))
    act = act[:, :, :, 0, :s // ATTN_TILE].reshape(-1)
    yT = _slc_win(act, tab, qT, ks, vsT, kw, vwT, bias, ocT, gT)
    return _post(x, None, yT, None, w_out.astype(BF16), _row(g_post), _row(g_mlp_pre),
                 w1.astype(BF16), w2.astype(BF16), _row(g_mlp_post))


def kernel(x, norm_mix_pre, norm_mix_post, norm_mlp_pre, norm_mlp_post, mlp_w1, mlp_w2, a_w_in, a_conv_w, a_q_norm,
           a_w_q_up, a_kv_norm, a_w_kv_up, a_w_out, c_w_in, c_cmp_pe_k, c_cmp_w1_k, c_cmp_w2_k, c_cmp_pe_v,
           c_cmp_w1_v, c_cmp_w2_v, c_w_out):
    depth = norm_mix_pre.shape[0]
    for layer in range(depth):
        i = layer // 2
        common = (norm_mix_post[layer], norm_mlp_pre[layer], mlp_w1[layer], mlp_w2[layer], norm_mlp_post[layer])
        if layer % 2 == 0:
            x = _layer_a(x, norm_mix_pre[layer], a_w_in[i], a_conv_w[i], a_q_norm[i], a_w_q_up[i], a_kv_norm[i],
                         a_w_kv_up[i], a_w_out[i], *common)
        else:
            x = _layer_c(x, norm_mix_pre[layer], c_w_in[i], c_cmp_pe_k[i], c_cmp_w1_k[i], c_cmp_w2_k[i],
                         c_cmp_pe_v[i], c_cmp_w1_v[i], c_cmp_w2_v[i], c_w_out[i], *common)
    return x
```

```python
import functools
import math

import numpy as np
import jax
import jax.numpy as jnp
from jax import lax
from jax.experimental import pallas as pl
from jax.experimental.pallas import tpu as pltpu

F32 = jnp.float32
BF16 = jnp.bfloat16

D_MODEL = 1024
D_FF = 4 * D_MODEL
EPS = 1e-6
NEG = -1e30
LOG2E = math.log2(math.e)

CONV_WIDTH = 512
CONV_K = 3
MLA_HEADS = 8
MLA_NOPE = 64
MLA_ROPE = 32
MLA_V = 64
MLA_KV_RANK = 256
MLA_Q_RANK = 768
MLA_SLOT = 128
ROPE_THETA = 10000.0
ROPE_HALF = MLA_ROPE // 2

NSA_HEADS = 16
NSA_GROUPS = 4
NSA_REP = 4
NSA_DH = 64
CMP_LEN = 32
CMP_STRIDE = 16
SLC_LEN = 64
SLC_SHIFT = 6
N_SEL = 16
WINDOW = 512
FORCE_BONUS = 1e4
KV_W = NSA_GROUPS * NSA_DH
C_MIX = NSA_HEADS * NSA_DH
GATE_ROWS = 16
ONES_ROWS = 16
V_ROWS = 64 + ONES_ROWS
KFEAT = 16

TOKEN_TILE = 512
ATTN_TILE = 256
MLA_HEADS_PER_STEP = 8
NSA_GROUPS_PER_STEP = 2
FF_CHUNK = 1024
VMEM_LIMIT = 56 * 1024 * 1024


def _cparams(sem):
    return pltpu.CompilerParams(dimension_semantics=sem, vmem_limit_bytes=VMEM_LIMIT)


def _const_spec(shape):
    nd = len(shape)
    return pl.BlockSpec(shape, lambda *_: (0,) * nd)


def _rms(x, g):
    ms = jnp.mean(x * x, axis=-1, keepdims=True)
    return x * lax.rsqrt(ms + EPS) * g


def _nt(a, b):
    return lax.dot_general(a, b, (((1,), (1,)), ((), ())), preferred_element_type=F32)


def _tn(a, b):
    return lax.dot_general(a, b, (((0,), (0,)), ((), ())), preferred_element_type=F32)


def _ones_rows(n):
    return jnp.where(lax.broadcasted_iota(jnp.int32, (ONES_ROWS, n), 0) == 0, 1.0, 0.0).astype(F32)


def _rope_rows(r1, r2, cos, sin):
    return r1 * cos - r2 * sin, r2 * cos + r1 * sin


def _a_pre_kernel(x_ref, g_ref, w1_ref, wkr_ref, convw_ref, qn_ref, wq_ref, kvn_ref, wk_ref, wv_ref,
                  cos_ref, sin_ref, yconv_ref, qT_ref, k_ref, vT_ref, carry_ref, *, tm, q_scale):
    @pl.when(pl.program_id(1) == 0)
    def _():
        carry_ref[...] = jnp.zeros_like(carry_ref)

    sub = min(ATTN_TILE, tm)
    w = convw_ref[...]
    w0, w1, w2 = w[0:1], w[1:2], w[2:3]
    hd = 16
    tail = carry_ref[...]
    z32 = jnp.zeros((MLA_SLOT - MLA_NOPE - MLA_ROPE, sub), F32)
    ones = _ones_rows(sub)
    o1, o2, o3 = CONV_WIDTH, 2 * CONV_WIDTH, 3 * CONV_WIDTH
    o4 = o3 + MLA_Q_RANK
    for cidx in range(tm // sub):
        rows = slice(cidx * sub, (cidx + 1) * sub)
        xn = _rms(x_ref[0, rows, :], g_ref[...]).astype(BF16)
        p1 = jnp.dot(xn, w1_ref[...], preferred_element_type=F32)
        g_b, g_c, hv = p1[:, :o1], p1[:, o1:o2], p1[:, o2:o3]
        c_q, c_kv = p1[:, o3:o4], p1[:, o4:]

        u = g_c * hv
        y = w2 * u + w1 * pltpu.roll(u, 1, 0) + w0 * pltpu.roll(u, 2, 0)
        uh = u[0:hd]
        row = lax.broadcasted_iota(jnp.int32, (hd, CONV_WIDTH), 0)
        h1 = jnp.where(row < 1, pltpu.roll(tail, 1, 0), pltpu.roll(uh, 1, 0))
        h2 = jnp.where(row < 2, pltpu.roll(tail, 2, 0), pltpu.roll(uh, 2, 0))
        yh = w2 * uh + w1 * h1 + w0 * h2
        yconv_ref[0, cidx * sub + hd:(cidx + 1) * sub, :] = (g_b[hd:] * y[hd:]).astype(BF16)
        yconv_ref[0, cidx * sub:cidx * sub + hd, :] = (g_b[0:hd] * yh).astype(BF16)
        tail = u[sub - hd:sub]

        cos = cos_ref[:, rows]
        sin = sin_ref[:, rows]

        hq = _rms(c_q, qn_ref[...]).astype(BF16)
        qT = _nt(wq_ref[...], hq)
        for h in range(MLA_HEADS):
            b0 = MLA_SLOT * h
            r1 = qT[b0 + MLA_NOPE:b0 + MLA_NOPE + ROPE_HALF]
            r2 = qT[b0 + MLA_NOPE + ROPE_HALF:b0 + MLA_NOPE + MLA_ROPE]
            e1, e2 = _rope_rows(r1, r2, cos, sin)
            slot = jnp.concatenate([qT[b0:b0 + MLA_NOPE], e1, e2, z32], axis=0) * q_scale
            qT_ref[0, b0:b0 + MLA_SLOT, rows] = slot.astype(BF16)

        krT = _nt(wkr_ref[...], xn)
        r1 = krT[MLA_NOPE:MLA_NOPE + ROPE_HALF]
        r2 = krT[MLA_NOPE + ROPE_HALF:MLA_NOPE + MLA_ROPE]
        e1, e2 = _rope_rows(r1, r2, cos, sin)
        slotT = jnp.concatenate([jnp.zeros((MLA_NOPE, sub), F32), e1, e2, z32], axis=0)
        slot = slotT.T

        hkv = _rms(c_kv, kvn_ref[...]).astype(BF16)
        kn = jnp.dot(hkv, wk_ref[...], preferred_element_type=F32)
        k = kn + jnp.concatenate([slot] * MLA_HEADS, axis=1)
        k_ref[0, rows, :] = k.astype(BF16)
        vT = _nt(wv_ref[...], hkv)
        vT = jnp.concatenate([x for h in range(MLA_HEADS) for x in (vT[h * MLA_V:(h + 1) * MLA_V], ones)], axis=0)
        vT_ref[0, cidx] = vT.astype(BF16)
    carry_ref[...] = tail


def _a_pre(x, g, w1, wkr, convw, qn, wq, kvn, wk, wv, cos, sin):
    b, s, d = x.shape
    tm = min(TOKEN_TILE, s)
    hw = MLA_HEADS * MLA_SLOT
    vw = MLA_HEADS * V_ROWS
    q_scale = (MLA_NOPE + MLA_ROPE) ** -0.5 * LOG2E
    kern = functools.partial(_a_pre_kernel, tm=tm, q_scale=q_scale)
    return pl.pallas_call(
        kern,
        grid=(b, s // tm),
        in_specs=[
            pl.BlockSpec((1, tm, d), lambda i, j: (i, j, 0)),
            _const_spec(g.shape), _const_spec(w1.shape), _const_spec(wkr.shape), _const_spec(convw.shape),
            _const_spec(qn.shape), _const_spec(wq.shape), _const_spec(kvn.shape), _const_spec(wk.shape),
            _const_spec(wv.shape),
            pl.BlockSpec((ROPE_HALF, tm), lambda i, j: (0, j)),
            pl.BlockSpec((ROPE_HALF, tm), lambda i, j: (0, j)),
        ],
        out_specs=[
            pl.BlockSpec((1, tm, CONV_WIDTH), lambda i, j: (i, j, 0)),
            pl.BlockSpec((1, hw, tm), lambda i, j: (i, 0, j)),
            pl.BlockSpec((1, tm, hw), lambda i, j: (i, j, 0)),
            pl.BlockSpec((1, tm // ATTN_TILE, vw, ATTN_TILE), lambda i, j: (i, j, 0, 0)),
        ],
        out_shape=[
            jax.ShapeDtypeStruct((b, s, CONV_WIDTH), BF16),
            jax.ShapeDtypeStruct((b, hw, s), BF16),
            jax.ShapeDtypeStruct((b, s, hw), BF16),
            jax.ShapeDtypeStruct((b, s // ATTN_TILE, vw, ATTN_TILE), BF16),
        ],
        scratch_shapes=[pltpu.VMEM((16, CONV_WIDTH), F32)],
        compiler_params=_cparams(("arbitrary", "arbitrary")),
        name="a_pre",
    )(x, g, w1, wkr, convw, qn, wq, kvn, wk, wv, cos, sin)


def _tile_pos(t, k0, q0):
    kpos = k0 + lax.broadcasted_iota(jnp.int32, (t, t), 0)
    qpos = q0 + lax.broadcasted_iota(jnp.int32, (t, t), 1)
    return kpos, qpos


def _stage_scores(s, s_slot):
    s_slot[...] = s
    return jnp.max(s, axis=0, keepdims=True)


def _stage_softmax(s_slot, p_slot, m, mt):
    m_new = jnp.maximum(m, mt)
    p_slot[...] = jnp.exp2(s_slot[...] - m_new).astype(BF16)
    return m_new, jnp.exp2(m - m_new)


def _stage_pv(acc_slot, vT, p_slot, alpha):
    acc_slot[...] = alpha * acc_slot[...] + jnp.dot(vT, p_slot[...], preferred_element_type=F32)


def _causal_flash(i, n_rest, tile_at, t, chains, scores, vtile, diag_extra, s_ref, p_ref, acc_ref):
    mt0 = []
    for c in chains:
        mt0.append(_stage_scores(scores(c, i) + diag_extra, s_ref.at[c]))
        acc_ref[c] = jnp.zeros((V_ROWS, t), F32)
    init = (tuple(mt0), tuple(jnp.full((1, t), NEG, F32) for _ in chains))
    last = jnp.maximum(n_rest - 1, 0)

    def body(n, carry):
        mt, m = carry
        j_cur = jnp.where(n == 0, i, tile_at(jnp.maximum(n - 1, 0)))
        j_next = tile_at(jnp.minimum(n, last))
        m_new, mt_next = [], []
        for c in chains:
            mn, alpha = _stage_softmax(s_ref.at[c], p_ref.at[c], m[c], mt[c])
            m_new.append(mn)
            mt_next.append(_stage_scores(scores(c, j_next), s_ref.at[c]))
            _stage_pv(acc_ref.at[c], vtile(c, j_cur), p_ref.at[c], alpha)
        return tuple(mt_next), tuple(m_new)

    n_iter = n_rest + 1
    quads = lax.shift_right_logical(n_iter, 2)

    def quad(k, cr):
        for u in range(4):
            cr = body(4 * k + u, cr)
        return cr

    carry = lax.fori_loop(0, quads, quad, init)
    lax.fori_loop(4 * quads, n_iter, body, carry)


def _normalized(acc_slot):
    acc = acc_slot[...]
    return acc[:V_ROWS - ONES_ROWS] / acc[V_ROWS - ONES_ROWS:V_ROWS - ONES_ROWS + 1]


def _mla_attn_kernel(qT_ref, k_ref, vT_ref, oT_ref, s_ref, p_ref, acc_ref, *, t, hb):
    i = pl.program_id(2)

    def scores(h, j):
        k = k_ref[0, pl.ds(pl.multiple_of(j * t, t), t), h * MLA_SLOT:(h + 1) * MLA_SLOT]
        return jnp.dot(k, qT_ref[0, h * MLA_SLOT:(h + 1) * MLA_SLOT, :], preferred_element_type=F32)

    def vtile(h, j):
        return vT_ref[0, j, h * V_ROWS:(h + 1) * V_ROWS, :]

    kpos, qpos = _tile_pos(t, 0, 0)
    causal = jnp.where(kpos <= qpos, 0.0, NEG)
    _causal_flash(i, i, lambda n: n, t, range(hb), scores, vtile, causal, s_ref, p_ref, acc_ref)
    for h in range(hb):
        oT_ref[0, h * MLA_V:(h + 1) * MLA_V, :] = _normalized(acc_ref.at[h]).astype(BF16)


def _mla_attn(qT, k, vT):
    b, hw, s = qT.shape
    t = min(ATTN_TILE, s)
    nkv = s // t
    hb = MLA_HEADS_PER_STEP
    kern = functools.partial(_mla_attn_kernel, t=t, hb=hb)
    return pl.pallas_call(
        kern,
        grid=(b, MLA_HEADS // hb, s // t),
        in_specs=[
            pl.BlockSpec((1, hb * MLA_SLOT, t), lambda bi, h, i: (bi, h, i)),
            pl.BlockSpec((1, s, hb * MLA_SLOT), lambda bi, h, i: (bi, 0, h)),
            pl.BlockSpec((1, nkv, hb * V_ROWS, t), lambda bi, h, i: (bi, 0, h, 0)),
        ],
        out_specs=pl.BlockSpec((1, hb * MLA_V, t), lambda bi, h, i: (bi, h, i)),
        out_shape=jax.ShapeDtypeStruct((b, MLA_HEADS * MLA_V, s), BF16),
        scratch_shapes=[pltpu.VMEM((hb, t, t), F32), pltpu.VMEM((hb, t, t), BF16),
                        pltpu.VMEM((hb, V_ROWS, t), F32)],
        compiler_params=_cparams(("arbitrary", "arbitrary", "arbitrary")),
        name="mla_attn",
    )(qT, k, vT)


def _post_kernel(*refs, has_tok):
    if has_tok:
        x_ref, ytok_ref, yT_ref, wo_tok_ref, wo_ref, gpost_ref, gpre_ref, w1_ref, w2_ref, gmlp_ref, o_ref = refs
    else:
        x_ref, yT_ref, wo_ref, gpost_ref, gpre_ref, w1_ref, w2_ref, gmlp_ref, o_ref = refs
    x = x_ref[0]
    mix = _tn(yT_ref[0], wo_ref[...])
    if has_tok:
        mix = mix + jnp.dot(ytok_ref[0], wo_tok_ref[...], preferred_element_type=F32)
    x1 = x + _rms(mix, gpost_ref[...])
    h = _rms(x1, gpre_ref[...]).astype(BF16)
    acc = jnp.zeros_like(x1)
    for c in range(D_FF // FF_CHUNK):
        a = jnp.dot(h, w1_ref[:, c * FF_CHUNK:(c + 1) * FF_CHUNK], preferred_element_type=F32)
        a = jnp.maximum(a, 0.0)
        a = (a * a).astype(BF16)
        acc = acc + jnp.dot(a, w2_ref[c * FF_CHUNK:(c + 1) * FF_CHUNK, :], preferred_element_type=F32)
    o_ref[0] = x1 + _rms(acc, gmlp_ref[...])


def _post(x, ytok, yT, wo_tok, wo, gpost, gpre, w1, w2, gmlp):
    b, s, d = x.shape
    tm = min(TOKEN_TILE, s)
    has_tok = ytok is not None
    kern = functools.partial(_post_kernel, has_tok=has_tok)
    xspec = pl.BlockSpec((1, tm, d), lambda i, j: (i, j, 0))
    ins, specs = [x], [xspec]
    if has_tok:
        ins.append(ytok)
        specs.append(pl.BlockSpec((1, tm, ytok.shape[-1]), lambda i, j: (i, j, 0)))
    ins.append(yT)
    specs.append(pl.BlockSpec((1, yT.shape[1], tm), lambda i, j: (i, 0, j)))
    if has_tok:
        ins.append(wo_tok)
        specs.append(_const_spec(wo_tok.shape))
    for a in (wo, gpost, gpre, w1, w2, gmlp):
        ins.append(a)
        specs.append(_const_spec(a.shape))
    return pl.pallas_call(
        kern,
        grid=(b, s // tm),
        in_specs=specs,
        out_specs=xspec,
        out_shape=jax.ShapeDtypeStruct(x.shape, x.dtype),
        compiler_params=_cparams(("arbitrary", "arbitrary")),
        name="post_tok" if has_tok else "post",
    )(*ins)


def _c_pre_kernel(x_ref, g_ref, wq_ref, wtok_ref, wf_ref, kfs_ref, kfw_ref, qT_ref, kc_ref, vc_ref, ks_ref, kw_ref,
                  vsT_ref, vwT_ref, gT_ref, *, tm, q_scale):
    xn = _rms(x_ref[0], g_ref[...]).astype(BF16)
    qT_ref[0] = (_nt(wq_ref[...], xn) * q_scale).astype(BF16)
    tok = jnp.dot(xn, wtok_ref[...], preferred_element_type=F32)
    kc_ref[0] = tok[:, :KV_W].astype(BF16)
    vc_ref[0] = tok[:, KV_W:2 * KV_W].astype(BF16)
    kfs = kfs_ref[...]
    kfw = kfw_ref[...]
    for g in range(NSA_GROUPS):
        a = 2 * KV_W + g * NSA_DH
        ks_ref[0, g] = jnp.concatenate([tok[:, a:a + NSA_DH], kfs], axis=1).astype(BF16)
        kw_ref[0, g] = jnp.concatenate([tok[:, a + KV_W:a + KV_W + NSA_DH], kfw], axis=1).astype(BF16)
    fT = _nt(wf_ref[...], xn)
    ones = _ones_rows(tm)

    def with_ones(v):
        return jnp.concatenate([x for g in range(NSA_GROUPS) for x in (v[g * NSA_DH:(g + 1) * NSA_DH], ones)],
                               axis=0)

    vs = with_ones(fT[:KV_W])
    vw = with_ones(fT[KV_W:2 * KV_W])
    for cidx in range(tm // ATTN_TILE):
        sl = slice(cidx * ATTN_TILE, (cidx + 1) * ATTN_TILE)
        vsT_ref[0, cidx] = vs[:, sl].astype(BF16)
        vwT_ref[0, cidx] = vw[:, sl].astype(BF16)
    gT_ref[0] = 1.0 / (1.0 + jnp.exp(-fT[2 * KV_W:]))


def _c_pre(x, g, wq, wtok, wf, kfs, kfw):
    b, s, d = x.shape
    tm = min(TOKEN_TILE, s)
    q_scale = NSA_DH ** -0.5 * LOG2E
    kern = functools.partial(_c_pre_kernel, tm=tm, q_scale=q_scale)
    nt = tm // ATTN_TILE
    grows = NSA_GROUPS * GATE_ROWS
    kaug = NSA_DH + kfs.shape[1]
    vrows = NSA_GROUPS * V_ROWS
    kf_spec = pl.BlockSpec((tm, kfs.shape[1]), lambda i, j: (j, 0))
    return pl.pallas_call(
        kern,
        grid=(b, s // tm),
        in_specs=[pl.BlockSpec((1, tm, d), lambda i, j: (i, j, 0)),
                  _const_spec(g.shape), _const_spec(wq.shape), _const_spec(wtok.shape), _const_spec(wf.shape),
                  kf_spec, kf_spec],
        out_specs=[
            pl.BlockSpec((1, C_MIX, tm), lambda i, j: (i, 0, j)),
            pl.BlockSpec((1, tm, KV_W), lambda i, j: (i, j, 0)),
            pl.BlockSpec((1, tm, KV_W), lambda i, j: (i, j, 0)),
            pl.BlockSpec((1, NSA_GROUPS, tm, kaug), lambda i, j: (i, 0, j, 0)),
            pl.BlockSpec((1, NSA_GROUPS, tm, kaug), lambda i, j: (i, 0, j, 0)),
            pl.BlockSpec((1, nt, vrows, ATTN_TILE), lambda i, j: (i, j, 0, 0)),
            pl.BlockSpec((1, nt, vrows, ATTN_TILE), lambda i, j: (i, j, 0, 0)),
            pl.BlockSpec((1, grows, tm), lambda i, j: (i, 0, j)),
        ],
        out_shape=[
            jax.ShapeDtypeStruct((b, C_MIX, s), BF16),
            jax.ShapeDtypeStruct((b, s, KV_W), BF16),
            jax.ShapeDtypeStruct((b, s, KV_W), BF16),
            jax.ShapeDtypeStruct((b, NSA_GROUPS, s, kaug), BF16),
            jax.ShapeDtypeStruct((b, NSA_GROUPS, s, kaug), BF16),
            jax.ShapeDtypeStruct((b, s // ATTN_TILE, vrows, ATTN_TILE), BF16),
            jax.ShapeDtypeStruct((b, s // ATTN_TILE, vrows, ATTN_TILE), BF16),
            jax.ShapeDtypeStruct((b, grows, s), F32),
        ],
        compiler_params=_cparams(("arbitrary", "arbitrary")),
        name="c_pre",
    )(x, g, wq, wtok, wf, kfs, kfw)


def _gelu_tanh(x):
    return 0.5 * x * (1.0 + jnp.tanh(math.sqrt(2.0 / math.pi) * (x + 0.044715 * (x * x * x))))


def _compress_kernel(zk_ref, zv_ref, pek_ref, pev_ref, wak_ref, wbk_ref, wav_ref, wbv_ref, w2k_ref, w2vT_ref, kf_ref,
                     kc_ref, vcT_ref, *, nch):
    def hidden(z_ref, pe_ref, wa_ref, wb_ref):
        z = z_ref[0].astype(F32)
        za = (z + pe_ref[0:1]).astype(BF16)
        zb = (z + pe_ref[1:2]).astype(BF16)
        a = jnp.dot(za, wa_ref[...], preferred_element_type=F32)
        bm = jnp.dot(zb, wb_ref[...], preferred_element_type=F32)
        return _gelu_tanh(a + pltpu.roll(bm, nch - 1, 0)).astype(BF16)

    hk = hidden(zk_ref, pek_ref, wak_ref, wbk_ref)
    kc = jnp.dot(hk, w2k_ref[...], preferred_element_type=F32)
    kf = kf_ref[...]
    hv = hidden(zv_ref, pev_ref, wav_ref, wbv_ref)
    vcT = _nt(w2vT_ref[...], hv)
    ones = _ones_rows(nch)
    for g in range(NSA_GROUPS):
        gs = slice(g * NSA_DH, (g + 1) * NSA_DH)
        kc_ref[0, g] = jnp.concatenate([kc[:, gs], kf], axis=1).astype(BF16)
        vcT_ref[0, g] = jnp.concatenate([vcT[gs], ones], axis=0).astype(BF16)


def _compress(zk, zv, pek, pev, wak, wbk, wav, wbv, w2k, w2vT, kf):
    b, nch, cw = zk.shape
    kern = functools.partial(_compress_kernel, nch=nch)
    zspec = pl.BlockSpec((1, nch, cw), lambda i: (i, 0, 0))
    consts = (pek, pev, wak, wbk, wav, wbv, w2k, w2vT, kf)
    return pl.pallas_call(
        kern,
        grid=(b,),
        in_specs=[zspec, zspec] + [_const_spec(a.shape) for a in consts],
        out_specs=[pl.BlockSpec((1, NSA_GROUPS, nch, NSA_DH + KFEAT), lambda i: (i, 0, 0, 0)),
                   pl.BlockSpec((1, NSA_GROUPS, V_ROWS, nch), lambda i: (i, 0, 0, 0))],
        out_shape=[jax.ShapeDtypeStruct((b, NSA_GROUPS, nch, NSA_DH + KFEAT), BF16),
                   jax.ShapeDtypeStruct((b, NSA_GROUPS, V_ROWS, nch), BF16)],
        compiler_params=_cparams(("arbitrary",)),
        name="compress",
    )(zk, zv, *consts)


def _split3(x):
    hi = x.astype(BF16)
    r = x - hi.astype(F32)
    mid = r.astype(BF16)
    lo = (r - mid.astype(F32)).astype(BF16)
    return hi, mid, lo


def _feature_rows(tab_ref, head, const, t):
    row = lax.broadcasted_iota(jnp.int32, (KFEAT, t), 0)
    r3 = row - 3 * (row >= 3).astype(jnp.int32) - 3 * (row >= 6).astype(jnp.int32)
    s_hi, s_mid, s_lo, slope = (tab_ref[head * 4 + k] for k in range(4))
    c_hi, c_mid, c_lo = (x.astype(F32) for x in _split3(slope * const))
    svals = jnp.where(r3 == 0, s_hi, jnp.where(r3 == 1, s_mid, s_lo))
    cvals = jnp.where(r3 == 0, c_hi, jnp.where(r3 == 1, c_mid, c_lo))
    return jnp.where(row < 6, svals, jnp.where(row < 9, cvals, 0.0)).astype(BF16)


def _cmp_topk_kernel(tab_ref, qT_ref, kc_ref, vcT_ref, ovT_ref, pool_ref, gT_ref, ocT_ref, bias_ref, act_ref,
                     qa_ref, e_ref, p_ref, acc_ref, imp_ref, rank_ref, *, t, gb, nch, nslc, nsel):
    gp = pl.program_id(1)
    i = pl.program_id(2)
    q0 = i * t
    chains = range(gb * NSA_REP)
    bpt = t // SLC_LEN

    const = (CMP_LEN - 1) / 2.0 - jnp.full((KFEAT, t), q0, jnp.int32).astype(F32)
    for c in chains:
        head = (gp * gb + c // NSA_REP) * NSA_REP + c % NSA_REP
        qa_ref[c, 0:NSA_DH, :] = qT_ref[0, c * NSA_DH:(c + 1) * NSA_DH, :]
        qa_ref[c, NSA_DH:NSA_DH + KFEAT, :] = _feature_rows(tab_ref, head, const, t)

    n_i = lax.broadcasted_iota(jnp.int32, (nch, t), 0)
    t_i = q0 + lax.broadcasted_iota(jnp.int32, (nch, t), 1)
    ended = jnp.where(n_i * CMP_STRIDE + (CMP_LEN - 1) <= t_i, 0.0, NEG)
    mt = [_stage_scores(jnp.dot(kc_ref[0, c // NSA_REP], qa_ref[c], preferred_element_type=F32) + ended,
                        e_ref.at[c]) for c in chains]
    for c in chains:
        m_eff = jnp.where(mt[c] > 0.5 * NEG, mt[c], 0.0)
        e = jnp.exp2(e_ref[c] - m_eff)
        e_ref[c] = e
        p_ref[c] = e.astype(BF16)
    for c in chains:
        acc_ref[c] = jnp.dot(vcT_ref[0, c // NSA_REP], p_ref[c], preferred_element_type=F32)

    j_i = lax.broadcasted_iota(jnp.int32, (nslc, t), 0)
    cur = jnp.right_shift(q0 + lax.broadcasted_iota(jnp.int32, (nslc, t), 1), SLC_SHIFT)
    forced = (j_i == 0) | (j_i == cur) | (j_i == cur - 1)
    ovT = ovT_ref[...]
    for gl in range(gb):
        psum = jnp.zeros((nch, t), F32)
        for r in range(NSA_REP):
            c = gl * NSA_REP + r
            acc = acc_ref[c]
            l = acc[NSA_DH:NSA_DH + 1]
            inv = 1.0 / jnp.where(l > 0.0, l, 1.0)
            gate = gT_ref[0, gl * GATE_ROWS + r:gl * GATE_ROWS + r + 1, :]
            ocT_ref[0, c * NSA_DH:(c + 1) * NSA_DH, :] = (gate * (acc[:NSA_DH] * inv)).astype(BF16)
            psum = psum + e_ref[c] * inv
        imp = sum(jnp.dot(ovT, part, preferred_element_type=F32) for part in _split3(psum))
        imp_ref[gl] = jnp.where(j_i > cur, NEG, imp + jnp.where(forced, FORCE_BONUS, 0.0))
        rank_ref[gl] = jnp.zeros((nslc, t), F32)

    for chunk in range(nslc // bpt):
        @pl.when(chunk <= i)
        def _():
            for gl in range(gb):
                imp = imp_ref[gl]
                rank = rank_ref[gl]
                for jp in range(chunk * bpt, (chunk + 1) * bpt):
                    row = imp[jp:jp + 1, :]
                    incs = []
                    for rg in range(nslc // 8):
                        blk = imp[8 * rg:8 * rg + 8]
                        if 8 * rg > jp:
                            incs.append(jnp.where(row >= blk, 1.0, 0.0))
                        elif 8 * rg + 7 <= jp:
                            incs.append(jnp.where(row > blk, 1.0, 0.0))
                        else:
                            above = lax.broadcasted_iota(jnp.int32, (8, t), 0) > jp - 8 * rg
                            incs.append(jnp.where(above, jnp.where(row >= blk, 1.0, 0.0),
                                                  jnp.where(row > blk, 1.0, 0.0)))
                    rank = rank + jnp.concatenate(incs, axis=0)
                rank_ref[gl] = rank

    best = jnp.full((nslc, t), NEG, F32)
    for gl in range(gb):
        bias = jnp.where(j_i > cur, NEG, jnp.where(rank_ref[gl] < nsel, 0.0, NEG))
        bias_ref[0, gl] = bias
        best = jnp.maximum(best, bias)
    hit = jnp.where(jnp.max(best, axis=1, keepdims=True) == 0.0, 1.0, 0.0)
    tiles = jnp.sum(hit * pool_ref[...], axis=0, keepdims=True)
    act_ref[0, 0, 0] = jnp.broadcast_to(jnp.where(tiles > 0.0, 1, 0), (8, 128)).astype(jnp.int32)


def _cmp_topk(tab, qT, kc, vcT, ovT, gT, nsel):
    b, _, s = qT.shape
    t = ATTN_TILE
    assert s % t == 0 and t % SLC_LEN == 0
    nch = kc.shape[2]
    nslc = ovT.shape[0]
    gb = NSA_GROUPS_PER_STEP
    nc = gb * NSA_REP
    kern = functools.partial(_cmp_topk_kernel, t=t, gb=gb, nch=nch, nslc=nslc, nsel=nsel)
    gw = nc * NSA_DH
    pool = np.zeros((nslc, 128), np.float32)
    pool[np.arange(nslc), np.arange(nslc) // (t // SLC_LEN)] = 1.0
    pool = jnp.asarray(pool, F32)
    return pl.pallas_call(
        kern,
        grid=(b, NSA_GROUPS // gb, s // t),
        in_specs=[
            pl.BlockSpec(memory_space=pltpu.SMEM),
            pl.BlockSpec((1, gw, t), lambda bi, g, i: (bi, g, i)),
            pl.BlockSpec((1, gb, nch, NSA_DH + KFEAT), lambda bi, g, i: (bi, g, 0, 0)),
            pl.BlockSpec((1, gb, V_ROWS, nch), lambda bi, g, i: (bi, g, 0, 0)),
            _const_spec(ovT.shape),
            _const_spec(pool.shape),
            pl.BlockSpec((1, gb * GATE_ROWS, t), lambda bi, g, i: (bi, g, i)),
        ],
        out_specs=[pl.BlockSpec((1, gw, t), lambda bi, g, i: (bi, g, i)),
                   pl.BlockSpec((1, gb, nslc, t), lambda bi, g, i: (bi, g, 0, i)),
                   pl.BlockSpec((1, 1, 1, 8, 128), lambda bi, g, i: (bi, g, i, 0, 0))],
        out_shape=[jax.ShapeDtypeStruct((b, C_MIX, s), BF16),
                   jax.ShapeDtypeStruct((b, NSA_GROUPS, nslc, s), F32),
                   jax.ShapeDtypeStruct((b, NSA_GROUPS // gb, s // t, 8, 128), jnp.int32)],
        scratch_shapes=[pltpu.VMEM((nc, NSA_DH + KFEAT, t), BF16),
                        pltpu.VMEM((nc, nch, t), F32), pltpu.VMEM((nc, nch, t), BF16),
                        pltpu.VMEM((nc, V_ROWS, t), F32),
                        pltpu.VMEM((gb, nslc, t), F32), pltpu.VMEM((gb, nslc, t), F32)],
        compiler_params=_cparams(("arbitrary", "arbitrary", "arbitrary")),
        name="cmp_topk",
    )(tab, qT, kc, vcT, ovT, pool, gT)


def _slc_win_kernel(act_ref, tab_ref, qT_ref, ks_ref, vsT_ref, kw_ref, vwT_ref, bias_ref, ocT_ref, gT_ref, yT_ref,
                    qa_ref, s_ref, p_ref, acc_ref, sw_ref, pw_ref, accw_ref, idx_ref, *, t, gb, nslc, nkv):
    gp = pl.program_id(1)
    i = pl.program_id(2)
    chains = range(gb * NSA_REP)
    feat0 = NSA_DH + nslc

    const = -jnp.full((KFEAT, t), i * t, jnp.int32).astype(F32)
    for c in chains:
        gl, r = divmod(c, NSA_REP)
        head = (gp * gb + gl) * NSA_REP + r
        qa_ref[c, 0:NSA_DH, :] = qT_ref[0, c * NSA_DH:(c + 1) * NSA_DH, :]
        qa_ref[c, NSA_DH:feat0, :] = bias_ref[0, gl].astype(BF16)
        qa_ref[c, feat0:feat0 + KFEAT, :] = _feature_rows(tab_ref, head, const, t)

    def scores_from(k_ref):
        def scores(c, j):
            k = k_ref[0, c // NSA_REP, pl.ds(pl.multiple_of(j * t, t), t), :]
            return jnp.dot(k, qa_ref[c], preferred_element_type=F32)
        return scores

    def vtile_from(vT_ref):
        def vtile(c, j):
            gl = c // NSA_REP
            return vT_ref[0, j, gl * V_ROWS:(gl + 1) * V_ROWS, :]
        return vtile

    kpos, qpos = _tile_pos(t, 0, 0)
    causal = jnp.where(kpos <= qpos, 0.0, NEG)

    w_scores = scores_from(kw_ref)
    w_vtile = vtile_from(vwT_ref)
    w_tiles = (i, jnp.maximum(i - 1, 0), jnp.maximum(i - 2, 0))
    in_seq1 = jnp.where((i - 1) * t + kpos >= 0, 0.0, NEG)
    in_seq2 = jnp.where((i - 2) * t + kpos >= 0, 0.0, NEG)
    w_extra = (causal, in_seq1, jnp.where(qpos < kpos, in_seq2, NEG))
    m = [jnp.full((1, t), NEG, F32) for _ in chains]
    mt = []
    for c in chains:
        accw_ref[c] = jnp.zeros((V_ROWS, t), F32)
        mt.append(_stage_scores(w_scores(c, w_tiles[0]) + w_extra[0], sw_ref.at[c]))
    for w in range(3):
        for c in chains:
            m[c], alpha = _stage_softmax(sw_ref.at[c], pw_ref.at[c], m[c], mt[c])
            if w + 1 < 3:
                mt[c] = _stage_scores(w_scores(c, w_tiles[w + 1]) + w_extra[w + 1], sw_ref.at[c])
            _stage_pv(accw_ref.at[c], w_vtile(c, w_tiles[w]), pw_ref.at[c], alpha)

    base = ((pl.program_id(0) * pl.num_programs(1) + gp) * pl.num_programs(2) + i) * nkv
    n_act = jnp.int32(0)
    for j in range(nkv):
        idx_ref[n_act] = j
        n_act = n_act + jnp.where((j < i) & (act_ref[base + j] != 0), 1, 0)
    _causal_flash(i, n_act, lambda n: idx_ref[n], t, chains, scores_from(ks_ref), vtile_from(vsT_ref), causal,
                  s_ref, p_ref, acc_ref)

    for c in chains:
        gl, r = divmod(c, NSA_REP)
        g0 = gl * GATE_ROWS
        rs = slice(c * NSA_DH, (c + 1) * NSA_DH)
        y = (ocT_ref[0, rs, :].astype(F32)
             + gT_ref[0, g0 + NSA_REP + r:g0 + NSA_REP + r + 1, :] * _normalized(acc_ref.at[c])
             + gT_ref[0, g0 + 2 * NSA_REP + r:g0 + 2 * NSA_REP + r + 1, :] * _normalized(accw_ref.at[c]))
        yT_ref[0, rs, :] = y.astype(BF16)


def _slc_win(act, tab, qT, ks, vsT, kw, vwT, bias, ocT, gT):
    b, _, s = qT.shape
    t = ATTN_TILE
    assert s % t == 0 and WINDOW == 2 * t
    nkv = s // t
    nslc = bias.shape[2]
    kaug = ks.shape[3]
    gb = NSA_GROUPS_PER_STEP
    nc = gb * NSA_REP
    gw = nc * NSA_DH
    kern = functools.partial(_slc_win_kernel, t=t, gb=gb, nslc=nslc, nkv=nkv)
    qspec = pl.BlockSpec((1, gw, t), lambda bi, g, i, *_: (bi, g, i))
    kspec = pl.BlockSpec((1, gb, s, kaug), lambda bi, g, i, *_: (bi, g, 0, 0))
    vspec = pl.BlockSpec((1, nkv, gb * V_ROWS, t), lambda bi, g, i, *_: (bi, 0, g, 0))
    grid_spec = pltpu.PrefetchScalarGridSpec(
        num_scalar_prefetch=2,
        grid=(b, NSA_GROUPS // gb, s // t),
        in_specs=[
            qspec, kspec, vspec, kspec, vspec,
            pl.BlockSpec((1, gb, nslc, t), lambda bi, g, i, *_: (bi, g, 0, i)),
            qspec,
            pl.BlockSpec((1, gb * GATE_ROWS, t), lambda bi, g, i, *_: (bi, g, i)),
        ],
        out_specs=qspec,
        scratch_shapes=[pltpu.VMEM((nc, kaug, t), BF16),
                        pltpu.VMEM((nc, t, t), F32), pltpu.VMEM((nc, t, t), BF16), pltpu.VMEM((nc, V_ROWS, t), F32),
                        pltpu.VMEM((nc, t, t), F32), pltpu.VMEM((nc, t, t), BF16),
                        pltpu.VMEM((nc, V_ROWS, t), F32),
                        pltpu.SMEM((nkv + 1,), jnp.int32)],
    )
    return pl.pallas_call(
        kern,
        grid_spec=grid_spec,
        out_shape=jax.ShapeDtypeStruct((b, C_MIX, s), BF16),
        compiler_params=_cparams(("arbitrary", "arbitrary", "arbitrary")),
        name="slc_win",
    )(act, tab, qT, ks, vsT, kw, vwT, bias, ocT, gT)


def _row(v):
    return v.reshape(1, -1).astype(F32)


def _layer_a(x, g_pre, w_in, conv_w, q_norm, w_q_up, kv_norm, w_kv_up, w_out, g_post, g_mlp_pre, w1, w2, g_mlp_post):
    b, s, d = x.shape
    o3 = 3 * CONV_WIDTH
    o5 = o3 + MLA_Q_RANK + MLA_KV_RANK
    pad = MLA_SLOT - MLA_NOPE - MLA_ROPE
    w_main = w_in[:, :o5].astype(BF16)
    wkr = jnp.pad(w_in[:, o5:].T, ((MLA_NOPE, pad), (0, 0))).astype(BF16)
    wq = w_q_up.reshape(MLA_Q_RANK, MLA_HEADS, MLA_NOPE + MLA_ROPE)
    wq = jnp.pad(wq, ((0, 0), (0, 0), (0, pad))).reshape(MLA_Q_RANK, MLA_HEADS * MLA_SLOT).T.astype(BF16)
    wkv = w_kv_up.reshape(MLA_KV_RANK, MLA_HEADS, MLA_NOPE + MLA_V)
    wk = jnp.pad(wkv[:, :, :MLA_NOPE], ((0, 0), (0, 0), (0, MLA_SLOT - MLA_NOPE)))
    wk = wk.reshape(MLA_KV_RANK, MLA_HEADS * MLA_SLOT).astype(BF16)
    wv = wkv[:, :, MLA_NOPE:].reshape(MLA_KV_RANK, MLA_HEADS * MLA_V).T.astype(BF16)

    inv = ROPE_THETA ** (-jnp.arange(ROPE_HALF, dtype=F32) / ROPE_HALF)
    ang = inv[:, None] * jnp.arange(s, dtype=F32)[None, :]
    cos, sin = jnp.cos(ang), jnp.sin(ang)

    yconv, qT, k, vT = _a_pre(x, _row(g_pre), w_main, wkr, conv_w.astype(F32), _row(q_norm), wq, _row(kv_norm),
                              wk, wv, cos, sin)
    yT = _mla_attn(qT, k, vT)
    wo = w_out.astype(BF16)
    return _post(x, yconv, yT, wo[:CONV_WIDTH], wo[CONV_WIDTH:], _row(g_post), _row(g_mlp_pre),
                 w1.astype(BF16), w2.astype(BF16), _row(g_mlp_post))


def _layer_c(x, g_pre, w_in, pe_k, w1_k, w2_k, pe_v, w1_v, w2_v, w_out, g_post, g_mlp_pre, w1, w2, g_mlp_post):
    b, s, d = x.shape
    G, R, Dh = NSA_GROUPS, NSA_REP, NSA_DH
    wq = w_in[:, :C_MIX].T.astype(BF16)
    kv = w_in[:, C_MIX:C_MIX + 6 * KV_W].reshape(d, 6, KV_W)
    wtok = jnp.concatenate([kv[:, 0], kv[:, 1], kv[:, 2], kv[:, 4]], axis=1).astype(BF16)
    wg = w_in[:, C_MIX + 6 * KV_W:].reshape(d, 3, G, R).transpose(0, 2, 1, 3)
    wg = jnp.pad(wg.reshape(d, G, 3 * R), ((0, 0), (0, 0), (0, GATE_ROWS - 3 * R))).reshape(d, G * GATE_ROWS)
    wf = jnp.concatenate([kv[:, 3], kv[:, 5], wg], axis=1).T.astype(BF16)

    n_slc = s // SLC_LEN
    pos = np.arange(s)
    blk, within = pos // SLC_LEN, pos % SLC_LEN
    feats = np.zeros((s, KFEAT), np.float32)
    feats[:, 0:3] = (blk * SLC_LEN)[:, None]
    feats[:, 3:6] = within[:, None]
    feats[:, 6:9] = 1.0
    onehot = (blk[:, None] == np.arange(n_slc)[None, :]).astype(np.float32)
    kfs = jnp.asarray(np.concatenate([onehot, feats], axis=1), F32)
    kfw = jnp.asarray(np.concatenate([np.zeros_like(onehot), feats], axis=1), F32)

    qT, kc_tok, vc_tok, ks, kw, vsT, vwT, gT = _c_pre(x, _row(g_pre), wq, wtok, wf, kfs, kfw)

    nch = s // CMP_STRIDE
    half = CMP_LEN // 2
    eye = jnp.eye(G, dtype=F32)

    def w1_halves(w):
        bd = jnp.einsum('hlde,gk->hlgdke', w.reshape(2, half, Dh, Dh), eye)
        bd = bd.reshape(2, half * KV_W, KV_W).astype(BF16)
        return bd[0], bd[1]

    def pe_halves(pe):
        return jnp.tile(pe.reshape(2, half, 1, Dh), (1, 1, G, 1)).reshape(2, half * KV_W).astype(F32)

    def w2_bd(w):
        return jnp.einsum('de,gk->gdke', w, eye).reshape(KV_W, KV_W)

    n_idx = np.arange(nch)
    cfeat = np.zeros((nch, KFEAT), np.float32)
    cfeat[:, 0:3] = (1024 * (n_idx // 64))[:, None]
    cfeat[:, 3:6] = (CMP_STRIDE * (n_idx % 64))[:, None]
    cfeat[:, 6:9] = 1.0
    kc, vcT = _compress(kc_tok.reshape(b, nch, CMP_STRIDE * KV_W), vc_tok.reshape(b, nch, CMP_STRIDE * KV_W),
                        pe_halves(pe_k), pe_halves(pe_v), *w1_halves(w1_k), *w1_halves(w1_v),
                        w2_bd(w2_k).astype(BF16), w2_bd(w2_v).T.astype(BF16), jnp.asarray(cfeat, F32))

    n_cmp = (s - CMP_LEN) // CMP_STRIDE + 1
    starts = np.arange(nch) * CMP_STRIDE
    ss = np.arange(n_slc) * SLC_LEN
    ov = np.clip(np.minimum(starts[:, None] + CMP_LEN, ss[None, :] + SLC_LEN)
                 - np.maximum(starts[:, None], ss[None, :]), 0, None) / CMP_LEN
    ov[n_cmp:] = 0.0
    ovT = jnp.asarray(ov.T, BF16)
    slopes_np = (2.0 ** (-8.0 * np.arange(1, NSA_HEADS + 1) / NSA_HEADS)).astype(np.float32)
    full = (slopes_np * np.float32(LOG2E)).astype(np.float32)
    pieces, rest = [], full.copy()
    for _ in range(3):
        piece = rest.astype(BF16).astype(np.float32)
        pieces.append(piece)
        rest = (rest - piece).astype(np.float32)
    tab = jnp.asarray(np.stack(pieces + [full], axis=1).reshape(-1), F32)

    ocT, bias, act = _cmp_topk(tab, qT, kc, vcT, ovT, gT, min(N_SEL, n_slc))
    act = act[:, :, :, 0, :s // ATTN_TILE].reshape(-1)
    yT = _slc_win(act, tab, qT, ks, vsT, kw, vwT, bias, ocT, gT)
    return _post(x, None, yT, None, w_out.astype(BF16), _row(g_post), _row(g_mlp_pre),
                 w1.astype(BF16), w2.astype(BF16), _row(g_mlp_post))


def kernel(x, norm_mix_pre, norm_mix_post, norm_mlp_pre, norm_mlp_post, mlp_w1, mlp_w2, a_w_in, a_conv_w, a_q_norm,
           a_w_q_up, a_kv_norm, a_w_kv_up, a_w_out, c_w_in, c_cmp_pe_k, c_cmp_w1_k, c_cmp_w2_k, c_cmp_pe_v,
           c_cmp_w1_v, c_cmp_w2_v, c_w_out):
    depth = norm_mix_pre.shape[0]
    for layer in range(depth):
        i = layer // 2
        common = (norm_mix_post[layer], norm_mlp_pre[layer], mlp_w1[layer], mlp_w2[layer], norm_mlp_post[layer])
        if layer % 2 == 0:
            x = _layer_a(x, norm_mix_pre[layer], a_w_in[i], a_conv_w[i], a_q_norm[i], a_w_q_up[i], a_kv_norm[i],
                         a_w_kv_up[i], a_w_out[i], *common)
        else:
            x = _layer_c(x, norm_mix_pre[layer], c_w_in[i], c_cmp_pe_k[i], c_cmp_w1_k[i], c_cmp_w2_k[i],
                         c_cmp_pe_v[i], c_cmp_w1_v[i], c_cmp_w2_v[i], c_w_out[i], *common)
    return x
```

```python
import functools
import math

import numpy as np
import jax
import jax.numpy as jnp
from jax import lax
from jax.experimental import pallas as pl
from jax.experimental.pallas import tpu as pltpu

F32 = jnp.float32
BF16 = jnp.bfloat16

D_MODEL = 1024
D_FF = 4 * D_MODEL
EPS = 1e-6
NEG = -1e30
LOG2E = math.log2(math.e)

CONV_WIDTH = 512
CONV_K = 3
MLA_HEADS = 8
MLA_NOPE = 64
MLA_ROPE = 32
MLA_V = 64
MLA_KV_RANK = 256
MLA_Q_RANK = 768
MLA_SLOT = 128
ROPE_THETA = 10000.0
ROPE_HALF = MLA_ROPE // 2

NSA_HEADS = 16
NSA_GROUPS = 4
NSA_REP = 4
NSA_DH = 64
CMP_LEN = 32
CMP_STRIDE = 16
SLC_LEN = 64
SLC_SHIFT = 6
N_SEL = 16
WINDOW = 512
FORCE_BONUS = 1e4
KV_W = NSA_GROUPS * NSA_DH
C_MIX = NSA_HEADS * NSA_DH
GATE_ROWS = 16
ONES_ROWS = 16
V_ROWS = 64 + ONES_ROWS
LANES = 128
SUBLANES = 8
KFEAT = 16

TOKEN_TILE = 512
ATTN_TILE = 256
MLA_HEADS_PER_STEP = 8
NSA_GROUPS_PER_STEP = 2
FF_CHUNK = 1024
VMEM_LIMIT = 56 * 1024 * 1024


def _cparams(sem):
    return pltpu.CompilerParams(dimension_semantics=sem, vmem_limit_bytes=VMEM_LIMIT)


def _const_spec(shape):
    nd = len(shape)
    return pl.BlockSpec(shape, lambda *_: (0,) * nd)


def _rms(x, g):
    ms = jnp.mean(x * x, axis=-1, keepdims=True)
    return x * lax.rsqrt(ms + EPS) * g


def _nt(a, b):
    return lax.dot_general(a, b, (((1,), (1,)), ((), ())), preferred_element_type=F32)


def _tn(a, b):
    return lax.dot_general(a, b, (((0,), (0,)), ((), ())), preferred_element_type=F32)


def _ones_rows(n):
    return jnp.where(lax.broadcasted_iota(jnp.int32, (ONES_ROWS, n), 0) == 0, 1.0, 0.0).astype(F32)


def _rope_rows(r1, r2, cos, sin):
    return r1 * cos - r2 * sin, r2 * cos + r1 * sin


def _a_pre_kernel(x_ref, g_ref, w1_ref, wkr_ref, convw_ref, qn_ref, wq_ref, kvn_ref, wk_ref, wv_ref,
                  cos_ref, sin_ref, yconv_ref, qT_ref, k_ref, vT_ref, carry_ref, *, tm, q_scale):
    @pl.when(pl.program_id(1) == 0)
    def _():
        carry_ref[...] = jnp.zeros_like(carry_ref)

    sub = min(ATTN_TILE, tm)
    w = convw_ref[...]
    w0, w1, w2 = w[0:1], w[1:2], w[2:3]
    hd = 16
    tail = carry_ref[...]
    z32 = jnp.zeros((MLA_SLOT - MLA_NOPE - MLA_ROPE, sub), F32)
    ones = _ones_rows(sub)
    o1, o2, o3 = CONV_WIDTH, 2 * CONV_WIDTH, 3 * CONV_WIDTH
    o4 = o3 + MLA_Q_RANK
    for cidx in range(tm // sub):
        rows = slice(cidx * sub, (cidx + 1) * sub)
        xn = _rms(x_ref[0, rows, :], g_ref[...]).astype(BF16)
        p1 = jnp.dot(xn, w1_ref[...], preferred_element_type=F32)
        g_b, g_c, hv = p1[:, :o1], p1[:, o1:o2], p1[:, o2:o3]
        c_q, c_kv = p1[:, o3:o4], p1[:, o4:]

        u = g_c * hv
        y = w2 * u + w1 * pltpu.roll(u, 1, 0) + w0 * pltpu.roll(u, 2, 0)
        uh = u[0:hd]
        row = lax.broadcasted_iota(jnp.int32, (hd, CONV_WIDTH), 0)
        h1 = jnp.where(row < 1, pltpu.roll(tail, 1, 0), pltpu.roll(uh, 1, 0))
        h2 = jnp.where(row < 2, pltpu.roll(tail, 2, 0), pltpu.roll(uh, 2, 0))
        yh = w2 * uh + w1 * h1 + w0 * h2
        yconv_ref[0, cidx * sub + hd:(cidx + 1) * sub, :] = (g_b[hd:] * y[hd:]).astype(BF16)
        yconv_ref[0, cidx * sub:cidx * sub + hd, :] = (g_b[0:hd] * yh).astype(BF16)
        tail = u[sub - hd:sub]

        cos = cos_ref[:, rows]
        sin = sin_ref[:, rows]

        hq = _rms(c_q, qn_ref[...]).astype(BF16)
        qT = _nt(wq_ref[...], hq)
        for h in range(MLA_HEADS):
            b0 = MLA_SLOT * h
            r1 = qT[b0 + MLA_NOPE:b0 + MLA_NOPE + ROPE_HALF]
            r2 = qT[b0 + MLA_NOPE + ROPE_HALF:b0 + MLA_NOPE + MLA_ROPE]
            e1, e2 = _rope_rows(r1, r2, cos, sin)
            slot = jnp.concatenate([qT[b0:b0 + MLA_NOPE], e1, e2, z32], axis=0) * q_scale
            qT_ref[0, b0:b0 + MLA_SLOT, rows] = slot.astype(BF16)

        krT = _nt(wkr_ref[...], xn)
        r1 = krT[MLA_NOPE:MLA_NOPE + ROPE_HALF]
        r2 = krT[MLA_NOPE + ROPE_HALF:MLA_NOPE + MLA_ROPE]
        e1, e2 = _rope_rows(r1, r2, cos, sin)
        slotT = jnp.concatenate([jnp.zeros((MLA_NOPE, sub), F32), e1, e2, z32], axis=0)
        slot = slotT.T

        hkv = _rms(c_kv, kvn_ref[...]).astype(BF16)
        kn = jnp.dot(hkv, wk_ref[...], preferred_element_type=F32)
        k = kn + jnp.concatenate([slot] * MLA_HEADS, axis=1)
        k_ref[0, rows, :] = k.astype(BF16)
        vT = _nt(wv_ref[...], hkv)
        vT = jnp.concatenate([x for h in range(MLA_HEADS) for x in (vT[h * MLA_V:(h + 1) * MLA_V], ones)], axis=0)
        vT_ref[0, cidx] = vT.astype(BF16)
    carry_ref[...] = tail


def _a_pre(x, g, w1, wkr, convw, qn, wq, kvn, wk, wv, cos, sin):
    b, s, d = x.shape
    tm = min(TOKEN_TILE, s)
    hw = MLA_HEADS * MLA_SLOT
    vw = MLA_HEADS * V_ROWS
    q_scale = (MLA_NOPE + MLA_ROPE) ** -0.5 * LOG2E
    kern = functools.partial(_a_pre_kernel, tm=tm, q_scale=q_scale)
    return pl.pallas_call(
        kern,
        grid=(b, s // tm),
        in_specs=[
            pl.BlockSpec((1, tm, d), lambda i, j: (i, j, 0)),
            _const_spec(g.shape), _const_spec(w1.shape), _const_spec(wkr.shape), _const_spec(convw.shape),
            _const_spec(qn.shape), _const_spec(wq.shape), _const_spec(kvn.shape), _const_spec(wk.shape),
            _const_spec(wv.shape),
            pl.BlockSpec((ROPE_HALF, tm), lambda i, j: (0, j)),
            pl.BlockSpec((ROPE_HALF, tm), lambda i, j: (0, j)),
        ],
        out_specs=[
            pl.BlockSpec((1, tm, CONV_WIDTH), lambda i, j: (i, j, 0)),
            pl.BlockSpec((1, hw, tm), lambda i, j: (i, 0, j)),
            pl.BlockSpec((1, tm, hw), lambda i, j: (i, j, 0)),
            pl.BlockSpec((1, tm // ATTN_TILE, vw, ATTN_TILE), lambda i, j: (i, j, 0, 0)),
        ],
        out_shape=[
            jax.ShapeDtypeStruct((b, s, CONV_WIDTH), BF16),
            jax.ShapeDtypeStruct((b, hw, s), BF16),
            jax.ShapeDtypeStruct((b, s, hw), BF16),
            jax.ShapeDtypeStruct((b, s // ATTN_TILE, vw, ATTN_TILE), BF16),
        ],
        scratch_shapes=[pltpu.VMEM((16, CONV_WIDTH), F32)],
        compiler_params=_cparams(("arbitrary", "arbitrary")),
        name="a_pre",
    )(x, g, w1, wkr, convw, qn, wq, kvn, wk, wv, cos, sin)


def _tile_pos(t, k0, q0):
    kpos = k0 + lax.broadcasted_iota(jnp.int32, (t, t), 0)
    qpos = q0 + lax.broadcasted_iota(jnp.int32, (t, t), 1)
    return kpos, qpos


def _stage_scores(s, s_slot):
    s_slot[...] = s
    return jnp.max(s, axis=0, keepdims=True)


def _stage_softmax(s_slot, m, mt):
    m_new = jnp.maximum(m, mt)
    return jnp.exp2(s_slot[...] - m_new).astype(BF16), m_new, jnp.exp2(m - m_new)


def _stage_pv(acc_slot, vT, p, alpha):
    acc_slot[...] = alpha * acc_slot[...] + jnp.dot(vT, p, preferred_element_type=F32)


def _causal_flash(i, n_rest, tile_at, t, chains, scores, vtile, diag_extra, s_ref, acc_ref):
    mt0 = []
    for c in chains:
        mt0.append(_stage_scores(scores(c, i) + diag_extra, s_ref.at[c]))
        acc_ref[c] = jnp.zeros((V_ROWS, t), F32)
    init = (tuple(mt0), tuple(jnp.full((1, t), NEG, F32) for _ in chains))
    last = jnp.maximum(n_rest - 1, 0)

    def body(n, carry):
        mt, m = carry
        j_cur = jnp.where(n == 0, i, tile_at(jnp.maximum(n - 1, 0)))
        j_next = tile_at(jnp.minimum(n, last))
        m_new, mt_next = [], []
        for c in chains:
            p, mn, alpha = _stage_softmax(s_ref.at[c], m[c], mt[c])
            m_new.append(mn)
            mt_next.append(_stage_scores(scores(c, j_next), s_ref.at[c]))
            _stage_pv(acc_ref.at[c], vtile(c, j_cur), p, alpha)
        return tuple(mt_next), tuple(m_new)

    n_iter = n_rest + 1
    quads = lax.shift_right_logical(n_iter, 2)

    def quad(k, cr):
        for u in range(4):
            cr = body(4 * k + u, cr)
        return cr

    carry = lax.fori_loop(0, quads, quad, init)
    lax.fori_loop(4 * quads, n_iter, body, carry)


def _normalized(acc_slot):
    acc = acc_slot[...]
    return acc[:V_ROWS - ONES_ROWS] / acc[V_ROWS - ONES_ROWS:V_ROWS - ONES_ROWS + 1]


def _mla_attn_kernel(qT_ref, k_ref, vT_ref, oT_ref, s_ref, acc_ref, *, t, hb):
    i = pl.program_id(2)

    def scores(h, j):
        k = k_ref[0, pl.ds(pl.multiple_of(j * t, t), t), h * MLA_SLOT:(h + 1) * MLA_SLOT]
        return jnp.dot(k, qT_ref[0, h * MLA_SLOT:(h + 1) * MLA_SLOT, :], preferred_element_type=F32)

    def vtile(h, j):
        return vT_ref[0, j, h * V_ROWS:(h + 1) * V_ROWS, :]

    kpos, qpos = _tile_pos(t, 0, 0)
    causal = jnp.where(kpos <= qpos, 0.0, NEG)
    _causal_flash(i, i, lambda n: n, t, range(hb), scores, vtile, causal, s_ref, acc_ref)
    for h in range(hb):
        oT_ref[0, h * MLA_V:(h + 1) * MLA_V, :] = _normalized(acc_ref.at[h]).astype(BF16)


def _mla_attn(qT, k, vT):
    b, hw, s = qT.shape
    t = min(ATTN_TILE, s)
    nkv = s // t
    hb = MLA_HEADS_PER_STEP
    kern = functools.partial(_mla_attn_kernel, t=t, hb=hb)
    return pl.pallas_call(
        kern,
        grid=(b, MLA_HEADS // hb, s // t),
        in_specs=[
            pl.BlockSpec((1, hb * MLA_SLOT, t), lambda bi, h, i: (bi, h, i)),
            pl.BlockSpec((1, s, hb * MLA_SLOT), lambda bi, h, i: (bi, 0, h)),
            pl.BlockSpec((1, nkv, hb * V_ROWS, t), lambda bi, h, i: (bi, 0, h, 0)),
        ],
        out_specs=pl.BlockSpec((1, hb * MLA_V, t), lambda bi, h, i: (bi, h, i)),
        out_shape=jax.ShapeDtypeStruct((b, MLA_HEADS * MLA_V, s), BF16),
        scratch_shapes=[pltpu.VMEM((hb, t, t), F32), pltpu.VMEM((hb, V_ROWS, t), F32)],
        compiler_params=_cparams(("arbitrary", "arbitrary", "arbitrary")),
        name="mla_attn",
    )(qT, k, vT)


def _post_kernel(*refs, has_tok):
    if has_tok:
        x_ref, ytok_ref, yT_ref, wo_tok_ref, wo_ref, gpost_ref, gpre_ref, w1_ref, w2_ref, gmlp_ref, o_ref = refs
    else:
        x_ref, yT_ref, wo_ref, gpost_ref, gpre_ref, w1_ref, w2_ref, gmlp_ref, o_ref = refs
    x = x_ref[0]
    mix = _tn(yT_ref[0], wo_ref[...])
    if has_tok:
        mix = mix + jnp.dot(ytok_ref[0], wo_tok_ref[...], preferred_element_type=F32)
    x1 = x + _rms(mix, gpost_ref[...])
    h = _rms(x1, gpre_ref[...]).astype(BF16)
    acc = jnp.zeros_like(x1)
    for c in range(D_FF // FF_CHUNK):
        a = jnp.dot(h, w1_ref[:, c * FF_CHUNK:(c + 1) * FF_CHUNK], preferred_element_type=F32)
        a = jnp.maximum(a, 0.0)
        a = (a * a).astype(BF16)
        acc = acc + jnp.dot(a, w2_ref[c * FF_CHUNK:(c + 1) * FF_CHUNK, :], preferred_element_type=F32)
    o_ref[0] = x1 + _rms(acc, gmlp_ref[...])


def _post(x, ytok, yT, wo_tok, wo, gpost, gpre, w1, w2, gmlp):
    b, s, d = x.shape
    tm = min(TOKEN_TILE, s)
    has_tok = ytok is not None
    kern = functools.partial(_post_kernel, has_tok=has_tok)
    xspec = pl.BlockSpec((1, tm, d), lambda i, j: (i, j, 0))
    ins, specs = [x], [xspec]
    if has_tok:
        ins.append(ytok)
        specs.append(pl.BlockSpec((1, tm, ytok.shape[-1]), lambda i, j: (i, j, 0)))
    ins.append(yT)
    specs.append(pl.BlockSpec((1, yT.shape[1], tm), lambda i, j: (i, 0, j)))
    if has_tok:
        ins.append(wo_tok)
        specs.append(_const_spec(wo_tok.shape))
    for a in (wo, gpost, gpre, w1, w2, gmlp):
        ins.append(a)
        specs.append(_const_spec(a.shape))
    return pl.pallas_call(
        kern,
        grid=(b, s // tm),
        in_specs=specs,
        out_specs=xspec,
        out_shape=jax.ShapeDtypeStruct(x.shape, x.dtype),
        compiler_params=_cparams(("arbitrary", "arbitrary")),
        name="post_tok" if has_tok else "post",
    )(*ins)


def _c_pre_kernel(x_ref, g_ref, wq_ref, wtok_ref, wf_ref, kfs_ref, kfw_ref, qT_ref, kc_ref, vc_ref, ks_ref, kw_ref,
                  vsT_ref, vwT_ref, gT_ref, *, tm, q_scale):
    xn = _rms(x_ref[0], g_ref[...]).astype(BF16)
    qT_ref[0] = (_nt(wq_ref[...], xn) * q_scale).astype(BF16)
    tok = jnp.dot(xn, wtok_ref[...], preferred_element_type=F32)
    kc_ref[0] = tok[:, :KV_W].astype(BF16)
    vc_ref[0] = tok[:, KV_W:2 * KV_W].astype(BF16)
    kfs = kfs_ref[...]
    kfw = kfw_ref[...]
    for g in range(NSA_GROUPS):
        a = 2 * KV_W + g * NSA_DH
        ks_ref[0, g] = jnp.concatenate([tok[:, a:a + NSA_DH], kfs], axis=1).astype(BF16)
        kw_ref[0, g] = jnp.concatenate([tok[:, a + KV_W:a + KV_W + NSA_DH], kfw], axis=1).astype(BF16)
    fT = _nt(wf_ref[...], xn)
    ones = _ones_rows(tm)

    def with_ones(v):
        return jnp.concatenate([x for g in range(NSA_GROUPS) for x in (v[g * NSA_DH:(g + 1) * NSA_DH], ones)],
                               axis=0)

    vs = with_ones(fT[:KV_W])
    vw = with_ones(fT[KV_W:2 * KV_W])
    for cidx in range(tm // ATTN_TILE):
        sl = slice(cidx * ATTN_TILE, (cidx + 1) * ATTN_TILE)
        vsT_ref[0, cidx] = vs[:, sl].astype(BF16)
        vwT_ref[0, cidx] = vw[:, sl].astype(BF16)
    gT_ref[0] = 1.0 / (1.0 + jnp.exp(-fT[2 * KV_W:]))


def _c_pre(x, g, wq, wtok, wf, kfs, kfw):
    b, s, d = x.shape
    tm = min(TOKEN_TILE, s)
    q_scale = NSA_DH ** -0.5 * LOG2E
    kern = functools.partial(_c_pre_kernel, tm=tm, q_scale=q_scale)
    nt = tm // ATTN_TILE
    grows = NSA_GROUPS * GATE_ROWS
    kaug = NSA_DH + kfs.shape[1]
    vrows = NSA_GROUPS * V_ROWS
    kf_spec = pl.BlockSpec((tm, kfs.shape[1]), lambda i, j: (j, 0))
    return pl.pallas_call(
        kern,
        grid=(b, s // tm),
        in_specs=[pl.BlockSpec((1, tm, d), lambda i, j: (i, j, 0)),
                  _const_spec(g.shape), _const_spec(wq.shape), _const_spec(wtok.shape), _const_spec(wf.shape),
                  kf_spec, kf_spec],
        out_specs=[
            pl.BlockSpec((1, C_MIX, tm), lambda i, j: (i, 0, j)),
            pl.BlockSpec((1, tm, KV_W), lambda i, j: (i, j, 0)),
            pl.BlockSpec((1, tm, KV_W), lambda i, j: (i, j, 0)),
            pl.BlockSpec((1, NSA_GROUPS, tm, kaug), lambda i, j: (i, 0, j, 0)),
            pl.BlockSpec((1, NSA_GROUPS, tm, kaug), lambda i, j: (i, 0, j, 0)),
            pl.BlockSpec((1, nt, vrows, ATTN_TILE), lambda i, j: (i, j, 0, 0)),
            pl.BlockSpec((1, nt, vrows, ATTN_TILE), lambda i, j: (i, j, 0, 0)),
            pl.BlockSpec((1, grows, tm), lambda i, j: (i, 0, j)),
        ],
        out_shape=[
            jax.ShapeDtypeStruct((b, C_MIX, s), BF16),
            jax.ShapeDtypeStruct((b, s, KV_W), BF16),
            jax.ShapeDtypeStruct((b, s, KV_W), BF16),
            jax.ShapeDtypeStruct((b, NSA_GROUPS, s, kaug), BF16),
            jax.ShapeDtypeStruct((b, NSA_GROUPS, s, kaug), BF16),
            jax.ShapeDtypeStruct((b, s // ATTN_TILE, vrows, ATTN_TILE), BF16),
            jax.ShapeDtypeStruct((b, s // ATTN_TILE, vrows, ATTN_TILE), BF16),
            jax.ShapeDtypeStruct((b, grows, s), F32),
        ],
        compiler_params=_cparams(("arbitrary", "arbitrary")),
        name="c_pre",
    )(x, g, wq, wtok, wf, kfs, kfw)


def _gelu_tanh(x):
    return 0.5 * x * (1.0 + jnp.tanh(math.sqrt(2.0 / math.pi) * (x + 0.044715 * (x * x * x))))


def _compress_kernel(zk_ref, zv_ref, pek_ref, pev_ref, wak_ref, wbk_ref, wav_ref, wbv_ref, w2k_ref, w2vT_ref, kf_ref,
                     kc_ref, vcT_ref, *, nch):
    def hidden(z_ref, pe_ref, wa_ref, wb_ref):
        z = z_ref[0].astype(F32)
        za = (z + pe_ref[0:1]).astype(BF16)
        zb = (z + pe_ref[1:2]).astype(BF16)
        a = jnp.dot(za, wa_ref[...], preferred_element_type=F32)
        bm = jnp.dot(zb, wb_ref[...], preferred_element_type=F32)
        return _gelu_tanh(a + pltpu.roll(bm, nch - 1, 0)).astype(BF16)

    hk = hidden(zk_ref, pek_ref, wak_ref, wbk_ref)
    kc = jnp.dot(hk, w2k_ref[...], preferred_element_type=F32)
    kf = kf_ref[...]
    hv = hidden(zv_ref, pev_ref, wav_ref, wbv_ref)
    vcT = _nt(w2vT_ref[...], hv)
    ones = _ones_rows(nch)
    for g in range(NSA_GROUPS):
        gs = slice(g * NSA_DH, (g + 1) * NSA_DH)
        kc_ref[0, g] = jnp.concatenate([kc[:, gs], kf], axis=1).astype(BF16)
        vcT_ref[0, g] = jnp.concatenate([vcT[gs], ones], axis=0).astype(BF16)


def _compress(zk, zv, pek, pev, wak, wbk, wav, wbv, w2k, w2vT, kf):
    b, nch, cw = zk.shape
    kern = functools.partial(_compress_kernel, nch=nch)
    zspec = pl.BlockSpec((1, nch, cw), lambda i: (i, 0, 0))
    consts = (pek, pev, wak, wbk, wav, wbv, w2k, w2vT, kf)
    return pl.pallas_call(
        kern,
        grid=(b,),
        in_specs=[zspec, zspec] + [_const_spec(a.shape) for a in consts],
        out_specs=[pl.BlockSpec((1, NSA_GROUPS, nch, NSA_DH + KFEAT), lambda i: (i, 0, 0, 0)),
                   pl.BlockSpec((1, NSA_GROUPS, V_ROWS, nch), lambda i: (i, 0, 0, 0))],
        out_shape=[jax.ShapeDtypeStruct((b, NSA_GROUPS, nch, NSA_DH + KFEAT), BF16),
                   jax.ShapeDtypeStruct((b, NSA_GROUPS, V_ROWS, nch), BF16)],
        compiler_params=_cparams(("arbitrary",)),
        name="compress",
    )(zk, zv, *consts)


def _split3(x):
    hi = x.astype(BF16)
    r = x - hi.astype(F32)
    mid = r.astype(BF16)
    lo = (r - mid.astype(F32)).astype(BF16)
    return hi, mid, lo


def _feature_rows(tab_ref, head, const, t):
    row = lax.broadcasted_iota(jnp.int32, (KFEAT, t), 0)
    r3 = row - 3 * (row >= 3).astype(jnp.int32) - 3 * (row >= 6).astype(jnp.int32)
    s_hi, s_mid, s_lo, slope = (tab_ref[head * 4 + k] for k in range(4))
    c_hi, c_mid, c_lo = (x.astype(F32) for x in _split3(slope * const))
    svals = jnp.where(r3 == 0, s_hi, jnp.where(r3 == 1, s_mid, s_lo))
    cvals = jnp.where(r3 == 0, c_hi, jnp.where(r3 == 1, c_mid, c_lo))
    return jnp.where(row < 6, svals, jnp.where(row < 9, cvals, 0.0)).astype(BF16)


def _cmp_topk_kernel(tab_ref, qT_ref, kc_ref, vcT_ref, pool_ref, gT_ref, ocT_ref, bias_ref, act_ref,
                     qa_ref, e_ref, acc_ref, imp_ref, rank_ref, ps_ref, *, t, gb, nch, nslc, nsel, ov_terms):
    gp = pl.program_id(1)
    i = pl.program_id(2)
    q0 = i * t
    chains = range(gb * NSA_REP)
    bpt = t // SLC_LEN

    const = (CMP_LEN - 1) / 2.0 - jnp.full((KFEAT, t), q0, jnp.int32).astype(F32)
    for c in chains:
        head = (gp * gb + c // NSA_REP) * NSA_REP + c % NSA_REP
        qa_ref[c, 0:NSA_DH, :] = qT_ref[0, c * NSA_DH:(c + 1) * NSA_DH, :]
        qa_ref[c, NSA_DH:NSA_DH + KFEAT, :] = _feature_rows(tab_ref, head, const, t)

    n_i = lax.broadcasted_iota(jnp.int32, (nch, t), 0)
    t_i = q0 + lax.broadcasted_iota(jnp.int32, (nch, t), 1)
    ended = jnp.where(n_i * CMP_STRIDE + (CMP_LEN - 1) <= t_i, 0.0, NEG)
    def cmp_scores(c):
        s = jnp.dot(kc_ref[0, c // NSA_REP], qa_ref[c], preferred_element_type=F32) + ended
        return _stage_scores(s, e_ref.at[c])

    def cmp_softmax_pv(c, mt):
        m_eff = jnp.where(mt > 0.5 * NEG, mt, 0.0)
        e = jnp.exp2(e_ref[c] - m_eff)
        e_ref[c] = e
        acc_ref[c] = jnp.dot(vcT_ref[0, c // NSA_REP], e.astype(BF16), preferred_element_type=F32)

    mt = [cmp_scores(c) for c in chains]
    for c in chains:
        cmp_softmax_pv(c, mt[c])

    j_i = lax.broadcasted_iota(jnp.int32, (nslc, t), 0)
    cur = jnp.right_shift(q0 + lax.broadcasted_iota(jnp.int32, (nslc, t), 1), SLC_SHIFT)
    forced = (j_i == 0) | (j_i == cur) | (j_i == cur - 1)
    per_blk = SLC_LEN // CMP_STRIDE
    for gl in range(gb):
        psum = jnp.zeros((nch, t), F32)
        for r in range(NSA_REP):
            c = gl * NSA_REP + r
            acc = acc_ref[c]
            l = acc[NSA_DH:NSA_DH + 1]
            inv = 1.0 / jnp.where(l > 0.0, l, 1.0)
            gate = gT_ref[0, gl * GATE_ROWS + r:gl * GATE_ROWS + r + 1, :]
            ocT_ref[0, c * NSA_DH:(c + 1) * NSA_DH, :] = (gate * (acc[:NSA_DH] * inv)).astype(BF16)
            psum = psum + e_ref[c] * inv
        for h in range(t // LANES):
            ps_ref[h] = psum[:, h * LANES:(h + 1) * LANES]
        imp = jnp.zeros((nslc, t), F32)
        for k, wgt in ov_terms:
            rows = jnp.concatenate([ps_ref[h, pl.ds(k % per_blk, nslc, stride=per_blk), :]
                                    for h in range(t // LANES)], axis=1)
            if k < 0:
                rows = jnp.where(j_i == 0, 0.0, pltpu.roll(rows, 1, 0))
            imp = imp + wgt * rows
        imp_ref[gl] = jnp.where(j_i > cur, NEG, imp + jnp.where(forced, FORCE_BONUS, 0.0))
        rank_ref[gl] = jnp.zeros((nslc, t), F32)

    for chunk in range(nslc // bpt):
        @pl.when(chunk <= i)
        def _():
            for gl in range(gb):
                imp = imp_ref[gl]
                rank = rank_ref[gl]
                for jp in range(chunk * bpt, (chunk + 1) * bpt):
                    row = imp[jp:jp + 1, :]
                    incs = []
                    for rg in range(nslc // 8):
                        blk = imp[8 * rg:8 * rg + 8]
                        if 8 * rg > jp:
                            incs.append(jnp.where(row >= blk, 1.0, 0.0))
                        elif 8 * rg + 7 <= jp:
                            incs.append(jnp.where(row > blk, 1.0, 0.0))
                        else:
                            above = lax.broadcasted_iota(jnp.int32, (8, t), 0) > jp - 8 * rg
                            incs.append(jnp.where(above, jnp.where(row >= blk, 1.0, 0.0),
                                                  jnp.where(row > blk, 1.0, 0.0)))
                    rank = rank + jnp.concatenate(incs, axis=0)
                rank_ref[gl] = rank

    best = jnp.full((nslc, t), NEG, F32)
    for gl in range(gb):
        bias = jnp.where(j_i > cur, NEG, jnp.where(rank_ref[gl] < nsel, 0.0, NEG))
        bias_ref[0, gl] = bias
        best = jnp.maximum(best, bias)
    hit = jnp.where(jnp.max(best, axis=1, keepdims=True) == 0.0, 1.0, 0.0)
    tiles = jnp.sum(hit * pool_ref[...], axis=0, keepdims=True)
    act_ref[0, 0, 0] = jnp.broadcast_to(jnp.where(tiles > 0.0, 1, 0), (8, 128)).astype(jnp.int32)


def _overlap_terms():
    per_blk = SLC_LEN // CMP_STRIDE
    terms = []
    for k in range(-(CMP_LEN // CMP_STRIDE) + 1, per_blk):
        lo, hi = max(k * CMP_STRIDE, 0), min(k * CMP_STRIDE + CMP_LEN, SLC_LEN)
        if hi > lo:
            terms.append((k, (hi - lo) / CMP_LEN))
    assert SLC_LEN % CMP_STRIDE == 0 and all(k >= -per_blk for k, _ in terms)
    return tuple(terms)


def _cmp_topk(tab, qT, kc, vcT, gT, nsel):
    b, _, s = qT.shape
    t = ATTN_TILE
    assert s % t == 0 and t % SLC_LEN == 0
    nch = kc.shape[2]
    nslc = s // SLC_LEN
    gb = NSA_GROUPS_PER_STEP
    nc = gb * NSA_REP
    kern = functools.partial(_cmp_topk_kernel, t=t, gb=gb, nch=nch, nslc=nslc, nsel=nsel,
                             ov_terms=_overlap_terms())
    gw = nc * NSA_DH
    pool = np.zeros((nslc, 128), np.float32)
    pool[np.arange(nslc), np.arange(nslc) // (t // SLC_LEN)] = 1.0
    pool = jnp.asarray(pool, F32)
    return pl.pallas_call(
        kern,
        grid=(b, NSA_GROUPS // gb, s // t),
        in_specs=[
            pl.BlockSpec(memory_space=pltpu.SMEM),
            pl.BlockSpec((1, gw, t), lambda bi, g, i: (bi, g, i)),
            pl.BlockSpec((1, gb, nch, NSA_DH + KFEAT), lambda bi, g, i: (bi, g, 0, 0)),
            pl.BlockSpec((1, gb, V_ROWS, nch), lambda bi, g, i: (bi, g, 0, 0)),
            _const_spec(pool.shape),
            pl.BlockSpec((1, gb * GATE_ROWS, t), lambda bi, g, i: (bi, g, i)),
        ],
        out_specs=[pl.BlockSpec((1, gw, t), lambda bi, g, i: (bi, g, i)),
                   pl.BlockSpec((1, gb, nslc, t), lambda bi, g, i: (bi, g, 0, i)),
                   pl.BlockSpec((1, 1, 1, 8, 128), lambda bi, g, i: (bi, g, i, 0, 0))],
        out_shape=[jax.ShapeDtypeStruct((b, C_MIX, s), BF16),
                   jax.ShapeDtypeStruct((b, NSA_GROUPS, nslc, s), F32),
                   jax.ShapeDtypeStruct((b, NSA_GROUPS // gb, s // t, 8, 128), jnp.int32)],
        scratch_shapes=[pltpu.VMEM((nc, NSA_DH + KFEAT, t), BF16),
                        pltpu.VMEM((nc, nch, t), F32),
                        pltpu.VMEM((nc, V_ROWS, t), F32),
                        pltpu.VMEM((gb, nslc, t), F32), pltpu.VMEM((gb, nslc, t), F32),
                        pltpu.VMEM((t // LANES, nch, LANES), F32)],
        compiler_params=_cparams(("arbitrary", "arbitrary", "arbitrary")),
        name="cmp_topk",
    )(tab, qT, kc, vcT, pool, gT)


def _slc_win_kernel(act_ref, tab_ref, qT_ref, ks_ref, vsT_ref, kw_ref, vwT_ref, bias_ref, ocT_ref, gT_ref, yT_ref,
                    qa_ref, s_ref, acc_ref, sw_ref, accw_ref, idx_ref, *, t, gb, nslc, nkv):
    gp = pl.program_id(1)
    i = pl.program_id(2)
    chains = range(gb * NSA_REP)
    feat0 = NSA_DH + nslc

    const = -jnp.full((KFEAT, t), i * t, jnp.int32).astype(F32)
    for c in chains:
        gl, r = divmod(c, NSA_REP)
        head = (gp * gb + gl) * NSA_REP + r
        qa_ref[c, 0:NSA_DH, :] = qT_ref[0, c * NSA_DH:(c + 1) * NSA_DH, :]
        qa_ref[c, NSA_DH:feat0, :] = bias_ref[0, gl].astype(BF16)
        qa_ref[c, feat0:feat0 + KFEAT, :] = _feature_rows(tab_ref, head, const, t)

    def scores_from(k_ref):
        def scores(c, j):
            k = k_ref[0, c // NSA_REP, pl.ds(pl.multiple_of(j * t, t), t), :]
            return jnp.dot(k, qa_ref[c], preferred_element_type=F32)
        return scores

    def vtile_from(vT_ref):
        def vtile(c, j):
            gl = c // NSA_REP
            return vT_ref[0, j, gl * V_ROWS:(gl + 1) * V_ROWS, :]
        return vtile

    kpos, qpos = _tile_pos(t, 0, 0)
    causal = jnp.where(kpos <= qpos, 0.0, NEG)

    w_scores = scores_from(kw_ref)
    w_vtile = vtile_from(vwT_ref)
    w_tiles = (i, jnp.maximum(i - 1, 0), jnp.maximum(i - 2, 0))
    in_seq1 = jnp.where((i - 1) * t + kpos >= 0, 0.0, NEG)
    in_seq2 = jnp.where((i - 2) * t + kpos >= 0, 0.0, NEG)
    w_extra = (causal, in_seq1, jnp.where(qpos < kpos, in_seq2, NEG))
    m = [jnp.full((1, t), NEG, F32) for _ in chains]
    mt = []
    for c in chains:
        accw_ref[c] = jnp.zeros((V_ROWS, t), F32)
        mt.append(_stage_scores(w_scores(c, w_tiles[0]) + w_extra[0], sw_ref.at[c]))
    for w in range(3):
        for c in chains:
            p, m[c], alpha = _stage_softmax(sw_ref.at[c], m[c], mt[c])
            if w + 1 < 3:
                mt[c] = _stage_scores(w_scores(c, w_tiles[w + 1]) + w_extra[w + 1], sw_ref.at[c])
            _stage_pv(accw_ref.at[c], w_vtile(c, w_tiles[w]), p, alpha)

    base = ((pl.program_id(0) * pl.num_programs(1) + gp) * pl.num_programs(2) + i) * nkv
    n_act = jnp.int32(0)
    for j in range(nkv):
        idx_ref[n_act] = j
        n_act = n_act + jnp.where((j < i) & (act_ref[base + j] != 0), 1, 0)
    _causal_flash(i, n_act, lambda n: idx_ref[n], t, chains, scores_from(ks_ref), vtile_from(vsT_ref), causal,
                  s_ref, acc_ref)

    for c in chains:
        gl, r = divmod(c, NSA_REP)
        g0 = gl * GATE_ROWS
        rs = slice(c * NSA_DH, (c + 1) * NSA_DH)
        y = (ocT_ref[0, rs, :].astype(F32)
             + gT_ref[0, g0 + NSA_REP + r:g0 + NSA_REP + r + 1, :] * _normalized(acc_ref.at[c])
             + gT_ref[0, g0 + 2 * NSA_REP + r:g0 + 2 * NSA_REP + r + 1, :] * _normalized(accw_ref.at[c]))
        yT_ref[0, rs, :] = y.astype(BF16)


def _slc_win(act, tab, qT, ks, vsT, kw, vwT, bias, ocT, gT):
    b, _, s = qT.shape
    t = ATTN_TILE
    assert s % t == 0 and WINDOW == 2 * t
    nkv = s // t
    nslc = bias.shape[2]
    kaug = ks.shape[3]
    gb = NSA_GROUPS_PER_STEP
    nc = gb * NSA_REP
    gw = nc * NSA_DH
    kern = functools.partial(_slc_win_kernel, t=t, gb=gb, nslc=nslc, nkv=nkv)
    qspec = pl.BlockSpec((1, gw, t), lambda bi, g, i, *_: (bi, g, i))
    kspec = pl.BlockSpec((1, gb, s, kaug), lambda bi, g, i, *_: (bi, g, 0, 0))
    vspec = pl.BlockSpec((1, nkv, gb * V_ROWS, t), lambda bi, g, i, *_: (bi, 0, g, 0))
    grid_spec = pltpu.PrefetchScalarGridSpec(
        num_scalar_prefetch=2,
        grid=(b, NSA_GROUPS // gb, s // t),
        in_specs=[
            qspec, kspec, vspec, kspec, vspec,
            pl.BlockSpec((1, gb, nslc, t), lambda bi, g, i, *_: (bi, g, 0, i)),
            qspec,
            pl.BlockSpec((1, gb * GATE_ROWS, t), lambda bi, g, i, *_: (bi, g, i)),
        ],
        out_specs=qspec,
        scratch_shapes=[pltpu.VMEM((nc, kaug, t), BF16),
                        pltpu.VMEM((nc, t, t), F32), pltpu.VMEM((nc, V_ROWS, t), F32),
                        pltpu.VMEM((nc, t, t), F32), pltpu.VMEM((nc, V_ROWS, t), F32),
                        pltpu.SMEM((nkv + 1,), jnp.int32)],
    )
    return pl.pallas_call(
        kern,
        grid_spec=grid_spec,
        out_shape=jax.ShapeDtypeStruct((b, C_MIX, s), BF16),
        compiler_params=_cparams(("arbitrary", "arbitrary", "arbitrary")),
        name="slc_win",
    )(act, tab, qT, ks, vsT, kw, vwT, bias, ocT, gT)


def _row(v):
    return v.reshape(1, -1).astype(F32)


def _layer_a(x, g_pre, w_in, conv_w, q_norm, w_q_up, kv_norm, w_kv_up, w_out, g_post, g_mlp_pre, w1, w2, g_mlp_post):
    b, s, d = x.shape
    o3 = 3 * CONV_WIDTH
    o5 = o3 + MLA_Q_RANK + MLA_KV_RANK
    pad = MLA_SLOT - MLA_NOPE - MLA_ROPE
    w_main = w_in[:, :o5].astype(BF16)
    wkr = jnp.pad(w_in[:, o5:].T, ((MLA_NOPE, pad), (0, 0))).astype(BF16)
    wq = w_q_up.reshape(MLA_Q_RANK, MLA_HEADS, MLA_NOPE + MLA_ROPE)
    wq = jnp.pad(wq, ((0, 0), (0, 0), (0, pad))).reshape(MLA_Q_RANK, MLA_HEADS * MLA_SLOT).T.astype(BF16)
    wkv = w_kv_up.reshape(MLA_KV_RANK, MLA_HEADS, MLA_NOPE + MLA_V)
    wk = jnp.pad(wkv[:, :, :MLA_NOPE], ((0, 0), (0, 0), (0, MLA_SLOT - MLA_NOPE)))
    wk = wk.reshape(MLA_KV_RANK, MLA_HEADS * MLA_SLOT).astype(BF16)
    wv = wkv[:, :, MLA_NOPE:].reshape(MLA_KV_RANK, MLA_HEADS * MLA_V).T.astype(BF16)

    inv = ROPE_THETA ** (-jnp.arange(ROPE_HALF, dtype=F32) / ROPE_HALF)
    ang = inv[:, None] * jnp.arange(s, dtype=F32)[None, :]
    cos, sin = jnp.cos(ang), jnp.sin(ang)

    yconv, qT, k, vT = _a_pre(x, _row(g_pre), w_main, wkr, conv_w.astype(F32), _row(q_norm), wq, _row(kv_norm),
                              wk, wv, cos, sin)
    yT = _mla_attn(qT, k, vT)
    wo = w_out.astype(BF16)
    return _post(x, yconv, yT, wo[:CONV_WIDTH], wo[CONV_WIDTH:], _row(g_post), _row(g_mlp_pre),
                 w1.astype(BF16), w2.astype(BF16), _row(g_mlp_post))


def _layer_c(x, g_pre, w_in, pe_k, w1_k, w2_k, pe_v, w1_v, w2_v, w_out, g_post, g_mlp_pre, w1, w2, g_mlp_post):
    b, s, d = x.shape
    G, R, Dh = NSA_GROUPS, NSA_REP, NSA_DH
    wq = w_in[:, :C_MIX].T.astype(BF16)
    kv = w_in[:, C_MIX:C_MIX + 6 * KV_W].reshape(d, 6, KV_W)
    wtok = jnp.concatenate([kv[:, 0], kv[:, 1], kv[:, 2], kv[:, 4]], axis=1).astype(BF16)
    wg = w_in[:, C_MIX + 6 * KV_W:].reshape(d, 3, G, R).transpose(0, 2, 1, 3)
    wg = jnp.pad(wg.reshape(d, G, 3 * R), ((0, 0), (0, 0), (0, GATE_ROWS - 3 * R))).reshape(d, G * GATE_ROWS)
    wf = jnp.concatenate([kv[:, 3], kv[:, 5], wg], axis=1).T.astype(BF16)

    n_slc = s // SLC_LEN
    pos = np.arange(s)
    blk, within = pos // SLC_LEN, pos % SLC_LEN
    feats = np.zeros((s, KFEAT), np.float32)
    feats[:, 0:3] = (blk * SLC_LEN)[:, None]
    feats[:, 3:6] = within[:, None]
    feats[:, 6:9] = 1.0
    onehot = (blk[:, None] == np.arange(n_slc)[None, :]).astype(np.float32)
    kfs = jnp.asarray(np.concatenate([onehot, feats], axis=1), F32)
    kfw = jnp.asarray(np.concatenate([np.zeros_like(onehot), feats], axis=1), F32)

    qT, kc_tok, vc_tok, ks, kw, vsT, vwT, gT = _c_pre(x, _row(g_pre), wq, wtok, wf, kfs, kfw)

    nch = s // CMP_STRIDE
    half = CMP_LEN // 2
    eye = jnp.eye(G, dtype=F32)

    def w1_halves(w):
        bd = jnp.einsum('hlde,gk->hlgdke', w.reshape(2, half, Dh, Dh), eye)
        bd = bd.reshape(2, half * KV_W, KV_W).astype(BF16)
        return bd[0], bd[1]

    def pe_halves(pe):
        return jnp.tile(pe.reshape(2, half, 1, Dh), (1, 1, G, 1)).reshape(2, half * KV_W).astype(F32)

    def w2_bd(w):
        return jnp.einsum('de,gk->gdke', w, eye).reshape(KV_W, KV_W)

    n_idx = np.arange(nch)
    cfeat = np.zeros((nch, KFEAT), np.float32)
    cfeat[:, 0:3] = (1024 * (n_idx // 64))[:, None]
    cfeat[:, 3:6] = (CMP_STRIDE * (n_idx % 64))[:, None]
    cfeat[:, 6:9] = 1.0
    kc, vcT = _compress(kc_tok.reshape(b, nch, CMP_STRIDE * KV_W), vc_tok.reshape(b, nch, CMP_STRIDE * KV_W),
                        pe_halves(pe_k), pe_halves(pe_v), *w1_halves(w1_k), *w1_halves(w1_v),
                        w2_bd(w2_k).astype(BF16), w2_bd(w2_v).T.astype(BF16), jnp.asarray(cfeat, F32))

    slopes_np = (2.0 ** (-8.0 * np.arange(1, NSA_HEADS + 1) / NSA_HEADS)).astype(np.float32)
    full = (slopes_np * np.float32(LOG2E)).astype(np.float32)
    pieces, rest = [], full.copy()
    for _ in range(3):
        piece = rest.astype(BF16).astype(np.float32)
        pieces.append(piece)
        rest = (rest - piece).astype(np.float32)
    tab = jnp.asarray(np.stack(pieces + [full], axis=1).reshape(-1), F32)

    ocT, bias, act = _cmp_topk(tab, qT, kc, vcT, gT, min(N_SEL, n_slc))
    act = act[:, :, :, 0, :s // ATTN_TILE].reshape(-1)
    yT = _slc_win(act, tab, qT, ks, vsT, kw, vwT, bias, ocT, gT)
    return _post(x, None, yT, None, w_out.astype(BF16), _row(g_post), _row(g_mlp_pre),
                 w1.astype(BF16), w2.astype(BF16), _row(g_mlp_post))


def kernel(x, norm_mix_pre, norm_mix_post, norm_mlp_pre, norm_mlp_post, mlp_w1, mlp_w2, a_w_in, a_conv_w, a_q_norm,
           a_w_q_up, a_kv_norm, a_w_kv_up, a_w_out, c_w_in, c_cmp_pe_k, c_cmp_w1_k, c_cmp_w2_k, c_cmp_pe_v,
           c_cmp_w1_v, c_cmp_w2_v, c_w_out):
    depth = norm_mix_pre.shape[0]
    for layer in range(depth):
        i = layer // 2
        common = (norm_mix_post[layer], norm_mlp_pre[layer], mlp_w1[layer], mlp_w2[layer], norm_mlp_post[layer])
        if layer % 2 == 0:
            x = _layer_a(x, norm_mix_pre[layer], a_w_in[i], a_conv_w[i], a_q_norm[i], a_w_q_up[i], a_kv_norm[i],
                         a_w_kv_up[i], a_w_out[i], *common)
        else:
            x = _layer_c(x, norm_mix_pre[layer], c_w_in[i], c_cmp_pe_k[i], c_cmp_w1_k[i], c_cmp_w2_k[i],
                         c_cmp_pe_v[i], c_cmp_w1_v[i], c_cmp_w2_v[i], c_w_out[i], *common)
    return x
```

```python
import functools
import math

import numpy as np
import jax
import jax.numpy as jnp
from jax import lax
from jax.experimental import pallas as pl
from jax.experimental.pallas import tpu as pltpu

F32 = jnp.float32
BF16 = jnp.bfloat16

D_MODEL = 1024
D_FF = 4 * D_MODEL
EPS = 1e-6
NEG = -1e30
LOG2E = math.log2(math.e)

CONV_WIDTH = 512
CONV_K = 3
MLA_HEADS = 8
MLA_NOPE = 64
MLA_ROPE = 32
MLA_V = 64
MLA_KV_RANK = 256
MLA_Q_RANK = 768
MLA_SLOT = 128
ROPE_THETA = 10000.0
ROPE_HALF = MLA_ROPE // 2

NSA_HEADS = 16
NSA_GROUPS = 4
NSA_REP = 4
NSA_DH = 64
CMP_LEN = 32
CMP_STRIDE = 16
SLC_LEN = 64
SLC_SHIFT = 6
N_SEL = 16
WINDOW = 512
FORCE_BONUS = 1e4
KV_W = NSA_GROUPS * NSA_DH
C_MIX = NSA_HEADS * NSA_DH
GATE_ROWS = 16
ONES_ROWS = 16
V_ROWS = 64 + ONES_ROWS
LANES = 128
SUBLANES = 8
BF16_ROWS = 16
KFEAT = 16

TOKEN_TILE = 512
POST_TILE = 512
ATTN_TILE = 256
MLA_HEADS_PER_STEP = 8
NSA_GROUPS_PER_STEP = 2
CMP_GROUPS_PER_STEP = 4
FF_CHUNK = 1024
VMEM_LIMIT = 56 * 1024 * 1024


def _cparams(sem):
    return pltpu.CompilerParams(dimension_semantics=sem, vmem_limit_bytes=VMEM_LIMIT)


def _const_spec(shape, single_buffer=False):
    nd = len(shape)
    if single_buffer:
        return pl.BlockSpec(shape, lambda *_: (0,) * nd, pipeline_mode=pl.Buffered(1))
    return pl.BlockSpec(shape, lambda *_: (0,) * nd)


def _rms(x, g):
    ms = jnp.mean(x * x, axis=-1, keepdims=True)
    return x * lax.rsqrt(ms + EPS) * g


def _nt(a, b):
    return lax.dot_general(a, b, (((1,), (1,)), ((), ())), preferred_element_type=F32)


def _tn(a, b):
    return lax.dot_general(a, b, (((0,), (0,)), ((), ())), preferred_element_type=F32)


def _ones_rows(n):
    return jnp.where(lax.broadcasted_iota(jnp.int32, (ONES_ROWS, n), 0) == 0, 1.0, 0.0).astype(F32)


def _rope_rows(r1, r2, cos, sin):
    return r1 * cos - r2 * sin, r2 * cos + r1 * sin


def _a_pre_kernel(x_ref, g_ref, w1_ref, wkr_ref, convw_ref, qn_ref, wq_ref, kvn_ref, wk_ref, wv_ref,
                  cos_ref, sin_ref, yconv_ref, qT_ref, k_ref, vT_ref, carry_ref, *, tm, q_scale):
    @pl.when(pl.program_id(1) == 0)
    def _():
        carry_ref[...] = jnp.zeros_like(carry_ref)

    sub = min(ATTN_TILE, tm)
    w = convw_ref[...]
    w0, w1, w2 = w[0:1], w[1:2], w[2:3]
    hd = BF16_ROWS
    tail = carry_ref[...]
    z32 = jnp.zeros((MLA_SLOT - MLA_NOPE - MLA_ROPE, sub), F32)
    ones = _ones_rows(sub)
    o1, o2, o3 = CONV_WIDTH, 2 * CONV_WIDTH, 3 * CONV_WIDTH
    o4 = o3 + MLA_Q_RANK
    for cidx in range(tm // sub):
        rows = slice(cidx * sub, (cidx + 1) * sub)
        xn = _rms(x_ref[0, rows, :], g_ref[...]).astype(BF16)
        p1 = jnp.dot(xn, w1_ref[...], preferred_element_type=F32)
        g_b, g_c, hv = p1[:, :o1], p1[:, o1:o2], p1[:, o2:o3]
        c_q, c_kv = p1[:, o3:o4], p1[:, o4:]

        u = g_c * hv
        y = w2 * u + w1 * pltpu.roll(u, 1, 0) + w0 * pltpu.roll(u, 2, 0)
        uh = u[0:hd]
        row = lax.broadcasted_iota(jnp.int32, (hd, CONV_WIDTH), 0)
        h1 = jnp.where(row < 1, pltpu.roll(tail, 1, 0), pltpu.roll(uh, 1, 0))
        h2 = jnp.where(row < 2, pltpu.roll(tail, 2, 0), pltpu.roll(uh, 2, 0))
        yh = w2 * uh + w1 * h1 + w0 * h2
        yconv_ref[0, cidx * sub + hd:(cidx + 1) * sub, :] = (g_b[hd:] * y[hd:]).astype(BF16)
        yconv_ref[0, cidx * sub:cidx * sub + hd, :] = (g_b[0:hd] * yh).astype(BF16)
        tail = u[sub - hd:sub]

        cos = cos_ref[:, rows]
        sin = sin_ref[:, rows]

        hq = _rms(c_q, qn_ref[...]).astype(BF16)
        qT = _nt(wq_ref[...], hq)
        for h in range(MLA_HEADS):
            b0 = MLA_SLOT * h
            r1 = qT[b0 + MLA_NOPE:b0 + MLA_NOPE + ROPE_HALF]
            r2 = qT[b0 + MLA_NOPE + ROPE_HALF:b0 + MLA_NOPE + MLA_ROPE]
            e1, e2 = _rope_rows(r1, r2, cos, sin)
            slot = jnp.concatenate([qT[b0:b0 + MLA_NOPE], e1, e2, z32], axis=0) * q_scale
            qT_ref[0, b0:b0 + MLA_SLOT, rows] = slot.astype(BF16)

        krT = _nt(wkr_ref[...], xn)
        r1 = krT[MLA_NOPE:MLA_NOPE + ROPE_HALF]
        r2 = krT[MLA_NOPE + ROPE_HALF:MLA_NOPE + MLA_ROPE]
        e1, e2 = _rope_rows(r1, r2, cos, sin)
        slotT = jnp.concatenate([jnp.zeros((MLA_NOPE, sub), F32), e1, e2, z32], axis=0)
        slot = slotT.T

        hkv = _rms(c_kv, kvn_ref[...]).astype(BF16)
        kn = jnp.dot(hkv, wk_ref[...], preferred_element_type=F32)
        k = kn + jnp.concatenate([slot] * MLA_HEADS, axis=1)
        k_ref[0, rows, :] = k.astype(BF16)
        vT = _nt(wv_ref[...], hkv)
        vT = jnp.concatenate([x for h in range(MLA_HEADS) for x in (vT[h * MLA_V:(h + 1) * MLA_V], ones)], axis=0)
        vT_ref[0, cidx] = vT.astype(BF16)
    carry_ref[...] = tail


def _a_pre(x, g, w1, wkr, convw, qn, wq, kvn, wk, wv, cos, sin):
    b, s, d = x.shape
    tm = min(TOKEN_TILE, s)
    hw = MLA_HEADS * MLA_SLOT
    vw = MLA_HEADS * V_ROWS
    q_scale = (MLA_NOPE + MLA_ROPE) ** -0.5 * LOG2E
    kern = functools.partial(_a_pre_kernel, tm=tm, q_scale=q_scale)
    return pl.pallas_call(
        kern,
        grid=(b, s // tm),
        in_specs=[
            pl.BlockSpec((1, tm, d), lambda i, j: (i, j, 0)),
            _const_spec(g.shape), _const_spec(w1.shape), _const_spec(wkr.shape), _const_spec(convw.shape),
            _const_spec(qn.shape), _const_spec(wq.shape), _const_spec(kvn.shape), _const_spec(wk.shape),
            _const_spec(wv.shape),
            pl.BlockSpec((ROPE_HALF, tm), lambda i, j: (0, j)),
            pl.BlockSpec((ROPE_HALF, tm), lambda i, j: (0, j)),
        ],
        out_specs=[
            pl.BlockSpec((1, tm, CONV_WIDTH), lambda i, j: (i, j, 0)),
            pl.BlockSpec((1, hw, tm), lambda i, j: (i, 0, j)),
            pl.BlockSpec((1, tm, hw), lambda i, j: (i, j, 0)),
            pl.BlockSpec((1, tm // ATTN_TILE, vw, ATTN_TILE), lambda i, j: (i, j, 0, 0)),
        ],
        out_shape=[
            jax.ShapeDtypeStruct((b, s, CONV_WIDTH), BF16),
            jax.ShapeDtypeStruct((b, hw, s), BF16),
            jax.ShapeDtypeStruct((b, s, hw), BF16),
            jax.ShapeDtypeStruct((b, s // ATTN_TILE, vw, ATTN_TILE), BF16),
        ],
        scratch_shapes=[pltpu.VMEM((16, CONV_WIDTH), F32)],
        compiler_params=_cparams(("arbitrary", "arbitrary")),
        name="a_pre",
    )(x, g, w1, wkr, convw, qn, wq, kvn, wk, wv, cos, sin)


def _tile_pos(t, k0, q0):
    kpos = k0 + lax.broadcasted_iota(jnp.int32, (t, t), 0)
    qpos = q0 + lax.broadcasted_iota(jnp.int32, (t, t), 1)
    return kpos, qpos


def _stage_scores(s, s_slot):
    s_slot[...] = s
    return jnp.max(s, axis=0, keepdims=True)


def _stage_softmax(s_slot, m, mt):
    m_new = jnp.maximum(m, mt)
    return jnp.exp2(s_slot[...] - m_new).astype(BF16), m_new, jnp.exp2(m - m_new)


def _stage_pv(acc_slot, vT, p, alpha):
    acc_slot[...] = alpha * acc_slot[...] + jnp.dot(vT, p, preferred_element_type=F32)


def _causal_flash(i, n_rest, tile_at, t, chains, scores, vtile, diag_extra, s_ref, acc_ref):
    mt0 = []
    for c in chains:
        mt0.append(_stage_scores(scores(c, i) + diag_extra, s_ref.at[c]))
        acc_ref[c] = jnp.zeros((V_ROWS, t), F32)
    init = (tuple(mt0), tuple(jnp.full((1, t), NEG, F32) for _ in chains))
    last = jnp.maximum(n_rest - 1, 0)

    def body(n, carry):
        mt, m = carry
        j_cur = jnp.where(n == 0, i, tile_at(jnp.maximum(n - 1, 0)))
        j_next = tile_at(jnp.minimum(n, last))
        m_new, mt_next = [], []
        for c in chains:
            p, mn, alpha = _stage_softmax(s_ref.at[c], m[c], mt[c])
            m_new.append(mn)
            mt_next.append(_stage_scores(scores(c, j_next), s_ref.at[c]))
            _stage_pv(acc_ref.at[c], vtile(c, j_cur), p, alpha)
        return tuple(mt_next), tuple(m_new)

    n_iter = n_rest + 1
    quads = lax.shift_right_logical(n_iter, 2)

    def quad(k, cr):
        for u in range(4):
            cr = body(4 * k + u, cr)
        return cr

    carry = lax.fori_loop(0, quads, quad, init)
    lax.fori_loop(4 * quads, n_iter, body, carry)


def _normalized(acc_slot):
    acc = acc_slot[...]
    return acc[:V_ROWS - ONES_ROWS] / acc[V_ROWS - ONES_ROWS:V_ROWS - ONES_ROWS + 1]


def _mla_attn_kernel(qT_ref, k_ref, vT_ref, oT_ref, s_ref, acc_ref, *, t, hb):
    i = pl.program_id(2)

    def scores(h, j):
        k = k_ref[0, pl.ds(pl.multiple_of(j * t, t), t), h * MLA_SLOT:(h + 1) * MLA_SLOT]
        return jnp.dot(k, qT_ref[0, h * MLA_SLOT:(h + 1) * MLA_SLOT, :], preferred_element_type=F32)

    def vtile(h, j):
        return vT_ref[0, j, h * V_ROWS:(h + 1) * V_ROWS, :]

    kpos, qpos = _tile_pos(t, 0, 0)
    causal = jnp.where(kpos <= qpos, 0.0, NEG)
    _causal_flash(i, i, lambda n: n, t, range(hb), scores, vtile, causal, s_ref, acc_ref)
    for h in range(hb):
        oT_ref[0, h * MLA_V:(h + 1) * MLA_V, :] = _normalized(acc_ref.at[h]).astype(BF16)


def _mla_attn(qT, k, vT):
    b, hw, s = qT.shape
    t = min(ATTN_TILE, s)
    nkv = s // t
    hb = MLA_HEADS_PER_STEP
    kern = functools.partial(_mla_attn_kernel, t=t, hb=hb)
    return pl.pallas_call(
        kern,
        grid=(b, MLA_HEADS // hb, s // t),
        in_specs=[
            pl.BlockSpec((1, hb * MLA_SLOT, t), lambda bi, h, i: (bi, h, i)),
            pl.BlockSpec((1, s, hb * MLA_SLOT), lambda bi, h, i: (bi, 0, h)),
            pl.BlockSpec((1, nkv, hb * V_ROWS, t), lambda bi, h, i: (bi, 0, h, 0)),
        ],
        out_specs=pl.BlockSpec((1, hb * MLA_V, t), lambda bi, h, i: (bi, h, i)),
        out_shape=jax.ShapeDtypeStruct((b, MLA_HEADS * MLA_V, s), BF16),
        scratch_shapes=[pltpu.VMEM((hb, t, t), F32), pltpu.VMEM((hb, V_ROWS, t), F32)],
        compiler_params=_cparams(("arbitrary", "arbitrary", "arbitrary")),
        name="mla_attn",
    )(qT, k, vT)


def _post_kernel(*refs, has_tok):
    if has_tok:
        x_ref, ytok_ref, yT_ref, wo_tok_ref, wo_ref, gpost_ref, gpre_ref, w1_ref, w2_ref, gmlp_ref, o_ref = refs
    else:
        x_ref, yT_ref, wo_ref, gpost_ref, gpre_ref, w1_ref, w2_ref, gmlp_ref, o_ref = refs
    x = x_ref[0]
    mix = _tn(yT_ref[0], wo_ref[...])
    if has_tok:
        mix = mix + jnp.dot(ytok_ref[0], wo_tok_ref[...], preferred_element_type=F32)
    x1 = x + _rms(mix, gpost_ref[...])
    h = _rms(x1, gpre_ref[...]).astype(BF16)
    acc = jnp.zeros_like(x1)
    for c in range(D_FF // FF_CHUNK):
        a = jnp.dot(h, w1_ref[:, c * FF_CHUNK:(c + 1) * FF_CHUNK], preferred_element_type=F32)
        a = jnp.maximum(a, 0.0)
        a = (a * a).astype(BF16)
        acc = acc + jnp.dot(a, w2_ref[c * FF_CHUNK:(c + 1) * FF_CHUNK, :], preferred_element_type=F32)
    o_ref[0] = x1 + _rms(acc, gmlp_ref[...])


def _post(x, ytok, yT, wo_tok, wo, gpost, gpre, w1, w2, gmlp):
    b, s, d = x.shape
    tm = min(POST_TILE, s)
    has_tok = ytok is not None
    kern = functools.partial(_post_kernel, has_tok=has_tok)
    xspec = pl.BlockSpec((1, tm, d), lambda i, j: (i, j, 0))
    ins, specs = [x], [xspec]
    if has_tok:
        ins.append(ytok)
        specs.append(pl.BlockSpec((1, tm, ytok.shape[-1]), lambda i, j: (i, j, 0)))
    ins.append(yT)
    specs.append(pl.BlockSpec((1, yT.shape[1], tm), lambda i, j: (i, 0, j)))
    if has_tok:
        ins.append(wo_tok)
        specs.append(_const_spec(wo_tok.shape, single_buffer=True))
    for a in (wo, gpost, gpre, w1, w2, gmlp):
        ins.append(a)
        specs.append(_const_spec(a.shape, single_buffer=True))
    return pl.pallas_call(
        kern,
        grid=(b, s // tm),
        in_specs=specs,
        out_specs=xspec,
        out_shape=jax.ShapeDtypeStruct(x.shape, x.dtype),
        compiler_params=_cparams(("arbitrary", "arbitrary")),
        name="post_tok" if has_tok else "post",
    )(*ins)


def _c_pre_kernel(x_ref, g_ref, wq_ref, wtok_ref, wf_ref, kfs_ref, kfw_ref, qT_ref, kc_ref, vc_ref, ks_ref, kw_ref,
                  vsT_ref, vwT_ref, gT_ref, *, tm, q_scale):
    xn = _rms(x_ref[0], g_ref[...]).astype(BF16)
    qT_ref[0] = (_nt(wq_ref[...], xn) * q_scale).astype(BF16)
    tok = jnp.dot(xn, wtok_ref[...], preferred_element_type=F32)
    kc_ref[0] = tok[:, :KV_W].astype(BF16)
    vc_ref[0] = tok[:, KV_W:2 * KV_W].astype(BF16)
    kfs = kfs_ref[...]
    kfw = kfw_ref[...]
    for g in range(NSA_GROUPS):
        a = 2 * KV_W + g * NSA_DH
        ks_ref[0, g] = jnp.concatenate([tok[:, a:a + NSA_DH], kfs], axis=1).astype(BF16)
        kw_ref[0, g] = jnp.concatenate([tok[:, a + KV_W:a + KV_W + NSA_DH], kfw], axis=1).astype(BF16)
    fT = _nt(wf_ref[...], xn)
    ones = _ones_rows(tm)

    def with_ones(v):
        return jnp.concatenate([x for g in range(NSA_GROUPS) for x in (v[g * NSA_DH:(g + 1) * NSA_DH], ones)],
                               axis=0)

    vs = with_ones(fT[:KV_W])
    vw = with_ones(fT[KV_W:2 * KV_W])
    for cidx in range(tm // ATTN_TILE):
        sl = slice(cidx * ATTN_TILE, (cidx + 1) * ATTN_TILE)
        vsT_ref[0, cidx] = vs[:, sl].astype(BF16)
        vwT_ref[0, cidx] = vw[:, sl].astype(BF16)
    gT_ref[0] = 1.0 / (1.0 + jnp.exp(-fT[2 * KV_W:]))


def _c_pre(x, g, wq, wtok, wf, kfs, kfw):
    b, s, d = x.shape
    tm = min(TOKEN_TILE, s)
    q_scale = NSA_DH ** -0.5 * LOG2E
    kern = functools.partial(_c_pre_kernel, tm=tm, q_scale=q_scale)
    nt = tm // ATTN_TILE
    grows = NSA_GROUPS * GATE_ROWS
    kaug = NSA_DH + kfs.shape[1]
    vrows = NSA_GROUPS * V_ROWS
    kf_spec = pl.BlockSpec((tm, kfs.shape[1]), lambda i, j: (j, 0))
    return pl.pallas_call(
        kern,
        grid=(b, s // tm),
        in_specs=[pl.BlockSpec((1, tm, d), lambda i, j: (i, j, 0)),
                  _const_spec(g.shape), _const_spec(wq.shape), _const_spec(wtok.shape), _const_spec(wf.shape),
                  kf_spec, kf_spec],
        out_specs=[
            pl.BlockSpec((1, C_MIX, tm), lambda i, j: (i, 0, j)),
            pl.BlockSpec((1, tm, KV_W), lambda i, j: (i, j, 0)),
            pl.BlockSpec((1, tm, KV_W), lambda i, j: (i, j, 0)),
            pl.BlockSpec((1, NSA_GROUPS, tm, kaug), lambda i, j: (i, 0, j, 0)),
            pl.BlockSpec((1, NSA_GROUPS, tm, kaug), lambda i, j: (i, 0, j, 0)),
            pl.BlockSpec((1, nt, vrows, ATTN_TILE), lambda i, j: (i, j, 0, 0)),
            pl.BlockSpec((1, nt, vrows, ATTN_TILE), lambda i, j: (i, j, 0, 0)),
            pl.BlockSpec((1, grows, tm), lambda i, j: (i, 0, j)),
        ],
        out_shape=[
            jax.ShapeDtypeStruct((b, C_MIX, s), BF16),
            jax.ShapeDtypeStruct((b, s, KV_W), BF16),
            jax.ShapeDtypeStruct((b, s, KV_W), BF16),
            jax.ShapeDtypeStruct((b, NSA_GROUPS, s, kaug), BF16),
            jax.ShapeDtypeStruct((b, NSA_GROUPS, s, kaug), BF16),
            jax.ShapeDtypeStruct((b, s // ATTN_TILE, vrows, ATTN_TILE), BF16),
            jax.ShapeDtypeStruct((b, s // ATTN_TILE, vrows, ATTN_TILE), BF16),
            jax.ShapeDtypeStruct((b, grows, s), F32),
        ],
        compiler_params=_cparams(("arbitrary", "arbitrary")),
        name="c_pre",
    )(x, g, wq, wtok, wf, kfs, kfw)


def _gelu_tanh(x):
    return 0.5 * x * (1.0 + jnp.tanh(math.sqrt(2.0 / math.pi) * (x + 0.044715 * (x * x * x))))


def _compress_kernel(zk_ref, zv_ref, pek_ref, pev_ref, wak_ref, wbk_ref, wav_ref, wbv_ref, w2k_ref, w2vT_ref, kf_ref,
                     kc_ref, vcT_ref, *, nch):
    def hidden(z_ref, pe_ref, wa_ref, wb_ref):
        z = z_ref[0].astype(F32)
        za = (z + pe_ref[0:1]).astype(BF16)
        zb = (z + pe_ref[1:2]).astype(BF16)
        a = jnp.dot(za, wa_ref[...], preferred_element_type=F32)
        bm = jnp.dot(zb, wb_ref[...], preferred_element_type=F32)
        return _gelu_tanh(a + pltpu.roll(bm, nch - 1, 0)).astype(BF16)

    hk = hidden(zk_ref, pek_ref, wak_ref, wbk_ref)
    kc = jnp.dot(hk, w2k_ref[...], preferred_element_type=F32)
    kf = kf_ref[...]
    hv = hidden(zv_ref, pev_ref, wav_ref, wbv_ref)
    vcT = _nt(w2vT_ref[...], hv)
    ones = _ones_rows(nch)
    for g in range(NSA_GROUPS):
        gs = slice(g * NSA_DH, (g + 1) * NSA_DH)
        kc_ref[0, g] = jnp.concatenate([kc[:, gs], kf], axis=1).astype(BF16)
        vcT_ref[0, g] = jnp.concatenate([vcT[gs], ones], axis=0).astype(BF16)


def _compress(zk, zv, pek, pev, wak, wbk, wav, wbv, w2k, w2vT, kf):
    b, nch, cw = zk.shape
    kern = functools.partial(_compress_kernel, nch=nch)
    zspec = pl.BlockSpec((1, nch, cw), lambda i: (i, 0, 0))
    consts = (pek, pev, wak, wbk, wav, wbv, w2k, w2vT, kf)
    return pl.pallas_call(
        kern,
        grid=(b,),
        in_specs=[zspec, zspec] + [_const_spec(a.shape) for a in consts],
        out_specs=[pl.BlockSpec((1, NSA_GROUPS, nch, NSA_DH + KFEAT), lambda i: (i, 0, 0, 0)),
                   pl.BlockSpec((1, NSA_GROUPS, V_ROWS, nch), lambda i: (i, 0, 0, 0))],
        out_shape=[jax.ShapeDtypeStruct((b, NSA_GROUPS, nch, NSA_DH + KFEAT), BF16),
                   jax.ShapeDtypeStruct((b, NSA_GROUPS, V_ROWS, nch), BF16)],
        compiler_params=_cparams(("arbitrary",)),
        name="compress",
    )(zk, zv, *consts)


def _split3(x):
    hi = x.astype(BF16)
    r = x - hi.astype(F32)
    mid = r.astype(BF16)
    lo = (r - mid.astype(F32)).astype(BF16)
    return hi, mid, lo


def _feature_rows(tab_ref, head, const, t):
    row = lax.broadcasted_iota(jnp.int32, (KFEAT, t), 0)
    r3 = row - 3 * (row >= 3).astype(jnp.int32) - 3 * (row >= 6).astype(jnp.int32)
    s_hi, s_mid, s_lo, slope = (tab_ref[head * 4 + k] for k in range(4))
    c_hi, c_mid, c_lo = (x.astype(F32) for x in _split3(slope * const))
    svals = jnp.where(r3 == 0, s_hi, jnp.where(r3 == 1, s_mid, s_lo))
    cvals = jnp.where(r3 == 0, c_hi, jnp.where(r3 == 1, c_mid, c_lo))
    return jnp.where(row < 6, svals, jnp.where(row < 9, cvals, 0.0)).astype(BF16)


def _cmp_topk_kernel(tab_ref, qT_ref, kc_ref, vcT_ref, pool_ref, gT_ref, ocT_ref, bias_ref, act_ref,
                     qa_ref, e_ref, acc_ref, imp_ref, rank_ref, ps_ref, *, t, gb, nch, nslc, nsel, ov_terms):
    gp = pl.program_id(1)
    i = pl.program_id(2)
    q0 = i * t
    chains = range(gb * NSA_REP)
    bpt = t // SLC_LEN

    const = (CMP_LEN - 1) / 2.0 - jnp.full((KFEAT, t), q0, jnp.int32).astype(F32)
    for c in chains:
        head = (gp * gb + c // NSA_REP) * NSA_REP + c % NSA_REP
        qa_ref[c, 0:NSA_DH, :] = qT_ref[0, c * NSA_DH:(c + 1) * NSA_DH, :]
        qa_ref[c, NSA_DH:NSA_DH + KFEAT, :] = _feature_rows(tab_ref, head, const, t)

    n_i = lax.broadcasted_iota(jnp.int32, (nch, t), 0)
    t_i = q0 + lax.broadcasted_iota(jnp.int32, (nch, t), 1)
    ended = jnp.where(n_i * CMP_STRIDE + (CMP_LEN - 1) <= t_i, 0.0, NEG)
    def cmp_scores(c):
        s = jnp.dot(kc_ref[0, c // NSA_REP], qa_ref[c], preferred_element_type=F32) + ended
        return _stage_scores(s, e_ref.at[c])

    def cmp_softmax_pv(c, mt):
        m_eff = jnp.where(mt > 0.5 * NEG, mt, 0.0)
        e = jnp.exp2(e_ref[c] - m_eff)
        e_ref[c] = e
        acc_ref[c] = jnp.dot(vcT_ref[0, c // NSA_REP], e.astype(BF16), preferred_element_type=F32)

    mt = [cmp_scores(c) for c in chains]
    for c in chains:
        cmp_softmax_pv(c, mt[c])

    j_i = lax.broadcasted_iota(jnp.int32, (nslc, t), 0)
    cur = jnp.right_shift(q0 + lax.broadcasted_iota(jnp.int32, (nslc, t), 1), SLC_SHIFT)
    forced = (j_i == 0) | (j_i == cur) | (j_i == cur - 1)
    per_blk = SLC_LEN // CMP_STRIDE
    for gl in range(gb):
        psum = jnp.zeros((nch, t), F32)
        for r in range(NSA_REP):
            c = gl * NSA_REP + r
            acc = acc_ref[c]
            l = acc[NSA_DH:NSA_DH + 1]
            inv = 1.0 / jnp.where(l > 0.0, l, 1.0)
            gate = gT_ref[0, gl * GATE_ROWS + r:gl * GATE_ROWS + r + 1, :]
            ocT_ref[0, c * NSA_DH:(c + 1) * NSA_DH, :] = (gate * (acc[:NSA_DH] * inv)).astype(BF16)
            psum = psum + e_ref[c] * inv
        for h in range(t // LANES):
            ps_ref[h] = psum[:, h * LANES:(h + 1) * LANES]
        imp = jnp.zeros((nslc, t), F32)
        for k, wgt in ov_terms:
            rows = jnp.concatenate([ps_ref[h, pl.ds(k % per_blk, nslc, stride=per_blk), :]
                                    for h in range(t // LANES)], axis=1)
            if k < 0:
                rows = jnp.where(j_i == 0, 0.0, pltpu.roll(rows, 1, 0))
            imp = imp + wgt * rows
        imp_ref[gl] = jnp.where(j_i > cur, NEG, imp + jnp.where(forced, FORCE_BONUS, 0.0))
        rank_ref[gl] = jnp.zeros((nslc, t), F32)

    for chunk in range(nslc // bpt):
        @pl.when(chunk <= i)
        def _():
            for gl in range(gb):
                imp = imp_ref[gl]
                rank = rank_ref[gl]
                for jp in range(chunk * bpt, (chunk + 1) * bpt):
                    row = imp[jp:jp + 1, :]
                    incs = []
                    for rg in range(nslc // SUBLANES):
                        r0 = SUBLANES * rg
                        blk = imp[r0:r0 + SUBLANES]
                        if r0 > jp:
                            incs.append(jnp.where(row >= blk, 1.0, 0.0))
                        elif r0 + SUBLANES - 1 <= jp:
                            incs.append(jnp.where(row > blk, 1.0, 0.0))
                        else:
                            above = lax.broadcasted_iota(jnp.int32, (SUBLANES, t), 0) > jp - r0
                            incs.append(jnp.where(above, jnp.where(row >= blk, 1.0, 0.0),
                                                  jnp.where(row > blk, 1.0, 0.0)))
                    rank = rank + jnp.concatenate(incs, axis=0)
                rank_ref[gl] = rank

    fg = NSA_GROUPS_PER_STEP
    for pair in range(gb // fg):
        best = jnp.full((nslc, t), NEG, F32)
        for gl in range(pair * fg, (pair + 1) * fg):
            bias = jnp.where(j_i > cur, NEG, jnp.where(rank_ref[gl] < nsel, 0.0, NEG))
            bias_ref[0, gl] = bias
            best = jnp.maximum(best, bias)
        hit = jnp.where(jnp.max(best, axis=1, keepdims=True) == 0.0, 1.0, 0.0)
        tiles = jnp.sum(hit * pool_ref[...], axis=0, keepdims=True)
        act_ref[0, pair, 0] = jnp.broadcast_to(jnp.where(tiles > 0.0, 1, 0), (SUBLANES, LANES)).astype(jnp.int32)


def _overlap_terms():
    per_blk = SLC_LEN // CMP_STRIDE
    terms = []
    for k in range(-(CMP_LEN // CMP_STRIDE) + 1, per_blk):
        lo, hi = max(k * CMP_STRIDE, 0), min(k * CMP_STRIDE + CMP_LEN, SLC_LEN)
        if hi > lo:
            terms.append((k, (hi - lo) / CMP_LEN))
    assert SLC_LEN % CMP_STRIDE == 0 and all(k >= -per_blk for k, _ in terms)
    return tuple(terms)


def _cmp_topk(tab, qT, kc, vcT, gT, nsel):
    b, _, s = qT.shape
    t = ATTN_TILE
    assert s % t == 0 and t % SLC_LEN == 0
    nch = kc.shape[2]
    nslc = s // SLC_LEN
    gb = CMP_GROUPS_PER_STEP
    fg = NSA_GROUPS_PER_STEP
    assert gb % fg == 0
    nc = gb * NSA_REP
    kern = functools.partial(_cmp_topk_kernel, t=t, gb=gb, nch=nch, nslc=nslc, nsel=nsel,
                             ov_terms=_overlap_terms())
    gw = nc * NSA_DH
    assert s // t <= LANES
    pool = np.zeros((nslc, LANES), np.float32)
    pool[np.arange(nslc), np.arange(nslc) // (t // SLC_LEN)] = 1.0
    pool = jnp.asarray(pool, F32)
    return pl.pallas_call(
        kern,
        grid=(b, NSA_GROUPS // gb, s // t),
        in_specs=[
            pl.BlockSpec(memory_space=pltpu.SMEM),
            pl.BlockSpec((1, gw, t), lambda bi, g, i: (bi, g, i)),
            pl.BlockSpec((1, gb, nch, NSA_DH + KFEAT), lambda bi, g, i: (bi, g, 0, 0)),
            pl.BlockSpec((1, gb, V_ROWS, nch), lambda bi, g, i: (bi, g, 0, 0)),
            _const_spec(pool.shape),
            pl.BlockSpec((1, gb * GATE_ROWS, t), lambda bi, g, i: (bi, g, i)),
        ],
        out_specs=[pl.BlockSpec((1, gw, t), lambda bi, g, i: (bi, g, i)),
                   pl.BlockSpec((1, gb, nslc, t), lambda bi, g, i: (bi, g, 0, i)),
                   pl.BlockSpec((1, gb // fg, 1, SUBLANES, LANES), lambda bi, g, i: (bi, g, i, 0, 0))],
        out_shape=[jax.ShapeDtypeStruct((b, C_MIX, s), BF16),
                   jax.ShapeDtypeStruct((b, NSA_GROUPS, nslc, s), F32),
                   jax.ShapeDtypeStruct((b, NSA_GROUPS // fg, s // t, SUBLANES, LANES), jnp.int32)],
        scratch_shapes=[pltpu.VMEM((nc, NSA_DH + KFEAT, t), BF16),
                        pltpu.VMEM((nc, nch, t), F32),
                        pltpu.VMEM((nc, V_ROWS, t), F32),
                        pltpu.VMEM((gb, nslc, t), F32), pltpu.VMEM((gb, nslc, t), F32),
                        pltpu.VMEM((t // LANES, nch, LANES), F32)],
        compiler_params=_cparams(("arbitrary", "arbitrary", "arbitrary")),
        name="cmp_topk",
    )(tab, qT, kc, vcT, pool, gT)


def _slc_win_kernel(act_ref, tab_ref, qT_ref, ks_ref, vsT_ref, kw_ref, vwT_ref, bias_ref, ocT_ref, gT_ref, yT_ref,
                    qa_ref, s_ref, acc_ref, sw_ref, accw_ref, idx_ref, *, t, gb, nslc, nkv):
    gp = pl.program_id(1)
    i = pl.program_id(2)
    chains = range(gb * NSA_REP)
    feat0 = NSA_DH + nslc

    const = -jnp.full((KFEAT, t), i * t, jnp.int32).astype(F32)
    for c in chains:
        gl, r = divmod(c, NSA_REP)
        head = (gp * gb + gl) * NSA_REP + r
        qa_ref[c, 0:NSA_DH, :] = qT_ref[0, c * NSA_DH:(c + 1) * NSA_DH, :]
        qa_ref[c, NSA_DH:feat0, :] = bias_ref[0, gl].astype(BF16)
        qa_ref[c, feat0:feat0 + KFEAT, :] = _feature_rows(tab_ref, head, const, t)

    def scores_from(k_ref):
        def scores(c, j):
            k = k_ref[0, c // NSA_REP, pl.ds(pl.multiple_of(j * t, t), t), :]
            return jnp.dot(k, qa_ref[c], preferred_element_type=F32)
        return scores

    def vtile_from(vT_ref):
        def vtile(c, j):
            gl = c // NSA_REP
            return vT_ref[0, j, gl * V_ROWS:(gl + 1) * V_ROWS, :]
        return vtile

    kpos, qpos = _tile_pos(t, 0, 0)
    causal = jnp.where(kpos <= qpos, 0.0, NEG)

    w_scores = scores_from(kw_ref)
    w_vtile = vtile_from(vwT_ref)
    w_tiles = (i, jnp.maximum(i - 1, 0), jnp.maximum(i - 2, 0))
    in_seq1 = jnp.where((i - 1) * t + kpos >= 0, 0.0, NEG)
    in_seq2 = jnp.where((i - 2) * t + kpos >= 0, 0.0, NEG)
    w_extra = (causal, in_seq1, jnp.where(qpos < kpos, in_seq2, NEG))
    m = [jnp.full((1, t), NEG, F32) for _ in chains]
    mt = []
    for c in chains:
        accw_ref[c] = jnp.zeros((V_ROWS, t), F32)
        mt.append(_stage_scores(w_scores(c, w_tiles[0]) + w_extra[0], sw_ref.at[c]))
    for w in range(3):
        for c in chains:
            p, m[c], alpha = _stage_softmax(sw_ref.at[c], m[c], mt[c])
            if w + 1 < 3:
                mt[c] = _stage_scores(w_scores(c, w_tiles[w + 1]) + w_extra[w + 1], sw_ref.at[c])
            _stage_pv(accw_ref.at[c], w_vtile(c, w_tiles[w]), p, alpha)

    base = ((pl.program_id(0) * pl.num_programs(1) + gp) * pl.num_programs(2) + i) * nkv
    n_act = jnp.int32(0)
    for j in range(nkv):
        idx_ref[n_act] = j
        n_act = n_act + jnp.where((j < i) & (act_ref[base + j] != 0), 1, 0)
    _causal_flash(i, n_act, lambda n: idx_ref[n], t, chains, scores_from(ks_ref), vtile_from(vsT_ref), causal,
                  s_ref, acc_ref)

    for c in chains:
        gl, r = divmod(c, NSA_REP)
        g0 = gl * GATE_ROWS
        rs = slice(c * NSA_DH, (c + 1) * NSA_DH)
        y = (ocT_ref[0, rs, :].astype(F32)
             + gT_ref[0, g0 + NSA_REP + r:g0 + NSA_REP + r + 1, :] * _normalized(acc_ref.at[c])
             + gT_ref[0, g0 + 2 * NSA_REP + r:g0 + 2 * NSA_REP + r + 1, :] * _normalized(accw_ref.at[c]))
        yT_ref[0, rs, :] = y.astype(BF16)


def _slc_win(act, tab, qT, ks, vsT, kw, vwT, bias, ocT, gT):
    b, _, s = qT.shape
    t = ATTN_TILE
    assert s % t == 0 and WINDOW == 2 * t
    nkv = s // t
    nslc = bias.shape[2]
    kaug = ks.shape[3]
    gb = NSA_GROUPS_PER_STEP
    nc = gb * NSA_REP
    gw = nc * NSA_DH
    kern = functools.partial(_slc_win_kernel, t=t, gb=gb, nslc=nslc, nkv=nkv)
    qspec = pl.BlockSpec((1, gw, t), lambda bi, g, i, *_: (bi, g, i))
    kspec = pl.BlockSpec((1, gb, s, kaug), lambda bi, g, i, *_: (bi, g, 0, 0))
    vspec = pl.BlockSpec((1, nkv, gb * V_ROWS, t), lambda bi, g, i, *_: (bi, 0, g, 0))
    grid_spec = pltpu.PrefetchScalarGridSpec(
        num_scalar_prefetch=2,
        grid=(b, NSA_GROUPS // gb, s // t),
        in_specs=[
            qspec, kspec, vspec, kspec, vspec,
            pl.BlockSpec((1, gb, nslc, t), lambda bi, g, i, *_: (bi, g, 0, i)),
            qspec,
            pl.BlockSpec((1, gb * GATE_ROWS, t), lambda bi, g, i, *_: (bi, g, i)),
        ],
        out_specs=qspec,
        scratch_shapes=[pltpu.VMEM((nc, kaug, t), BF16),
                        pltpu.VMEM((nc, t, t), F32), pltpu.VMEM((nc, V_ROWS, t), F32),
                        pltpu.VMEM((nc, t, t), F32), pltpu.VMEM((nc, V_ROWS, t), F32),
                        pltpu.SMEM((nkv + 1,), jnp.int32)],
    )
    return pl.pallas_call(
        kern,
        grid_spec=grid_spec,
        out_shape=jax.ShapeDtypeStruct((b, C_MIX, s), BF16),
        compiler_params=_cparams(("arbitrary", "arbitrary", "arbitrary")),
        name="slc_win",
    )(act, tab, qT, ks, vsT, kw, vwT, bias, ocT, gT)


def _row(v):
    return v.reshape(1, -1).astype(F32)


def _layer_a(x, g_pre, w_in, conv_w, q_norm, w_q_up, kv_norm, w_kv_up, w_out, g_post, g_mlp_pre, w1, w2, g_mlp_post):
    b, s, d = x.shape
    o3 = 3 * CONV_WIDTH
    o5 = o3 + MLA_Q_RANK + MLA_KV_RANK
    pad = MLA_SLOT - MLA_NOPE - MLA_ROPE
    w_main = w_in[:, :o5].astype(BF16)
    wkr = jnp.pad(w_in[:, o5:].T, ((MLA_NOPE, pad), (0, 0))).astype(BF16)
    wq = w_q_up.reshape(MLA_Q_RANK, MLA_HEADS, MLA_NOPE + MLA_ROPE)
    wq = jnp.pad(wq, ((0, 0), (0, 0), (0, pad))).reshape(MLA_Q_RANK, MLA_HEADS * MLA_SLOT).T.astype(BF16)
    wkv = w_kv_up.reshape(MLA_KV_RANK, MLA_HEADS, MLA_NOPE + MLA_V)
    wk = jnp.pad(wkv[:, :, :MLA_NOPE], ((0, 0), (0, 0), (0, MLA_SLOT - MLA_NOPE)))
    wk = wk.reshape(MLA_KV_RANK, MLA_HEADS * MLA_SLOT).astype(BF16)
    wv = wkv[:, :, MLA_NOPE:].reshape(MLA_KV_RANK, MLA_HEADS * MLA_V).T.astype(BF16)

    inv = ROPE_THETA ** (-jnp.arange(ROPE_HALF, dtype=F32) / ROPE_HALF)
    ang = inv[:, None] * jnp.arange(s, dtype=F32)[None, :]
    cos, sin = jnp.cos(ang), jnp.sin(ang)

    yconv, qT, k, vT = _a_pre(x, _row(g_pre), w_main, wkr, conv_w.astype(F32), _row(q_norm), wq, _row(kv_norm),
                              wk, wv, cos, sin)
    yT = _mla_attn(qT, k, vT)
    wo = w_out.astype(BF16)
    return _post(x, yconv, yT, wo[:CONV_WIDTH], wo[CONV_WIDTH:], _row(g_post), _row(g_mlp_pre),
                 w1.astype(BF16), w2.astype(BF16), _row(g_mlp_post))


def _layer_c(x, g_pre, w_in, pe_k, w1_k, w2_k, pe_v, w1_v, w2_v, w_out, g_post, g_mlp_pre, w1, w2, g_mlp_post):
    b, s, d = x.shape
    G, R, Dh = NSA_GROUPS, NSA_REP, NSA_DH
    wq = w_in[:, :C_MIX].T.astype(BF16)
    kv = w_in[:, C_MIX:C_MIX + 6 * KV_W].reshape(d, 6, KV_W)
    wtok = jnp.concatenate([kv[:, 0], kv[:, 1], kv[:, 2], kv[:, 4]], axis=1).astype(BF16)
    wg = w_in[:, C_MIX + 6 * KV_W:].reshape(d, 3, G, R).transpose(0, 2, 1, 3)
    wg = jnp.pad(wg.reshape(d, G, 3 * R), ((0, 0), (0, 0), (0, GATE_ROWS - 3 * R))).reshape(d, G * GATE_ROWS)
    wf = jnp.concatenate([kv[:, 3], kv[:, 5], wg], axis=1).T.astype(BF16)

    n_slc = s // SLC_LEN
    pos = np.arange(s)
    blk, within = pos // SLC_LEN, pos % SLC_LEN
    feats = np.zeros((s, KFEAT), np.float32)
    feats[:, 0:3] = (blk * SLC_LEN)[:, None]
    feats[:, 3:6] = within[:, None]
    feats[:, 6:9] = 1.0
    onehot = (blk[:, None] == np.arange(n_slc)[None, :]).astype(np.float32)
    kfs = jnp.asarray(np.concatenate([onehot, feats], axis=1), F32)
    kfw = jnp.asarray(np.concatenate([np.zeros_like(onehot), feats], axis=1), F32)

    qT, kc_tok, vc_tok, ks, kw, vsT, vwT, gT = _c_pre(x, _row(g_pre), wq, wtok, wf, kfs, kfw)

    nch = s // CMP_STRIDE
    half = CMP_LEN // 2
    eye = jnp.eye(G, dtype=F32)

    def w1_halves(w):
        bd = jnp.einsum('hlde,gk->hlgdke', w.reshape(2, half, Dh, Dh), eye)
        bd = bd.reshape(2, half * KV_W, KV_W).astype(BF16)
        return bd[0], bd[1]

    def pe_halves(pe):
        return jnp.tile(pe.reshape(2, half, 1, Dh), (1, 1, G, 1)).reshape(2, half * KV_W).astype(F32)

    def w2_bd(w):
        return jnp.einsum('de,gk->gdke', w, eye).reshape(KV_W, KV_W)

    n_idx = np.arange(nch)
    cfeat = np.zeros((nch, KFEAT), np.float32)
    cfeat[:, 0:3] = (1024 * (n_idx // 64))[:, None]
    cfeat[:, 3:6] = (CMP_STRIDE * (n_idx % 64))[:, None]
    cfeat[:, 6:9] = 1.0
    kc, vcT = _compress(kc_tok.reshape(b, nch, CMP_STRIDE * KV_W), vc_tok.reshape(b, nch, CMP_STRIDE * KV_W),
                        pe_halves(pe_k), pe_halves(pe_v), *w1_halves(w1_k), *w1_halves(w1_v),
                        w2_bd(w2_k).astype(BF16), w2_bd(w2_v).T.astype(BF16), jnp.asarray(cfeat, F32))

    slopes_np = (2.0 ** (-8.0 * np.arange(1, NSA_HEADS + 1) / NSA_HEADS)).astype(np.float32)
    full = (slopes_np * np.float32(LOG2E)).astype(np.float32)
    pieces, rest = [], full.copy()
    for _ in range(3):
        piece = rest.astype(BF16).astype(np.float32)
        pieces.append(piece)
        rest = (rest - piece).astype(np.float32)
    tab = jnp.asarray(np.stack(pieces + [full], axis=1).reshape(-1), F32)

    ocT, bias, act = _cmp_topk(tab, qT, kc, vcT, gT, min(N_SEL, n_slc))
    act = act[:, :, :, 0, :s // ATTN_TILE].reshape(-1)
    yT = _slc_win(act, tab, qT, ks, vsT, kw, vwT, bias, ocT, gT)
    return _post(x, None, yT, None, w_out.astype(BF16), _row(g_post), _row(g_mlp_pre),
                 w1.astype(BF16), w2.astype(BF16), _row(g_mlp_post))


def kernel(x, norm_mix_pre, norm_mix_post, norm_mlp_pre, norm_mlp_post, mlp_w1, mlp_w2, a_w_in, a_conv_w, a_q_norm,
           a_w_q_up, a_kv_norm, a_w_kv_up, a_w_out, c_w_in, c_cmp_pe_k, c_cmp_w1_k, c_cmp_w2_k, c_cmp_pe_v,
           c_cmp_w1_v, c_cmp_w2_v, c_w_out):
    depth = norm_mix_pre.shape[0]
    for layer in range(depth):
        i = layer // 2
        common = (norm_mix_post[layer], norm_mlp_pre[layer], mlp_w1[layer], mlp_w2[layer], norm_mlp_post[layer])
        if layer % 2 == 0:
            x = _layer_a(x, norm_mix_pre[layer], a_w_in[i], a_conv_w[i], a_q_norm[i], a_w_q_up[i], a_kv_norm[i],
                         a_w_kv_up[i], a_w_out[i], *common)
        else:
            x = _layer_c(x, norm_mix_pre[layer], c_w_in[i], c_cmp_pe_k[i], c_cmp_w1_k[i], c_cmp_w2_k[i],
                         c_cmp_pe_v[i], c_cmp_w1_v[i], c_cmp_w2_v[i], c_w_out[i], *common)
    return x
```

```python
import functools
import math

import numpy as np
import jax
import jax.numpy as jnp
from jax import lax
from jax.experimental import pallas as pl
from jax.experimental.pallas import tpu as pltpu

F32 = jnp.float32
BF16 = jnp.bfloat16

D_MODEL = 1024
D_FF = 4 * D_MODEL
EPS = 1e-6
NEG = -1e30
LOG2E = math.log2(math.e)

CONV_WIDTH = 512
CONV_K = 3
MLA_HEADS = 8
MLA_NOPE = 64
MLA_ROPE = 32
MLA_V = 64
MLA_KV_RANK = 256
MLA_Q_RANK = 768
MLA_SLOT = 128
ROPE_THETA = 10000.0
ROPE_HALF = MLA_ROPE // 2

NSA_HEADS = 16
NSA_GROUPS = 4
NSA_REP = 4
NSA_DH = 64
CMP_LEN = 32
CMP_STRIDE = 16
SLC_LEN = 64
SLC_SHIFT = 6
N_SEL = 16
WINDOW = 512
FORCE_BONUS = 1e4
KV_W = NSA_GROUPS * NSA_DH
C_MIX = NSA_HEADS * NSA_DH
GATE_ROWS = 16
ONES_ROWS = 16
V_ROWS = 64 + ONES_ROWS
LANES = 128
SUBLANES = 8
BF16_ROWS = 16
KFEAT = 16

TOKEN_TILE = 512
POST_TILE = 512
ATTN_TILE = 256
MLA_HEADS_PER_STEP = 8
NSA_GROUPS_PER_STEP = 2
CMP_GROUPS_PER_STEP = 4
FF_CHUNK = 1024
VMEM_LIMIT = 56 * 1024 * 1024


def _cparams(sem):
    return pltpu.CompilerParams(dimension_semantics=sem, vmem_limit_bytes=VMEM_LIMIT)


def _const_spec(shape, single_buffer=False):
    nd = len(shape)
    if single_buffer:
        return pl.BlockSpec(shape, lambda *_: (0,) * nd, pipeline_mode=pl.Buffered(1))
    return pl.BlockSpec(shape, lambda *_: (0,) * nd)


def _rms(x, g):
    ms = jnp.mean(x * x, axis=-1, keepdims=True)
    return x * lax.rsqrt(ms + EPS) * g


def _nt(a, b):
    return lax.dot_general(a, b, (((1,), (1,)), ((), ())), preferred_element_type=F32)


def _tn(a, b):
    return lax.dot_general(a, b, (((0,), (0,)), ((), ())), preferred_element_type=F32)


def _ones_rows(n):
    return jnp.where(lax.broadcasted_iota(jnp.int32, (ONES_ROWS, n), 0) == 0, 1.0, 0.0).astype(F32)


def _rope_rows(r1, r2, cos, sin):
    return r1 * cos - r2 * sin, r2 * cos + r1 * sin


def _a_pre_kernel(x_ref, g_ref, w1_ref, wkr_ref, convw_ref, qn_ref, wq_ref, kvn_ref, wk_ref, wv_ref,
                  cos_ref, sin_ref, yconv_ref, qT_ref, k_ref, vT_ref, carry_ref, *, tm, q_scale):
    @pl.when(pl.program_id(1) == 0)
    def _():
        carry_ref[...] = jnp.zeros_like(carry_ref)

    sub = min(ATTN_TILE, tm)
    w = convw_ref[...]
    w0, w1, w2 = w[0:1], w[1:2], w[2:3]
    hd = BF16_ROWS
    tail = carry_ref[...]
    z32 = jnp.zeros((MLA_SLOT - MLA_NOPE - MLA_ROPE, sub), F32)
    ones = _ones_rows(sub)
    o1, o2, o3 = CONV_WIDTH, 2 * CONV_WIDTH, 3 * CONV_WIDTH
    o4 = o3 + MLA_Q_RANK
    for cidx in range(tm // sub):
        rows = slice(cidx * sub, (cidx + 1) * sub)
        xn = _rms(x_ref[0, rows, :], g_ref[...]).astype(BF16)
        p1 = jnp.dot(xn, w1_ref[...], preferred_element_type=F32)
        g_b, g_c, hv = p1[:, :o1], p1[:, o1:o2], p1[:, o2:o3]
        c_q, c_kv = p1[:, o3:o4], p1[:, o4:]

        u = g_c * hv
        y = w2 * u + w1 * pltpu.roll(u, 1, 0) + w0 * pltpu.roll(u, 2, 0)
        uh = u[0:hd]
        row = lax.broadcasted_iota(jnp.int32, (hd, CONV_WIDTH), 0)
        h1 = jnp.where(row < 1, pltpu.roll(tail, 1, 0), pltpu.roll(uh, 1, 0))
        h2 = jnp.where(row < 2, pltpu.roll(tail, 2, 0), pltpu.roll(uh, 2, 0))
        yh = w2 * uh + w1 * h1 + w0 * h2
        yconv_ref[0, cidx * sub + hd:(cidx + 1) * sub, :] = (g_b[hd:] * y[hd:]).astype(BF16)
        yconv_ref[0, cidx * sub:cidx * sub + hd, :] = (g_b[0:hd] * yh).astype(BF16)
        tail = u[sub - hd:sub]

        cos = cos_ref[:, rows]
        sin = sin_ref[:, rows]

        hq = _rms(c_q, qn_ref[...]).astype(BF16)
        qT = _nt(wq_ref[...], hq)
        for h in range(MLA_HEADS):
            b0 = MLA_SLOT * h
            r1 = qT[b0 + MLA_NOPE:b0 + MLA_NOPE + ROPE_HALF]
            r2 = qT[b0 + MLA_NOPE + ROPE_HALF:b0 + MLA_NOPE + MLA_ROPE]
            e1, e2 = _rope_rows(r1, r2, cos, sin)
            slot = jnp.concatenate([qT[b0:b0 + MLA_NOPE], e1, e2, z32], axis=0) * q_scale
            qT_ref[0, b0:b0 + MLA_SLOT, rows] = slot.astype(BF16)

        krT = _nt(wkr_ref[...], xn)
        r1 = krT[MLA_NOPE:MLA_NOPE + ROPE_HALF]
        r2 = krT[MLA_NOPE + ROPE_HALF:MLA_NOPE + MLA_ROPE]
        e1, e2 = _rope_rows(r1, r2, cos, sin)
        slotT = jnp.concatenate([jnp.zeros((MLA_NOPE, sub), F32), e1, e2, z32], axis=0)
        slot = slotT.T

        hkv = _rms(c_kv, kvn_ref[...]).astype(BF16)
        kn = jnp.dot(hkv, wk_ref[...], preferred_element_type=F32)
        k = kn + jnp.concatenate([slot] * MLA_HEADS, axis=1)
        k_ref[0, rows, :] = k.astype(BF16)
        vT = _nt(wv_ref[...], hkv)
        vT = jnp.concatenate([x for h in range(MLA_HEADS) for x in (vT[h * MLA_V:(h + 1) * MLA_V], ones)], axis=0)
        vT_ref[0, cidx] = vT.astype(BF16)
    carry_ref[...] = tail


def _a_pre(x, g, w1, wkr, convw, qn, wq, kvn, wk, wv, cos, sin):
    b, s, d = x.shape
    tm = min(TOKEN_TILE, s)
    hw = MLA_HEADS * MLA_SLOT
    vw = MLA_HEADS * V_ROWS
    q_scale = (MLA_NOPE + MLA_ROPE) ** -0.5 * LOG2E
    kern = functools.partial(_a_pre_kernel, tm=tm, q_scale=q_scale)
    return pl.pallas_call(
        kern,
        grid=(b, s // tm),
        in_specs=[
            pl.BlockSpec((1, tm, d), lambda i, j: (i, j, 0)),
            _const_spec(g.shape), _const_spec(w1.shape), _const_spec(wkr.shape), _const_spec(convw.shape),
            _const_spec(qn.shape), _const_spec(wq.shape), _const_spec(kvn.shape), _const_spec(wk.shape),
            _const_spec(wv.shape),
            pl.BlockSpec((ROPE_HALF, tm), lambda i, j: (0, j)),
            pl.BlockSpec((ROPE_HALF, tm), lambda i, j: (0, j)),
        ],
        out_specs=[
            pl.BlockSpec((1, tm, CONV_WIDTH), lambda i, j: (i, j, 0)),
            pl.BlockSpec((1, hw, tm), lambda i, j: (i, 0, j)),
            pl.BlockSpec((1, tm, hw), lambda i, j: (i, j, 0)),
            pl.BlockSpec((1, tm // ATTN_TILE, vw, ATTN_TILE), lambda i, j: (i, j, 0, 0)),
        ],
        out_shape=[
            jax.ShapeDtypeStruct((b, s, CONV_WIDTH), BF16),
            jax.ShapeDtypeStruct((b, hw, s), BF16),
            jax.ShapeDtypeStruct((b, s, hw), BF16),
            jax.ShapeDtypeStruct((b, s // ATTN_TILE, vw, ATTN_TILE), BF16),
        ],
        scratch_shapes=[pltpu.VMEM((16, CONV_WIDTH), F32)],
        compiler_params=_cparams(("arbitrary", "arbitrary")),
        name="a_pre",
    )(x, g, w1, wkr, convw, qn, wq, kvn, wk, wv, cos, sin)


def _tile_pos(t, k0, q0):
    kpos = k0 + lax.broadcasted_iota(jnp.int32, (t, t), 0)
    qpos = q0 + lax.broadcasted_iota(jnp.int32, (t, t), 1)
    return kpos, qpos


def _stage_scores(s, s_slot):
    s_slot[...] = s
    return jnp.max(s, axis=0, keepdims=True)


def _stage_softmax(s_slot, m, mt):
    m_new = jnp.maximum(m, mt)
    return jnp.exp2(s_slot[...] - m_new).astype(BF16), m_new, jnp.exp2(m - m_new)


def _stage_pv(acc_slot, vT, p, alpha):
    acc_slot[...] = alpha * acc_slot[...] + jnp.dot(vT, p, preferred_element_type=F32)


def _causal_flash(i, n_rest, tile_fns, list_of, t, chains, scores, vtile, diag_extra, s_ref, acc_ref):
    mt0 = []
    for c in chains:
        mt0.append(_stage_scores(scores(c, i) + diag_extra, s_ref.at[c]))
        acc_ref[c] = jnp.zeros((V_ROWS, t), F32)
    init = (tuple(mt0), tuple(jnp.full((1, t), NEG, F32) for _ in chains))
    last = jnp.maximum(n_rest - 1, 0)

    def body(n, carry):
        mt, m = carry
        j_cur = [jnp.where(n == 0, i, f(jnp.maximum(n - 1, 0))) for f in tile_fns]
        j_next = [f(jnp.minimum(n, last)) for f in tile_fns]
        m_new, mt_next = [], []
        for c in chains:
            p, mn, alpha = _stage_softmax(s_ref.at[c], m[c], mt[c])
            m_new.append(mn)
            mt_next.append(_stage_scores(scores(c, j_next[list_of(c)]), s_ref.at[c]))
            _stage_pv(acc_ref.at[c], vtile(c, j_cur[list_of(c)]), p, alpha)
        return tuple(mt_next), tuple(m_new)

    n_iter = n_rest + 1
    quads = lax.shift_right_logical(n_iter, 2)

    def quad(k, cr):
        for u in range(4):
            cr = body(4 * k + u, cr)
        return cr

    carry = lax.fori_loop(0, quads, quad, init)
    lax.fori_loop(4 * quads, n_iter, body, carry)


def _normalized(acc_slot):
    acc = acc_slot[...]
    return acc[:V_ROWS - ONES_ROWS] / acc[V_ROWS - ONES_ROWS:V_ROWS - ONES_ROWS + 1]


def _mla_attn_kernel(qT_ref, k_ref, vT_ref, oT_ref, s_ref, acc_ref, *, t, hb):
    i = pl.program_id(2)

    def scores(h, j):
        k = k_ref[0, pl.ds(pl.multiple_of(j * t, t), t), h * MLA_SLOT:(h + 1) * MLA_SLOT]
        return jnp.dot(k, qT_ref[0, h * MLA_SLOT:(h + 1) * MLA_SLOT, :], preferred_element_type=F32)

    def vtile(h, j):
        return vT_ref[0, j, h * V_ROWS:(h + 1) * V_ROWS, :]

    kpos, qpos = _tile_pos(t, 0, 0)
    causal = jnp.where(kpos <= qpos, 0.0, NEG)
    _causal_flash(i, i, [lambda n: n], lambda h: 0, t, range(hb), scores, vtile, causal, s_ref, acc_ref)
    for h in range(hb):
        oT_ref[0, h * MLA_V:(h + 1) * MLA_V, :] = _normalized(acc_ref.at[h]).astype(BF16)


def _mla_attn(qT, k, vT):
    b, hw, s = qT.shape
    t = min(ATTN_TILE, s)
    nkv = s // t
    hb = MLA_HEADS_PER_STEP
    kern = functools.partial(_mla_attn_kernel, t=t, hb=hb)
    return pl.pallas_call(
        kern,
        grid=(b, MLA_HEADS // hb, s // t),
        in_specs=[
            pl.BlockSpec((1, hb * MLA_SLOT, t), lambda bi, h, i: (bi, h, i)),
            pl.BlockSpec((1, s, hb * MLA_SLOT), lambda bi, h, i: (bi, 0, h)),
            pl.BlockSpec((1, nkv, hb * V_ROWS, t), lambda bi, h, i: (bi, 0, h, 0)),
        ],
        out_specs=pl.BlockSpec((1, hb * MLA_V, t), lambda bi, h, i: (bi, h, i)),
        out_shape=jax.ShapeDtypeStruct((b, MLA_HEADS * MLA_V, s), BF16),
        scratch_shapes=[pltpu.VMEM((hb, t, t), F32), pltpu.VMEM((hb, V_ROWS, t), F32)],
        compiler_params=_cparams(("arbitrary", "arbitrary", "arbitrary")),
        name="mla_attn",
    )(qT, k, vT)


def _post_kernel(*refs, has_tok):
    if has_tok:
        x_ref, ytok_ref, yT_ref, wo_tok_ref, wo_ref, gpost_ref, gpre_ref, w1_ref, w2_ref, gmlp_ref, o_ref = refs
    else:
        x_ref, yT_ref, wo_ref, gpost_ref, gpre_ref, w1_ref, w2_ref, gmlp_ref, o_ref = refs
    x = x_ref[0]
    mix = _tn(yT_ref[0], wo_ref[...])
    if has_tok:
        mix = mix + jnp.dot(ytok_ref[0], wo_tok_ref[...], preferred_element_type=F32)
    x1 = x + _rms(mix, gpost_ref[...])
    h = _rms(x1, gpre_ref[...]).astype(BF16)
    acc = jnp.zeros_like(x1)
    for c in range(D_FF // FF_CHUNK):
        a = jnp.dot(h, w1_ref[:, c * FF_CHUNK:(c + 1) * FF_CHUNK], preferred_element_type=F32)
        a = jnp.maximum(a, 0.0)
        a = (a * a).astype(BF16)
        acc = acc + jnp.dot(a, w2_ref[c * FF_CHUNK:(c + 1) * FF_CHUNK, :], preferred_element_type=F32)
    o_ref[0] = x1 + _rms(acc, gmlp_ref[...])


def _post(x, ytok, yT, wo_tok, wo, gpost, gpre, w1, w2, gmlp):
    b, s, d = x.shape
    tm = min(POST_TILE, s)
    has_tok = ytok is not None
    kern = functools.partial(_post_kernel, has_tok=has_tok)
    xspec = pl.BlockSpec((1, tm, d), lambda i, j: (i, j, 0))
    ins, specs = [x], [xspec]
    if has_tok:
        ins.append(ytok)
        specs.append(pl.BlockSpec((1, tm, ytok.shape[-1]), lambda i, j: (i, j, 0)))
    ins.append(yT)
    specs.append(pl.BlockSpec((1, yT.shape[1], tm), lambda i, j: (i, 0, j)))
    if has_tok:
        ins.append(wo_tok)
        specs.append(_const_spec(wo_tok.shape, single_buffer=True))
    for a in (wo, gpost, gpre, w1, w2, gmlp):
        ins.append(a)
        specs.append(_const_spec(a.shape, single_buffer=True))
    return pl.pallas_call(
        kern,
        grid=(b, s // tm),
        in_specs=specs,
        out_specs=xspec,
        out_shape=jax.ShapeDtypeStruct(x.shape, x.dtype),
        compiler_params=_cparams(("arbitrary", "arbitrary")),
        name="post_tok" if has_tok else "post",
    )(*ins)


def _c_pre_kernel(x_ref, g_ref, wq_ref, wtok_ref, wf_ref, kfs_ref, kfw_ref, qT_ref, kc_ref, vc_ref, ks_ref, kw_ref,
                  vsT_ref, vwT_ref, gT_ref, z_ref, *, tm, q_scale):
    xn = _rms(x_ref[0], g_ref[...]).astype(BF16)
    qT_ref[0] = (_nt(wq_ref[...], xn) * q_scale).astype(BF16)
    tok = jnp.dot(xn, wtok_ref[...], preferred_element_type=F32)
    halves = KV_W // LANES
    for lg in range(2 * halves):
        z_ref[lg] = tok[:, lg * LANES:(lg + 1) * LANES]
    for l in range(CMP_STRIDE):
        for lg in range(2 * halves):
            rows = z_ref[lg, pl.ds(l, tm // CMP_STRIDE, stride=CMP_STRIDE), :].astype(BF16)
            out_ref = kc_ref if lg < halves else vc_ref
            off = l * KV_W + (lg % halves) * LANES
            out_ref[0, :, off:off + LANES] = rows
    kfs = kfs_ref[...]
    kfw = kfw_ref[...]
    for g in range(NSA_GROUPS):
        a = 2 * KV_W + g * NSA_DH
        ks_ref[0, g] = jnp.concatenate([tok[:, a:a + NSA_DH], kfs], axis=1).astype(BF16)
        kw_ref[0, g] = jnp.concatenate([tok[:, a + KV_W:a + KV_W + NSA_DH], kfw], axis=1).astype(BF16)
    fT = _nt(wf_ref[...], xn)
    ones = _ones_rows(tm)

    def with_ones(v):
        return jnp.concatenate([x for g in range(NSA_GROUPS) for x in (v[g * NSA_DH:(g + 1) * NSA_DH], ones)],
                               axis=0)

    vs = with_ones(fT[:KV_W])
    vw = with_ones(fT[KV_W:2 * KV_W])
    for cidx in range(tm // ATTN_TILE):
        sl = slice(cidx * ATTN_TILE, (cidx + 1) * ATTN_TILE)
        vsT_ref[0, cidx] = vs[:, sl].astype(BF16)
        vwT_ref[0, cidx] = vw[:, sl].astype(BF16)
    gT_ref[0] = 1.0 / (1.0 + jnp.exp(-fT[2 * KV_W:]))


def _c_pre(x, g, wq, wtok, wf, kfs, kfw):
    b, s, d = x.shape
    tm = min(TOKEN_TILE, s)
    q_scale = NSA_DH ** -0.5 * LOG2E
    kern = functools.partial(_c_pre_kernel, tm=tm, q_scale=q_scale)
    nt = tm // ATTN_TILE
    grows = NSA_GROUPS * GATE_ROWS
    kaug = NSA_DH + kfs.shape[1]
    vrows = NSA_GROUPS * V_ROWS
    kf_spec = pl.BlockSpec((tm, kfs.shape[1]), lambda i, j: (j, 0))
    return pl.pallas_call(
        kern,
        grid=(b, s // tm),
        in_specs=[pl.BlockSpec((1, tm, d), lambda i, j: (i, j, 0)),
                  _const_spec(g.shape), _const_spec(wq.shape), _const_spec(wtok.shape), _const_spec(wf.shape),
                  kf_spec, kf_spec],
        out_specs=[
            pl.BlockSpec((1, C_MIX, tm), lambda i, j: (i, 0, j)),
            pl.BlockSpec((1, tm // CMP_STRIDE, CMP_STRIDE * KV_W), lambda i, j: (i, j, 0)),
            pl.BlockSpec((1, tm // CMP_STRIDE, CMP_STRIDE * KV_W), lambda i, j: (i, j, 0)),
            pl.BlockSpec((1, NSA_GROUPS, tm, kaug), lambda i, j: (i, 0, j, 0)),
            pl.BlockSpec((1, NSA_GROUPS, tm, kaug), lambda i, j: (i, 0, j, 0)),
            pl.BlockSpec((1, nt, vrows, ATTN_TILE), lambda i, j: (i, j, 0, 0)),
            pl.BlockSpec((1, nt, vrows, ATTN_TILE), lambda i, j: (i, j, 0, 0)),
            pl.BlockSpec((1, grows, tm), lambda i, j: (i, 0, j)),
        ],
        out_shape=[
            jax.ShapeDtypeStruct((b, C_MIX, s), BF16),
            jax.ShapeDtypeStruct((b, s // CMP_STRIDE, CMP_STRIDE * KV_W), BF16),
            jax.ShapeDtypeStruct((b, s // CMP_STRIDE, CMP_STRIDE * KV_W), BF16),
            jax.ShapeDtypeStruct((b, NSA_GROUPS, s, kaug), BF16),
            jax.ShapeDtypeStruct((b, NSA_GROUPS, s, kaug), BF16),
            jax.ShapeDtypeStruct((b, s // ATTN_TILE, vrows, ATTN_TILE), BF16),
            jax.ShapeDtypeStruct((b, s // ATTN_TILE, vrows, ATTN_TILE), BF16),
            jax.ShapeDtypeStruct((b, grows, s), F32),
        ],
        scratch_shapes=[pltpu.VMEM((2 * KV_W // LANES, tm, LANES), F32)],
        compiler_params=_cparams(("arbitrary", "arbitrary")),
        name="c_pre",
    )(x, g, wq, wtok, wf, kfs, kfw)


def _gelu_tanh(x):
    return 0.5 * x * (1.0 + jnp.tanh(math.sqrt(2.0 / math.pi) * (x + 0.044715 * (x * x * x))))


def _compress_kernel(zk_ref, zv_ref, pek_ref, pev_ref, wak_ref, wbk_ref, wav_ref, wbv_ref, w2k_ref, w2vT_ref, kf_ref,
                     kc_ref, vcT_ref, *, nch):
    def hidden(z_ref, pe_ref, wa_ref, wb_ref):
        z = z_ref[0].astype(F32)
        za = (z + pe_ref[0:1]).astype(BF16)
        zb = (z + pe_ref[1:2]).astype(BF16)
        a = jnp.dot(za, wa_ref[...], preferred_element_type=F32)
        bm = jnp.dot(zb, wb_ref[...], preferred_element_type=F32)
        return _gelu_tanh(a + pltpu.roll(bm, nch - 1, 0)).astype(BF16)

    hk = hidden(zk_ref, pek_ref, wak_ref, wbk_ref)
    kc = jnp.dot(hk, w2k_ref[...], preferred_element_type=F32)
    kf = kf_ref[...]
    hv = hidden(zv_ref, pev_ref, wav_ref, wbv_ref)
    vcT = _nt(w2vT_ref[...], hv)
    ones = _ones_rows(nch)
    for g in range(NSA_GROUPS):
        gs = slice(g * NSA_DH, (g + 1) * NSA_DH)
        kc_ref[0, g] = jnp.concatenate([kc[:, gs], kf], axis=1).astype(BF16)
        vcT_ref[0, g] = jnp.concatenate([vcT[gs], ones], axis=0).astype(BF16)


def _compress(zk, zv, pek, pev, wak, wbk, wav, wbv, w2k, w2vT, kf):
    b, nch, cw = zk.shape
    kern = functools.partial(_compress_kernel, nch=nch)
    zspec = pl.BlockSpec((1, nch, cw), lambda i: (i, 0, 0))
    consts = (pek, pev, wak, wbk, wav, wbv, w2k, w2vT, kf)
    return pl.pallas_call(
        kern,
        grid=(b,),
        in_specs=[zspec, zspec] + [_const_spec(a.shape) for a in consts],
        out_specs=[pl.BlockSpec((1, NSA_GROUPS, nch, NSA_DH + KFEAT), lambda i: (i, 0, 0, 0)),
                   pl.BlockSpec((1, NSA_GROUPS, V_ROWS, nch), lambda i: (i, 0, 0, 0))],
        out_shape=[jax.ShapeDtypeStruct((b, NSA_GROUPS, nch, NSA_DH + KFEAT), BF16),
                   jax.ShapeDtypeStruct((b, NSA_GROUPS, V_ROWS, nch), BF16)],
        compiler_params=_cparams(("arbitrary",)),
        name="compress",
    )(zk, zv, *consts)


def _split3(x):
    hi = x.astype(BF16)
    r = x - hi.astype(F32)
    mid = r.astype(BF16)
    lo = (r - mid.astype(F32)).astype(BF16)
    return hi, mid, lo


def _feature_rows(tab_ref, head, const, t):
    row = lax.broadcasted_iota(jnp.int32, (KFEAT, t), 0)
    r3 = row - 3 * (row >= 3).astype(jnp.int32) - 3 * (row >= 6).astype(jnp.int32)
    s_hi, s_mid, s_lo, slope = (tab_ref[head * 4 + k] for k in range(4))
    c_hi, c_mid, c_lo = (x.astype(F32) for x in _split3(slope * const))
    svals = jnp.where(r3 == 0, s_hi, jnp.where(r3 == 1, s_mid, s_lo))
    cvals = jnp.where(r3 == 0, c_hi, jnp.where(r3 == 1, c_mid, c_lo))
    return jnp.where(row < 6, svals, jnp.where(row < 9, cvals, 0.0)).astype(BF16)


def _cmp_topk_kernel(tab_ref, qT_ref, kc_ref, vcT_ref, pool_ref, gT_ref, ocT_ref, bias_ref, act_ref,
                     qa_ref, e_ref, acc_ref, imp_ref, rank_ref, ps_ref, *, t, gb, nch, nslc, nsel, ov_terms):
    gp = pl.program_id(1)
    i = pl.program_id(2)
    q0 = i * t
    chains = range(gb * NSA_REP)
    bpt = t // SLC_LEN

    const = (CMP_LEN - 1) / 2.0 - jnp.full((KFEAT, t), q0, jnp.int32).astype(F32)
    for c in chains:
        head = (gp * gb + c // NSA_REP) * NSA_REP + c % NSA_REP
        qa_ref[c, 0:NSA_DH, :] = qT_ref[0, c * NSA_DH:(c + 1) * NSA_DH, :]
        qa_ref[c, NSA_DH:NSA_DH + KFEAT, :] = _feature_rows(tab_ref, head, const, t)

    n_i = lax.broadcasted_iota(jnp.int32, (nch, t), 0)
    t_i = q0 + lax.broadcasted_iota(jnp.int32, (nch, t), 1)
    ended = jnp.where(n_i * CMP_STRIDE + (CMP_LEN - 1) <= t_i, 0.0, NEG)
    def cmp_scores(c):
        s = jnp.dot(kc_ref[0, c // NSA_REP], qa_ref[c], preferred_element_type=F32) + ended
        return _stage_scores(s, e_ref.at[c])

    def cmp_softmax_pv(c, mt):
        m_eff = jnp.where(mt > 0.5 * NEG, mt, 0.0)
        e = jnp.exp2(e_ref[c] - m_eff)
        e_ref[c] = e
        acc_ref[c] = jnp.dot(vcT_ref[0, c // NSA_REP], e.astype(BF16), preferred_element_type=F32)

    mt = [cmp_scores(c) for c in chains]
    for c in chains:
        cmp_softmax_pv(c, mt[c])

    j_i = lax.broadcasted_iota(jnp.int32, (nslc, t), 0)
    cur = jnp.right_shift(q0 + lax.broadcasted_iota(jnp.int32, (nslc, t), 1), SLC_SHIFT)
    forced = (j_i == 0) | (j_i == cur) | (j_i == cur - 1)
    per_blk = SLC_LEN // CMP_STRIDE
    for gl in range(gb):
        psum = jnp.zeros((nch, t), F32)
        for r in range(NSA_REP):
            c = gl * NSA_REP + r
            acc = acc_ref[c]
            l = acc[NSA_DH:NSA_DH + 1]
            inv = 1.0 / jnp.where(l > 0.0, l, 1.0)
            gate = gT_ref[0, gl * GATE_ROWS + r:gl * GATE_ROWS + r + 1, :]
            ocT_ref[0, c * NSA_DH:(c + 1) * NSA_DH, :] = (gate * (acc[:NSA_DH] * inv)).astype(BF16)
            psum = psum + e_ref[c] * inv
        for h in range(t // LANES):
            ps_ref[h] = psum[:, h * LANES:(h + 1) * LANES]
        imp = jnp.zeros((nslc, t), F32)
        for k, wgt in ov_terms:
            rows = jnp.concatenate([ps_ref[h, pl.ds(k % per_blk, nslc, stride=per_blk), :]
                                    for h in range(t // LANES)], axis=1)
            if k < 0:
                rows = jnp.where(j_i == 0, 0.0, pltpu.roll(rows, 1, 0))
            imp = imp + wgt * rows
        imp_ref[gl] = jnp.where(j_i > cur, NEG, imp + jnp.where(forced, FORCE_BONUS, 0.0))
        rank_ref[gl] = jnp.zeros((nslc, t), F32)

    for chunk in range(nslc // bpt):
        @pl.when(chunk <= i)
        def _():
            for gl in range(gb):
                imp = imp_ref[gl]
                rank = rank_ref[gl]
                for jp in range(chunk * bpt, (chunk + 1) * bpt):
                    row = imp[jp:jp + 1, :]
                    incs = []
                    for rg in range(nslc // SUBLANES):
                        r0 = SUBLANES * rg
                        blk = imp[r0:r0 + SUBLANES]
                        if r0 > jp:
                            incs.append(jnp.where(row >= blk, 1.0, 0.0))
                        elif r0 + SUBLANES - 1 <= jp:
                            incs.append(jnp.where(row > blk, 1.0, 0.0))
                        else:
                            above = lax.broadcasted_iota(jnp.int32, (SUBLANES, t), 0) > jp - r0
                            incs.append(jnp.where(above, jnp.where(row >= blk, 1.0, 0.0),
                                                  jnp.where(row > blk, 1.0, 0.0)))
                    rank = rank + jnp.concatenate(incs, axis=0)
                rank_ref[gl] = rank

    for gl in range(gb):
        bias = jnp.where(j_i > cur, NEG, jnp.where(rank_ref[gl] < nsel, 0.0, NEG))
        bias_ref[0, gl] = bias
        hit = jnp.where(jnp.max(bias, axis=1, keepdims=True) == 0.0, 1.0, 0.0)
        tiles = jnp.sum(hit * pool_ref[...], axis=0, keepdims=True)
        act_ref[0, gl, 0] = jnp.broadcast_to(jnp.where(tiles > 0.0, 1, 0), (SUBLANES, LANES)).astype(jnp.int32)


def _overlap_terms():
    per_blk = SLC_LEN // CMP_STRIDE
    terms = []
    for k in range(-(CMP_LEN // CMP_STRIDE) + 1, per_blk):
        lo, hi = max(k * CMP_STRIDE, 0), min(k * CMP_STRIDE + CMP_LEN, SLC_LEN)
        if hi > lo:
            terms.append((k, (hi - lo) / CMP_LEN))
    assert SLC_LEN % CMP_STRIDE == 0 and all(k >= -per_blk for k, _ in terms)
    return tuple(terms)


def _cmp_topk(tab, qT, kc, vcT, gT, nsel):
    b, _, s = qT.shape
    t = ATTN_TILE
    assert s % t == 0 and t % SLC_LEN == 0
    nch = kc.shape[2]
    nslc = s // SLC_LEN
    gb = CMP_GROUPS_PER_STEP
    nc = gb * NSA_REP
    kern = functools.partial(_cmp_topk_kernel, t=t, gb=gb, nch=nch, nslc=nslc, nsel=nsel,
                             ov_terms=_overlap_terms())
    gw = nc * NSA_DH
    assert s // t <= LANES
    pool = np.zeros((nslc, LANES), np.float32)
    pool[np.arange(nslc), np.arange(nslc) // (t // SLC_LEN)] = 1.0
    pool = jnp.asarray(pool, F32)
    return pl.pallas_call(
        kern,
        grid=(b, NSA_GROUPS // gb, s // t),
        in_specs=[
            pl.BlockSpec(memory_space=pltpu.SMEM),
            pl.BlockSpec((1, gw, t), lambda bi, g, i: (bi, g, i)),
            pl.BlockSpec((1, gb, nch, NSA_DH + KFEAT), lambda bi, g, i: (bi, g, 0, 0)),
            pl.BlockSpec((1, gb, V_ROWS, nch), lambda bi, g, i: (bi, g, 0, 0)),
            _const_spec(pool.shape),
            pl.BlockSpec((1, gb * GATE_ROWS, t), lambda bi, g, i: (bi, g, i)),
        ],
        out_specs=[pl.BlockSpec((1, gw, t), lambda bi, g, i: (bi, g, i)),
                   pl.BlockSpec((1, gb, nslc, t), lambda bi, g, i: (bi, g, 0, i)),
                   pl.BlockSpec((1, gb, 1, SUBLANES, LANES), lambda bi, g, i: (bi, g, i, 0, 0))],
        out_shape=[jax.ShapeDtypeStruct((b, C_MIX, s), BF16),
                   jax.ShapeDtypeStruct((b, NSA_GROUPS, nslc, s), F32),
                   jax.ShapeDtypeStruct((b, NSA_GROUPS, s // t, SUBLANES, LANES), jnp.int32)],
        scratch_shapes=[pltpu.VMEM((nc, NSA_DH + KFEAT, t), BF16),
                        pltpu.VMEM((nc, nch, t), F32),
                        pltpu.VMEM((nc, V_ROWS, t), F32),
                        pltpu.VMEM((gb, nslc, t), F32), pltpu.VMEM((gb, nslc, t), F32),
                        pltpu.VMEM((t // LANES, nch, LANES), F32)],
        compiler_params=_cparams(("arbitrary", "arbitrary", "arbitrary")),
        name="cmp_topk",
    )(tab, qT, kc, vcT, pool, gT)


def _slc_win_kernel(act_ref, tab_ref, qT_ref, ks_ref, vsT_ref, kw_ref, vwT_ref, bias_ref, ocT_ref, gT_ref, yT_ref,
                    qa_ref, s_ref, acc_ref, sw_ref, accw_ref, idx_ref, *, t, gb, nslc, nkv):
    gp = pl.program_id(1)
    i = pl.program_id(2)
    chains = range(gb * NSA_REP)
    feat0 = NSA_DH + nslc

    const = -jnp.full((KFEAT, t), i * t, jnp.int32).astype(F32)
    for c in chains:
        gl, r = divmod(c, NSA_REP)
        head = (gp * gb + gl) * NSA_REP + r
        qa_ref[c, 0:NSA_DH, :] = qT_ref[0, c * NSA_DH:(c + 1) * NSA_DH, :]
        qa_ref[c, NSA_DH:feat0, :] = bias_ref[0, gl].astype(BF16)
        qa_ref[c, feat0:feat0 + KFEAT, :] = _feature_rows(tab_ref, head, const, t)

    def scores_from(k_ref):
        def scores(c, j):
            k = k_ref[0, c // NSA_REP, pl.ds(pl.multiple_of(j * t, t), t), :]
            return jnp.dot(k, qa_ref[c], preferred_element_type=F32)
        return scores

    def vtile_from(vT_ref):
        def vtile(c, j):
            gl = c // NSA_REP
            return vT_ref[0, j, gl * V_ROWS:(gl + 1) * V_ROWS, :]
        return vtile

    kpos, qpos = _tile_pos(t, 0, 0)
    causal = jnp.where(kpos <= qpos, 0.0, NEG)

    w_scores = scores_from(kw_ref)
    w_vtile = vtile_from(vwT_ref)
    w_tiles = (i, jnp.maximum(i - 1, 0), jnp.maximum(i - 2, 0))
    in_seq1 = jnp.where((i - 1) * t + kpos >= 0, 0.0, NEG)
    in_seq2 = jnp.where((i - 2) * t + kpos >= 0, 0.0, NEG)
    w_extra = (causal, in_seq1, jnp.where(qpos < kpos, in_seq2, NEG))
    m = [jnp.full((1, t), NEG, F32) for _ in chains]
    mt = []
    for c in chains:
        accw_ref[c] = jnp.zeros((V_ROWS, t), F32)
        mt.append(_stage_scores(w_scores(c, w_tiles[0]) + w_extra[0], sw_ref.at[c]))
    for w in range(3):
        for c in chains:
            p, m[c], alpha = _stage_softmax(sw_ref.at[c], m[c], mt[c])
            if w + 1 < 3:
                mt[c] = _stage_scores(w_scores(c, w_tiles[w + 1]) + w_extra[w + 1], sw_ref.at[c])
            _stage_pv(accw_ref.at[c], w_vtile(c, w_tiles[w]), p, alpha)

    stride = nkv + 1
    counts, spares = [], []
    for gl in range(gb):
        g_abs = gp * gb + gl
        base = ((pl.program_id(0) * NSA_GROUPS + g_abs) * pl.num_programs(2) + i) * nkv
        n_act, spare = jnp.int32(0), jnp.int32(0)
        for j in range(nkv):
            idx_ref[gl * stride + n_act] = j
            keep = (j < i) & (act_ref[base + j] != 0)
            n_act = n_act + jnp.where(keep, 1, 0)
            spare = jnp.where((j < i) & jnp.logical_not(keep), j, spare)
        counts.append(n_act)
        spares.append(spare)
    n_rest = functools.reduce(jnp.maximum, counts)
    for gl in range(gb):
        for k in range(nkv):
            idx_ref[gl * stride + k] = jnp.where(k >= counts[gl], spares[gl], idx_ref[gl * stride + k])
    tile_fns = [functools.partial(lambda gl, n: idx_ref[gl * stride + n], gl) for gl in range(gb)]
    _causal_flash(i, n_rest, tile_fns, lambda c: c // NSA_REP, t, chains, scores_from(ks_ref),
                  vtile_from(vsT_ref), causal, s_ref, acc_ref)

    for c in chains:
        gl, r = divmod(c, NSA_REP)
        g0 = gl * GATE_ROWS
        rs = slice(c * NSA_DH, (c + 1) * NSA_DH)
        y = (ocT_ref[0, rs, :].astype(F32)
             + gT_ref[0, g0 + NSA_REP + r:g0 + NSA_REP + r + 1, :] * _normalized(acc_ref.at[c])
             + gT_ref[0, g0 + 2 * NSA_REP + r:g0 + 2 * NSA_REP + r + 1, :] * _normalized(accw_ref.at[c]))
        yT_ref[0, rs, :] = y.astype(BF16)


def _slc_win(act, tab, qT, ks, vsT, kw, vwT, bias, ocT, gT):
    b, _, s = qT.shape
    t = ATTN_TILE
    assert s % t == 0 and WINDOW == 2 * t
    nkv = s // t
    nslc = bias.shape[2]
    kaug = ks.shape[3]
    gb = NSA_GROUPS_PER_STEP
    nc = gb * NSA_REP
    gw = nc * NSA_DH
    kern = functools.partial(_slc_win_kernel, t=t, gb=gb, nslc=nslc, nkv=nkv)
    qspec = pl.BlockSpec((1, gw, t), lambda bi, g, i, *_: (bi, g, i))
    kspec = pl.BlockSpec((1, gb, s, kaug), lambda bi, g, i, *_: (bi, g, 0, 0))
    vspec = pl.BlockSpec((1, nkv, gb * V_ROWS, t), lambda bi, g, i, *_: (bi, 0, g, 0))
    grid_spec = pltpu.PrefetchScalarGridSpec(
        num_scalar_prefetch=2,
        grid=(b, NSA_GROUPS // gb, s // t),
        in_specs=[
            qspec, kspec, vspec, kspec, vspec,
            pl.BlockSpec((1, gb, nslc, t), lambda bi, g, i, *_: (bi, g, 0, i)),
            qspec,
            pl.BlockSpec((1, gb * GATE_ROWS, t), lambda bi, g, i, *_: (bi, g, i)),
        ],
        out_specs=qspec,
        scratch_shapes=[pltpu.VMEM((nc, kaug, t), BF16),
                        pltpu.VMEM((nc, t, t), F32), pltpu.VMEM((nc, V_ROWS, t), F32),
                        pltpu.VMEM((nc, t, t), F32), pltpu.VMEM((nc, V_ROWS, t), F32),
                        pltpu.SMEM((gb * (nkv + 1),), jnp.int32)],
    )
    return pl.pallas_call(
        kern,
        grid_spec=grid_spec,
        out_shape=jax.ShapeDtypeStruct((b, C_MIX, s), BF16),
        compiler_params=_cparams(("arbitrary", "arbitrary", "arbitrary")),
        name="slc_win",
    )(act, tab, qT, ks, vsT, kw, vwT, bias, ocT, gT)


def _row(v):
    return v.reshape(1, -1).astype(F32)


def _layer_a(x, g_pre, w_in, conv_w, q_norm, w_q_up, kv_norm, w_kv_up, w_out, g_post, g_mlp_pre, w1, w2, g_mlp_post):
    b, s, d = x.shape
    o3 = 3 * CONV_WIDTH
    o5 = o3 + MLA_Q_RANK + MLA_KV_RANK
    pad = MLA_SLOT - MLA_NOPE - MLA_ROPE
    w_main = w_in[:, :o5].astype(BF16)
    wkr = jnp.pad(w_in[:, o5:].T, ((MLA_NOPE, pad), (0, 0))).astype(BF16)
    wq = w_q_up.reshape(MLA_Q_RANK, MLA_HEADS, MLA_NOPE + MLA_ROPE)
    wq = jnp.pad(wq, ((0, 0), (0, 0), (0, pad))).reshape(MLA_Q_RANK, MLA_HEADS * MLA_SLOT).T.astype(BF16)
    wkv = w_kv_up.reshape(MLA_KV_RANK, MLA_HEADS, MLA_NOPE + MLA_V)
    wk = jnp.pad(wkv[:, :, :MLA_NOPE], ((0, 0), (0, 0), (0, MLA_SLOT - MLA_NOPE)))
    wk = wk.reshape(MLA_KV_RANK, MLA_HEADS * MLA_SLOT).astype(BF16)
    wv = wkv[:, :, MLA_NOPE:].reshape(MLA_KV_RANK, MLA_HEADS * MLA_V).T.astype(BF16)

    inv = ROPE_THETA ** (-jnp.arange(ROPE_HALF, dtype=F32) / ROPE_HALF)
    ang = inv[:, None] * jnp.arange(s, dtype=F32)[None, :]
    cos, sin = jnp.cos(ang), jnp.sin(ang)

    yconv, qT, k, vT = _a_pre(x, _row(g_pre), w_main, wkr, conv_w.astype(F32), _row(q_norm), wq, _row(kv_norm),
                              wk, wv, cos, sin)
    yT = _mla_attn(qT, k, vT)
    wo = w_out.astype(BF16)
    return _post(x, yconv, yT, wo[:CONV_WIDTH], wo[CONV_WIDTH:], _row(g_post), _row(g_mlp_pre),
                 w1.astype(BF16), w2.astype(BF16), _row(g_mlp_post))


def _layer_c(x, g_pre, w_in, pe_k, w1_k, w2_k, pe_v, w1_v, w2_v, w_out, g_post, g_mlp_pre, w1, w2, g_mlp_post):
    b, s, d = x.shape
    G, R, Dh = NSA_GROUPS, NSA_REP, NSA_DH
    wq = w_in[:, :C_MIX].T.astype(BF16)
    kv = w_in[:, C_MIX:C_MIX + 6 * KV_W].reshape(d, 6, KV_W)
    wtok = jnp.concatenate([kv[:, 0], kv[:, 1], kv[:, 2], kv[:, 4]], axis=1).astype(BF16)
    wg = w_in[:, C_MIX + 6 * KV_W:].reshape(d, 3, G, R).transpose(0, 2, 1, 3)
    wg = jnp.pad(wg.reshape(d, G, 3 * R), ((0, 0), (0, 0), (0, GATE_ROWS - 3 * R))).reshape(d, G * GATE_ROWS)
    wf = jnp.concatenate([kv[:, 3], kv[:, 5], wg], axis=1).T.astype(BF16)

    n_slc = s // SLC_LEN
    pos = np.arange(s)
    blk, within = pos // SLC_LEN, pos % SLC_LEN
    feats = np.zeros((s, KFEAT), np.float32)
    feats[:, 0:3] = (blk * SLC_LEN)[:, None]
    feats[:, 3:6] = within[:, None]
    feats[:, 6:9] = 1.0
    onehot = (blk[:, None] == np.arange(n_slc)[None, :]).astype(np.float32)
    kfs = jnp.asarray(np.concatenate([onehot, feats], axis=1), F32)
    kfw = jnp.asarray(np.concatenate([np.zeros_like(onehot), feats], axis=1), F32)

    qT, kc_z, vc_z, ks, kw, vsT, vwT, gT = _c_pre(x, _row(g_pre), wq, wtok, wf, kfs, kfw)

    nch = s // CMP_STRIDE
    half = CMP_LEN // 2
    eye = jnp.eye(G, dtype=F32)

    def w1_halves(w):
        bd = jnp.einsum('hlde,gk->hlgdke', w.reshape(2, half, Dh, Dh), eye)
        bd = bd.reshape(2, half * KV_W, KV_W).astype(BF16)
        return bd[0], bd[1]

    def pe_halves(pe):
        return jnp.tile(pe.reshape(2, half, 1, Dh), (1, 1, G, 1)).reshape(2, half * KV_W).astype(F32)

    def w2_bd(w):
        return jnp.einsum('de,gk->gdke', w, eye).reshape(KV_W, KV_W)

    n_idx = np.arange(nch)
    cfeat = np.zeros((nch, KFEAT), np.float32)
    cfeat[:, 0:3] = (1024 * (n_idx // 64))[:, None]
    cfeat[:, 3:6] = (CMP_STRIDE * (n_idx % 64))[:, None]
    cfeat[:, 6:9] = 1.0
    kc, vcT = _compress(kc_z, vc_z, pe_halves(pe_k), pe_halves(pe_v), *w1_halves(w1_k), *w1_halves(w1_v),
                        w2_bd(w2_k).astype(BF16), w2_bd(w2_v).T.astype(BF16), jnp.asarray(cfeat, F32))

    slopes_np = (2.0 ** (-8.0 * np.arange(1, NSA_HEADS + 1) / NSA_HEADS)).astype(np.float32)
    full = (slopes_np * np.float32(LOG2E)).astype(np.float32)
    pieces, rest = [], full.copy()
    for _ in range(3):
        piece = rest.astype(BF16).astype(np.float32)
        pieces.append(piece)
        rest = (rest - piece).astype(np.float32)
    tab = jnp.asarray(np.stack(pieces + [full], axis=1).reshape(-1), F32)

    ocT, bias, act = _cmp_topk(tab, qT, kc, vcT, gT, min(N_SEL, n_slc))
    act = act[:, :, :, 0, :s // ATTN_TILE].reshape(-1)
    yT = _slc_win(act, tab, qT, ks, vsT, kw, vwT, bias, ocT, gT)
    return _post(x, None, yT, None, w_out.astype(BF16), _row(g_post), _row(g_mlp_pre),
                 w1.astype(BF16), w2.astype(BF16), _row(g_mlp_post))


def kernel(x, norm_mix_pre, norm_mix_post, norm_mlp_pre, norm_mlp_post, mlp_w1, mlp_w2, a_w_in, a_conv_w, a_q_norm,
           a_w_q_up, a_kv_norm, a_w_kv_up, a_w_out, c_w_in, c_cmp_pe_k, c_cmp_w1_k, c_cmp_w2_k, c_cmp_pe_v,
           c_cmp_w1_v, c_cmp_w2_v, c_w_out):
    depth = norm_mix_pre.shape[0]
    for layer in range(depth):
        i = layer // 2
        common = (norm_mix_post[layer], norm_mlp_pre[layer], mlp_w1[layer], mlp_w2[layer], norm_mlp_post[layer])
        if layer % 2 == 0:
            x = _layer_a(x, norm_mix_pre[layer], a_w_in[i], a_conv_w[i], a_q_norm[i], a_w_q_up[i], a_kv_norm[i],
                         a_w_kv_up[i], a_w_out[i], *common)
        else:
            x = _layer_c(x, norm_mix_pre[layer], c_w_in[i], c_cmp_pe_k[i], c_cmp_w1_k[i], c_cmp_w2_k[i],
                         c_cmp_pe_v[i], c_cmp_w1_v[i], c_cmp_w2_v[i], c_w_out[i], *common)
    return x
```

```python
import functools
import math

import numpy as np
import jax
import jax.numpy as jnp
from jax import lax
from jax.experimental import pallas as pl
from jax.experimental.pallas import tpu as pltpu

F32 = jnp.float32
BF16 = jnp.bfloat16

D_MODEL = 1024
D_FF = 4 * D_MODEL
EPS = 1e-6
NEG = -1e30
LOG2E = math.log2(math.e)

CONV_WIDTH = 512
CONV_K = 3
MLA_HEADS = 8
MLA_NOPE = 64
MLA_ROPE = 32
MLA_V = 64
MLA_KV_RANK = 256
MLA_Q_RANK = 768
MLA_SLOT = 128
ROPE_THETA = 10000.0
ROPE_HALF = MLA_ROPE // 2

NSA_HEADS = 16
NSA_GROUPS = 4
NSA_REP = 4
NSA_DH = 64
CMP_LEN = 32
CMP_STRIDE = 16
SLC_LEN = 64
SLC_SHIFT = 6
N_SEL = 16
WINDOW = 512
FORCE_BONUS = 1e4
KV_W = NSA_GROUPS * NSA_DH
C_MIX = NSA_HEADS * NSA_DH
GATE_ROWS = 16
ONES_ROWS = 16
V_ROWS = 64 + ONES_ROWS
LANES = 128
SUBLANES = 8
BF16_ROWS = 16
KFEAT = 16

TOKEN_TILE = 512
POST_TILE = 512
ATTN_TILE = 256
MLA_HEADS_PER_STEP = 8
NSA_GROUPS_PER_STEP = 4
CMP_GROUPS_PER_STEP = 4
FF_CHUNK = 1024
VMEM_LIMIT = 60 * 1024 * 1024


def _cparams(sem):
    return pltpu.CompilerParams(dimension_semantics=sem, vmem_limit_bytes=VMEM_LIMIT)


def _const_spec(shape, single_buffer=False):
    nd = len(shape)
    if single_buffer:
        return pl.BlockSpec(shape, lambda *_: (0,) * nd, pipeline_mode=pl.Buffered(1))
    return pl.BlockSpec(shape, lambda *_: (0,) * nd)


def _rms(x, g):
    ms = jnp.mean(x * x, axis=-1, keepdims=True)
    return x * lax.rsqrt(ms + EPS) * g


def _nt(a, b):
    return lax.dot_general(a, b, (((1,), (1,)), ((), ())), preferred_element_type=F32)


def _tn(a, b):
    return lax.dot_general(a, b, (((0,), (0,)), ((), ())), preferred_element_type=F32)


def _ones_rows(n):
    return jnp.where(lax.broadcasted_iota(jnp.int32, (ONES_ROWS, n), 0) == 0, 1.0, 0.0).astype(F32)


def _rope_rows(r1, r2, cos, sin):
    return r1 * cos - r2 * sin, r2 * cos + r1 * sin


def _a_pre_kernel(x_ref, g_ref, w1_ref, wkr_ref, convw_ref, qn_ref, wq_ref, kvn_ref, wk_ref, wv_ref,
                  cos_ref, sin_ref, yconv_ref, qT_ref, k_ref, vT_ref, carry_ref, *, tm, q_scale):
    @pl.when(pl.program_id(1) == 0)
    def _():
        carry_ref[...] = jnp.zeros_like(carry_ref)

    sub = min(ATTN_TILE, tm)
    w = convw_ref[...]
    w0, w1, w2 = w[0:1], w[1:2], w[2:3]
    hd = BF16_ROWS
    tail = carry_ref[...]
    z32 = jnp.zeros((MLA_SLOT - MLA_NOPE - MLA_ROPE, sub), F32)
    ones = _ones_rows(sub)
    o1, o2, o3 = CONV_WIDTH, 2 * CONV_WIDTH, 3 * CONV_WIDTH
    o4 = o3 + MLA_Q_RANK
    for cidx in range(tm // sub):
        rows = slice(cidx * sub, (cidx + 1) * sub)
        xn = _rms(x_ref[0, rows, :], g_ref[...]).astype(BF16)
        p1 = jnp.dot(xn, w1_ref[...], preferred_element_type=F32)
        g_b, g_c, hv = p1[:, :o1], p1[:, o1:o2], p1[:, o2:o3]
        c_q, c_kv = p1[:, o3:o4], p1[:, o4:]

        u = g_c * hv
        y = w2 * u + w1 * pltpu.roll(u, 1, 0) + w0 * pltpu.roll(u, 2, 0)
        uh = u[0:hd]
        row = lax.broadcasted_iota(jnp.int32, (hd, CONV_WIDTH), 0)
        h1 = jnp.where(row < 1, pltpu.roll(tail, 1, 0), pltpu.roll(uh, 1, 0))
        h2 = jnp.where(row < 2, pltpu.roll(tail, 2, 0), pltpu.roll(uh, 2, 0))
        yh = w2 * uh + w1 * h1 + w0 * h2
        yconv_ref[0, cidx * sub + hd:(cidx + 1) * sub, :] = (g_b[hd:] * y[hd:]).astype(BF16)
        yconv_ref[0, cidx * sub:cidx * sub + hd, :] = (g_b[0:hd] * yh).astype(BF16)
        tail = u[sub - hd:sub]

        cos = cos_ref[:, rows]
        sin = sin_ref[:, rows]

        hq = _rms(c_q, qn_ref[...]).astype(BF16)
        qT = _nt(wq_ref[...], hq)
        for h in range(MLA_HEADS):
            b0 = MLA_SLOT * h
            r1 = qT[b0 + MLA_NOPE:b0 + MLA_NOPE + ROPE_HALF]
            r2 = qT[b0 + MLA_NOPE + ROPE_HALF:b0 + MLA_NOPE + MLA_ROPE]
            e1, e2 = _rope_rows(r1, r2, cos, sin)
            slot = jnp.concatenate([qT[b0:b0 + MLA_NOPE], e1, e2, z32], axis=0) * q_scale
            qT_ref[0, b0:b0 + MLA_SLOT, rows] = slot.astype(BF16)

        krT = _nt(wkr_ref[...], xn)
        r1 = krT[MLA_NOPE:MLA_NOPE + ROPE_HALF]
        r2 = krT[MLA_NOPE + ROPE_HALF:MLA_NOPE + MLA_ROPE]
        e1, e2 = _rope_rows(r1, r2, cos, sin)
        slotT = jnp.concatenate([jnp.zeros((MLA_NOPE, sub), F32), e1, e2, z32], axis=0)
        slot = slotT.T

        hkv = _rms(c_kv, kvn_ref[...]).astype(BF16)
        kn = jnp.dot(hkv, wk_ref[...], preferred_element_type=F32)
        k = kn + jnp.concatenate([slot] * MLA_HEADS, axis=1)
        k_ref[0, rows, :] = k.astype(BF16)
        vT = _nt(wv_ref[...], hkv)
        vT = jnp.concatenate([x for h in range(MLA_HEADS) for x in (vT[h * MLA_V:(h + 1) * MLA_V], ones)], axis=0)
        vT_ref[0, cidx] = vT.astype(BF16)
    carry_ref[...] = tail


def _a_pre(x, g, w1, wkr, convw, qn, wq, kvn, wk, wv, cos, sin):
    b, s, d = x.shape
    tm = min(TOKEN_TILE, s)
    hw = MLA_HEADS * MLA_SLOT
    vw = MLA_HEADS * V_ROWS
    q_scale = (MLA_NOPE + MLA_ROPE) ** -0.5 * LOG2E
    kern = functools.partial(_a_pre_kernel, tm=tm, q_scale=q_scale)
    return pl.pallas_call(
        kern,
        grid=(b, s // tm),
        in_specs=[
            pl.BlockSpec((1, tm, d), lambda i, j: (i, j, 0)),
            _const_spec(g.shape), _const_spec(w1.shape), _const_spec(wkr.shape), _const_spec(convw.shape),
            _const_spec(qn.shape), _const_spec(wq.shape), _const_spec(kvn.shape), _const_spec(wk.shape),
            _const_spec(wv.shape),
            pl.BlockSpec((ROPE_HALF, tm), lambda i, j: (0, j)),
            pl.BlockSpec((ROPE_HALF, tm), lambda i, j: (0, j)),
        ],
        out_specs=[
            pl.BlockSpec((1, tm, CONV_WIDTH), lambda i, j: (i, j, 0)),
            pl.BlockSpec((1, hw, tm), lambda i, j: (i, 0, j)),
            pl.BlockSpec((1, tm, hw), lambda i, j: (i, j, 0)),
            pl.BlockSpec((1, tm // ATTN_TILE, vw, ATTN_TILE), lambda i, j: (i, j, 0, 0)),
        ],
        out_shape=[
            jax.ShapeDtypeStruct((b, s, CONV_WIDTH), BF16),
            jax.ShapeDtypeStruct((b, hw, s), BF16),
            jax.ShapeDtypeStruct((b, s, hw), BF16),
            jax.ShapeDtypeStruct((b, s // ATTN_TILE, vw, ATTN_TILE), BF16),
        ],
        scratch_shapes=[pltpu.VMEM((16, CONV_WIDTH), F32)],
        compiler_params=_cparams(("arbitrary", "arbitrary")),
        name="a_pre",
    )(x, g, w1, wkr, convw, qn, wq, kvn, wk, wv, cos, sin)


def _tile_pos(t, k0, q0):
    kpos = k0 + lax.broadcasted_iota(jnp.int32, (t, t), 0)
    qpos = q0 + lax.broadcasted_iota(jnp.int32, (t, t), 1)
    return kpos, qpos


def _stage_scores(s, s_slot):
    s_slot[...] = s
    return jnp.max(s, axis=0, keepdims=True)


def _stage_softmax(s_slot, m, mt):
    m_new = jnp.maximum(m, mt)
    return jnp.exp2(s_slot[...] - m_new).astype(BF16), m_new, jnp.exp2(m - m_new)


def _stage_pv(acc_slot, vT, p, alpha):
    acc_slot[...] = alpha * acc_slot[...] + jnp.dot(vT, p, preferred_element_type=F32)


def _causal_flash(i, n_rest, tile_fns, list_of, t, chains, scores, vtile, diag_extra, s_ref, acc_ref):
    mt0 = []
    for c in chains:
        mt0.append(_stage_scores(scores(c, i) + diag_extra, s_ref.at[c]))
        acc_ref[c] = jnp.zeros((V_ROWS, t), F32)
    init = (tuple(mt0), tuple(jnp.full((1, t), NEG, F32) for _ in chains))
    last = jnp.maximum(n_rest - 1, 0)

    def body(n, carry):
        mt, m = carry
        j_cur = [jnp.where(n == 0, i, f(jnp.maximum(n - 1, 0))) for f in tile_fns]
        j_next = [f(jnp.minimum(n, last)) for f in tile_fns]
        m_new, mt_next = [], []
        for c in chains:
            p, mn, alpha = _stage_softmax(s_ref.at[c], m[c], mt[c])
            m_new.append(mn)
            mt_next.append(_stage_scores(scores(c, j_next[list_of(c)]), s_ref.at[c]))
            _stage_pv(acc_ref.at[c], vtile(c, j_cur[list_of(c)]), p, alpha)
        return tuple(mt_next), tuple(m_new)

    n_iter = n_rest + 1
    quads = lax.shift_right_logical(n_iter, 2)

    def quad(k, cr):
        for u in range(4):
            cr = body(4 * k + u, cr)
        return cr

    carry = lax.fori_loop(0, quads, quad, init)
    lax.fori_loop(4 * quads, n_iter, body, carry)


def _normalized(acc_slot):
    acc = acc_slot[...]
    return acc[:V_ROWS - ONES_ROWS] / acc[V_ROWS - ONES_ROWS:V_ROWS - ONES_ROWS + 1]


def _mla_attn_kernel(qT_ref, k_ref, vT_ref, oT_ref, s_ref, acc_ref, *, t, hb):
    i = pl.program_id(2)

    def scores(h, j):
        k = k_ref[0, pl.ds(pl.multiple_of(j * t, t), t), h * MLA_SLOT:(h + 1) * MLA_SLOT]
        return jnp.dot(k, qT_ref[0, h * MLA_SLOT:(h + 1) * MLA_SLOT, :], preferred_element_type=F32)

    def vtile(h, j):
        return vT_ref[0, j, h * V_ROWS:(h + 1) * V_ROWS, :]

    kpos, qpos = _tile_pos(t, 0, 0)
    causal = jnp.where(kpos <= qpos, 0.0, NEG)
    _causal_flash(i, i, [lambda n: n], lambda h: 0, t, range(hb), scores, vtile, causal, s_ref, acc_ref)
    for h in range(hb):
        oT_ref[0, h * MLA_V:(h + 1) * MLA_V, :] = _normalized(acc_ref.at[h]).astype(BF16)


def _mla_attn(qT, k, vT):
    b, hw, s = qT.shape
    t = min(ATTN_TILE, s)
    nkv = s // t
    hb = MLA_HEADS_PER_STEP
    kern = functools.partial(_mla_attn_kernel, t=t, hb=hb)
    return pl.pallas_call(
        kern,
        grid=(b, MLA_HEADS // hb, s // t),
        in_specs=[
            pl.BlockSpec((1, hb * MLA_SLOT, t), lambda bi, h, i: (bi, h, i)),
            pl.BlockSpec((1, s, hb * MLA_SLOT), lambda bi, h, i: (bi, 0, h)),
            pl.BlockSpec((1, nkv, hb * V_ROWS, t), lambda bi, h, i: (bi, 0, h, 0)),
        ],
        out_specs=pl.BlockSpec((1, hb * MLA_V, t), lambda bi, h, i: (bi, h, i)),
        out_shape=jax.ShapeDtypeStruct((b, MLA_HEADS * MLA_V, s), BF16),
        scratch_shapes=[pltpu.VMEM((hb, t, t), F32), pltpu.VMEM((hb, V_ROWS, t), F32)],
        compiler_params=_cparams(("arbitrary", "arbitrary", "arbitrary")),
        name="mla_attn",
    )(qT, k, vT)


def _post_kernel(*refs, has_tok):
    if has_tok:
        x_ref, ytok_ref, yT_ref, wo_tok_ref, wo_ref, gpost_ref, gpre_ref, w1_ref, w2_ref, gmlp_ref, o_ref = refs
    else:
        x_ref, yT_ref, wo_ref, gpost_ref, gpre_ref, w1_ref, w2_ref, gmlp_ref, o_ref = refs
    x = x_ref[0]
    mix = _tn(yT_ref[0], wo_ref[...])
    if has_tok:
        mix = mix + jnp.dot(ytok_ref[0], wo_tok_ref[...], preferred_element_type=F32)
    x1 = x + _rms(mix, gpost_ref[...])
    h = _rms(x1, gpre_ref[...]).astype(BF16)
    acc = jnp.zeros_like(x1)
    for c in range(D_FF // FF_CHUNK):
        a = jnp.dot(h, w1_ref[:, c * FF_CHUNK:(c + 1) * FF_CHUNK], preferred_element_type=F32)
        a = jnp.maximum(a, 0.0)
        a = (a * a).astype(BF16)
        acc = acc + jnp.dot(a, w2_ref[c * FF_CHUNK:(c + 1) * FF_CHUNK, :], preferred_element_type=F32)
    o_ref[0] = x1 + _rms(acc, gmlp_ref[...])


def _post(x, ytok, yT, wo_tok, wo, gpost, gpre, w1, w2, gmlp):
    b, s, d = x.shape
    tm = min(POST_TILE, s)
    has_tok = ytok is not None
    kern = functools.partial(_post_kernel, has_tok=has_tok)
    xspec = pl.BlockSpec((1, tm, d), lambda i, j: (i, j, 0))
    ins, specs = [x], [xspec]
    if has_tok:
        ins.append(ytok)
        specs.append(pl.BlockSpec((1, tm, ytok.shape[-1]), lambda i, j: (i, j, 0)))
    ins.append(yT)
    specs.append(pl.BlockSpec((1, yT.shape[1], tm), lambda i, j: (i, 0, j)))
    if has_tok:
        ins.append(wo_tok)
        specs.append(_const_spec(wo_tok.shape, single_buffer=True))
    for a in (wo, gpost, gpre, w1, w2, gmlp):
        ins.append(a)
        specs.append(_const_spec(a.shape, single_buffer=True))
    return pl.pallas_call(
        kern,
        grid=(b, s // tm),
        in_specs=specs,
        out_specs=xspec,
        out_shape=jax.ShapeDtypeStruct(x.shape, x.dtype),
        compiler_params=_cparams(("arbitrary", "arbitrary")),
        name="post_tok" if has_tok else "post",
    )(*ins)


def _c_pre_kernel(x_ref, g_ref, wq_ref, wtok_ref, wf_ref, kfs_ref, kfw_ref, qT_ref, kc_ref, vc_ref, ks_ref, kw_ref,
                  vsT_ref, vwT_ref, gT_ref, z_ref, *, tm, q_scale):
    xn = _rms(x_ref[0], g_ref[...]).astype(BF16)
    qT_ref[0] = (_nt(wq_ref[...], xn) * q_scale).astype(BF16)
    tok = jnp.dot(xn, wtok_ref[...], preferred_element_type=F32)
    halves = KV_W // LANES
    for lg in range(2 * halves):
        z_ref[lg] = tok[:, lg * LANES:(lg + 1) * LANES]
    for l in range(CMP_STRIDE):
        for lg in range(2 * halves):
            rows = z_ref[lg, pl.ds(l, tm // CMP_STRIDE, stride=CMP_STRIDE), :].astype(BF16)
            out_ref = kc_ref if lg < halves else vc_ref
            off = l * KV_W + (lg % halves) * LANES
            out_ref[0, :, off:off + LANES] = rows
    kfs = kfs_ref[...]
    kfw = kfw_ref[...]
    for g in range(NSA_GROUPS):
        a = 2 * KV_W + g * NSA_DH
        ks_ref[0, g] = jnp.concatenate([tok[:, a:a + NSA_DH], kfs], axis=1).astype(BF16)
        kw_ref[0, g] = jnp.concatenate([tok[:, a + KV_W:a + KV_W + NSA_DH], kfw], axis=1).astype(BF16)
    fT = _nt(wf_ref[...], xn)
    ones = _ones_rows(tm)

    def with_ones(v):
        return jnp.concatenate([x for g in range(NSA_GROUPS) for x in (v[g * NSA_DH:(g + 1) * NSA_DH], ones)],
                               axis=0)

    vs = with_ones(fT[:KV_W])
    vw = with_ones(fT[KV_W:2 * KV_W])
    for cidx in range(tm // ATTN_TILE):
        sl = slice(cidx * ATTN_TILE, (cidx + 1) * ATTN_TILE)
        vsT_ref[0, cidx] = vs[:, sl].astype(BF16)
        vwT_ref[0, cidx] = vw[:, sl].astype(BF16)
    gT_ref[0] = 1.0 / (1.0 + jnp.exp(-fT[2 * KV_W:]))


def _c_pre(x, g, wq, wtok, wf, kfs, kfw):
    b, s, d = x.shape
    tm = min(TOKEN_TILE, s)
    q_scale = NSA_DH ** -0.5 * LOG2E
    kern = functools.partial(_c_pre_kernel, tm=tm, q_scale=q_scale)
    nt = tm // ATTN_TILE
    grows = NSA_GROUPS * GATE_ROWS
    kaug = NSA_DH + kfs.shape[1]
    vrows = NSA_GROUPS * V_ROWS
    kf_spec = pl.BlockSpec((tm, kfs.shape[1]), lambda i, j: (j, 0))
    return pl.pallas_call(
        kern,
        grid=(b, s // tm),
        in_specs=[pl.BlockSpec((1, tm, d), lambda i, j: (i, j, 0)),
                  _const_spec(g.shape), _const_spec(wq.shape), _const_spec(wtok.shape), _const_spec(wf.shape),
                  kf_spec, kf_spec],
        out_specs=[
            pl.BlockSpec((1, C_MIX, tm), lambda i, j: (i, 0, j)),
            pl.BlockSpec((1, tm // CMP_STRIDE, CMP_STRIDE * KV_W), lambda i, j: (i, j, 0)),
            pl.BlockSpec((1, tm // CMP_STRIDE, CMP_STRIDE * KV_W), lambda i, j: (i, j, 0)),
            pl.BlockSpec((1, NSA_GROUPS, tm, kaug), lambda i, j: (i, 0, j, 0)),
            pl.BlockSpec((1, NSA_GROUPS, tm, kaug), lambda i, j: (i, 0, j, 0)),
            pl.BlockSpec((1, nt, vrows, ATTN_TILE), lambda i, j: (i, j, 0, 0)),
            pl.BlockSpec((1, nt, vrows, ATTN_TILE), lambda i, j: (i, j, 0, 0)),
            pl.BlockSpec((1, grows, tm), lambda i, j: (i, 0, j)),
        ],
        out_shape=[
            jax.ShapeDtypeStruct((b, C_MIX, s), BF16),
            jax.ShapeDtypeStruct((b, s // CMP_STRIDE, CMP_STRIDE * KV_W), BF16),
            jax.ShapeDtypeStruct((b, s // CMP_STRIDE, CMP_STRIDE * KV_W), BF16),
            jax.ShapeDtypeStruct((b, NSA_GROUPS, s, kaug), BF16),
            jax.ShapeDtypeStruct((b, NSA_GROUPS, s, kaug), BF16),
            jax.ShapeDtypeStruct((b, s // ATTN_TILE, vrows, ATTN_TILE), BF16),
            jax.ShapeDtypeStruct((b, s // ATTN_TILE, vrows, ATTN_TILE), BF16),
            jax.ShapeDtypeStruct((b, grows, s), F32),
        ],
        scratch_shapes=[pltpu.VMEM((2 * KV_W // LANES, tm, LANES), F32)],
        compiler_params=_cparams(("arbitrary", "arbitrary")),
        name="c_pre",
    )(x, g, wq, wtok, wf, kfs, kfw)


def _gelu_tanh(x):
    return 0.5 * x * (1.0 + jnp.tanh(math.sqrt(2.0 / math.pi) * (x + 0.044715 * (x * x * x))))


def _compress_kernel(zk_ref, zv_ref, pek_ref, pev_ref, wak_ref, wbk_ref, wav_ref, wbv_ref, w2k_ref, w2vT_ref, kf_ref,
                     kc_ref, vcT_ref, *, nch):
    def hidden(z_ref, pe_ref, wa_ref, wb_ref):
        z = z_ref[0].astype(F32)
        za = (z + pe_ref[0:1]).astype(BF16)
        zb = (z + pe_ref[1:2]).astype(BF16)
        a = jnp.dot(za, wa_ref[...], preferred_element_type=F32)
        bm = jnp.dot(zb, wb_ref[...], preferred_element_type=F32)
        return _gelu_tanh(a + pltpu.roll(bm, nch - 1, 0)).astype(BF16)

    hk = hidden(zk_ref, pek_ref, wak_ref, wbk_ref)
    kc = jnp.dot(hk, w2k_ref[...], preferred_element_type=F32)
    kf = kf_ref[...]
    hv = hidden(zv_ref, pev_ref, wav_ref, wbv_ref)
    vcT = _nt(w2vT_ref[...], hv)
    ones = _ones_rows(nch)
    for g in range(NSA_GROUPS):
        gs = slice(g * NSA_DH, (g + 1) * NSA_DH)
        kc_ref[0, g] = jnp.concatenate([kc[:, gs], kf], axis=1).astype(BF16)
        vcT_ref[0, g] = jnp.concatenate([vcT[gs], ones], axis=0).astype(BF16)


def _compress(zk, zv, pek, pev, wak, wbk, wav, wbv, w2k, w2vT, kf):
    b, nch, cw = zk.shape
    kern = functools.partial(_compress_kernel, nch=nch)
    zspec = pl.BlockSpec((1, nch, cw), lambda i: (i, 0, 0))
    consts = (pek, pev, wak, wbk, wav, wbv, w2k, w2vT, kf)
    return pl.pallas_call(
        kern,
        grid=(b,),
        in_specs=[zspec, zspec] + [_const_spec(a.shape) for a in consts],
        out_specs=[pl.BlockSpec((1, NSA_GROUPS, nch, NSA_DH + KFEAT), lambda i: (i, 0, 0, 0)),
                   pl.BlockSpec((1, NSA_GROUPS, V_ROWS, nch), lambda i: (i, 0, 0, 0))],
        out_shape=[jax.ShapeDtypeStruct((b, NSA_GROUPS, nch, NSA_DH + KFEAT), BF16),
                   jax.ShapeDtypeStruct((b, NSA_GROUPS, V_ROWS, nch), BF16)],
        compiler_params=_cparams(("arbitrary",)),
        name="compress",
    )(zk, zv, *consts)


def _split3(x):
    hi = x.astype(BF16)
    r = x - hi.astype(F32)
    mid = r.astype(BF16)
    lo = (r - mid.astype(F32)).astype(BF16)
    return hi, mid, lo


def _feature_rows(tab_ref, head, const, t):
    row = lax.broadcasted_iota(jnp.int32, (KFEAT, t), 0)
    r3 = row - 3 * (row >= 3).astype(jnp.int32) - 3 * (row >= 6).astype(jnp.int32)
    s_hi, s_mid, s_lo, slope = (tab_ref[head * 4 + k] for k in range(4))
    c_hi, c_mid, c_lo = (x.astype(F32) for x in _split3(slope * const))
    svals = jnp.where(r3 == 0, s_hi, jnp.where(r3 == 1, s_mid, s_lo))
    cvals = jnp.where(r3 == 0, c_hi, jnp.where(r3 == 1, c_mid, c_lo))
    return jnp.where(row < 6, svals, jnp.where(row < 9, cvals, 0.0)).astype(BF16)


def _cmp_topk_kernel(tab_ref, qT_ref, kc_ref, vcT_ref, pool_ref, gT_ref, ocT_ref, bias_ref, act_ref,
                     qa_ref, e_ref, acc_ref, imp_ref, rank_ref, ps_ref, *, t, gb, nch, nslc, nsel, ov_terms):
    gp = pl.program_id(1)
    i = pl.program_id(2)
    q0 = i * t
    chains = range(gb * NSA_REP)
    bpt = t // SLC_LEN

    const = (CMP_LEN - 1) / 2.0 - jnp.full((KFEAT, t), q0, jnp.int32).astype(F32)
    for c in chains:
        head = (gp * gb + c // NSA_REP) * NSA_REP + c % NSA_REP
        qa_ref[c, 0:NSA_DH, :] = qT_ref[0, c * NSA_DH:(c + 1) * NSA_DH, :]
        qa_ref[c, NSA_DH:NSA_DH + KFEAT, :] = _feature_rows(tab_ref, head, const, t)

    n_i = lax.broadcasted_iota(jnp.int32, (nch, t), 0)
    t_i = q0 + lax.broadcasted_iota(jnp.int32, (nch, t), 1)
    ended = jnp.where(n_i * CMP_STRIDE + (CMP_LEN - 1) <= t_i, 0.0, NEG)
    def cmp_scores(c):
        s = jnp.dot(kc_ref[0, c // NSA_REP], qa_ref[c], preferred_element_type=F32) + ended
        return _stage_scores(s, e_ref.at[c])

    def cmp_softmax_pv(c, mt):
        m_eff = jnp.where(mt > 0.5 * NEG, mt, 0.0)
        e = jnp.exp2(e_ref[c] - m_eff)
        e_ref[c] = e
        acc_ref[c] = jnp.dot(vcT_ref[0, c // NSA_REP], e.astype(BF16), preferred_element_type=F32)

    mt = [cmp_scores(c) for c in chains]
    for c in chains:
        cmp_softmax_pv(c, mt[c])

    j_i = lax.broadcasted_iota(jnp.int32, (nslc, t), 0)
    cur = jnp.right_shift(q0 + lax.broadcasted_iota(jnp.int32, (nslc, t), 1), SLC_SHIFT)
    forced = (j_i == 0) | (j_i == cur) | (j_i == cur - 1)
    per_blk = SLC_LEN // CMP_STRIDE
    for gl in range(gb):
        psum = jnp.zeros((nch, t), F32)
        for r in range(NSA_REP):
            c = gl * NSA_REP + r
            acc = acc_ref[c]
            l = acc[NSA_DH:NSA_DH + 1]
            inv = 1.0 / jnp.where(l > 0.0, l, 1.0)
            gate = gT_ref[0, gl * GATE_ROWS + r:gl * GATE_ROWS + r + 1, :]
            ocT_ref[0, c * NSA_DH:(c + 1) * NSA_DH, :] = (gate * (acc[:NSA_DH] * inv)).astype(BF16)
            psum = psum + e_ref[c] * inv
        for h in range(t // LANES):
            ps_ref[h] = psum[:, h * LANES:(h + 1) * LANES]
        imp = jnp.zeros((nslc, t), F32)
        for k, wgt in ov_terms:
            rows = jnp.concatenate([ps_ref[h, pl.ds(k % per_blk, nslc, stride=per_blk), :]
                                    for h in range(t // LANES)], axis=1)
            if k < 0:
                rows = jnp.where(j_i == 0, 0.0, pltpu.roll(rows, 1, 0))
            imp = imp + wgt * rows
        imp_ref[gl] = jnp.where(j_i > cur, NEG, imp + jnp.where(forced, FORCE_BONUS, 0.0))
        rank_ref[gl] = jnp.zeros((nslc, t), F32)

    for chunk in range(nslc // bpt):
        @pl.when(chunk <= i)
        def _():
            for gl in range(gb):
                imp = imp_ref[gl]
                rank = rank_ref[gl]
                for jp in range(chunk * bpt, (chunk + 1) * bpt):
                    row = imp[jp:jp + 1, :]
                    incs = []
                    for rg in range(nslc // SUBLANES):
                        r0 = SUBLANES * rg
                        blk = imp[r0:r0 + SUBLANES]
                        if r0 > jp:
                            incs.append(jnp.where(row >= blk, 1.0, 0.0))
                        elif r0 + SUBLANES - 1 <= jp:
                            incs.append(jnp.where(row > blk, 1.0, 0.0))
                        else:
                            above = lax.broadcasted_iota(jnp.int32, (SUBLANES, t), 0) > jp - r0
                            incs.append(jnp.where(above, jnp.where(row >= blk, 1.0, 0.0),
                                                  jnp.where(row > blk, 1.0, 0.0)))
                    rank = rank + jnp.concatenate(incs, axis=0)
                rank_ref[gl] = rank

    for gl in range(gb):
        bias = jnp.where(j_i > cur, NEG, jnp.where(rank_ref[gl] < nsel, 0.0, NEG))
        bias_ref[0, gl] = bias
        hit = jnp.where(jnp.max(bias, axis=1, keepdims=True) == 0.0, 1.0, 0.0)
        tiles = jnp.sum(hit * pool_ref[...], axis=0, keepdims=True)
        act_ref[0, gl, 0] = jnp.broadcast_to(jnp.where(tiles > 0.0, 1, 0), (SUBLANES, LANES)).astype(jnp.int32)


def _overlap_terms():
    per_blk = SLC_LEN // CMP_STRIDE
    terms = []
    for k in range(-(CMP_LEN // CMP_STRIDE) + 1, per_blk):
        lo, hi = max(k * CMP_STRIDE, 0), min(k * CMP_STRIDE + CMP_LEN, SLC_LEN)
        if hi > lo:
            terms.append((k, (hi - lo) / CMP_LEN))
    assert SLC_LEN % CMP_STRIDE == 0 and all(k >= -per_blk for k, _ in terms)
    return tuple(terms)


def _cmp_topk(tab, qT, kc, vcT, gT, nsel):
    b, _, s = qT.shape
    t = ATTN_TILE
    assert s % t == 0 and t % SLC_LEN == 0
    nch = kc.shape[2]
    nslc = s // SLC_LEN
    gb = CMP_GROUPS_PER_STEP
    nc = gb * NSA_REP
    kern = functools.partial(_cmp_topk_kernel, t=t, gb=gb, nch=nch, nslc=nslc, nsel=nsel,
                             ov_terms=_overlap_terms())
    gw = nc * NSA_DH
    assert s // t <= LANES
    pool = np.zeros((nslc, LANES), np.float32)
    pool[np.arange(nslc), np.arange(nslc) // (t // SLC_LEN)] = 1.0
    pool = jnp.asarray(pool, F32)
    return pl.pallas_call(
        kern,
        grid=(b, NSA_GROUPS // gb, s // t),
        in_specs=[
            pl.BlockSpec(memory_space=pltpu.SMEM),
            pl.BlockSpec((1, gw, t), lambda bi, g, i: (bi, g, i)),
            pl.BlockSpec((1, gb, nch, NSA_DH + KFEAT), lambda bi, g, i: (bi, g, 0, 0)),
            pl.BlockSpec((1, gb, V_ROWS, nch), lambda bi, g, i: (bi, g, 0, 0)),
            _const_spec(pool.shape),
            pl.BlockSpec((1, gb * GATE_ROWS, t), lambda bi, g, i: (bi, g, i)),
        ],
        out_specs=[pl.BlockSpec((1, gw, t), lambda bi, g, i: (bi, g, i)),
                   pl.BlockSpec((1, gb, nslc, t), lambda bi, g, i: (bi, g, 0, i)),
                   pl.BlockSpec((1, gb, 1, SUBLANES, LANES), lambda bi, g, i: (bi, g, i, 0, 0))],
        out_shape=[jax.ShapeDtypeStruct((b, C_MIX, s), BF16),
                   jax.ShapeDtypeStruct((b, NSA_GROUPS, nslc, s), F32),
                   jax.ShapeDtypeStruct((b, NSA_GROUPS, s // t, SUBLANES, LANES), jnp.int32)],
        scratch_shapes=[pltpu.VMEM((nc, NSA_DH + KFEAT, t), BF16),
                        pltpu.VMEM((nc, nch, t), F32),
                        pltpu.VMEM((nc, V_ROWS, t), F32),
                        pltpu.VMEM((gb, nslc, t), F32), pltpu.VMEM((gb, nslc, t), F32),
                        pltpu.VMEM((t // LANES, nch, LANES), F32)],
        compiler_params=_cparams(("arbitrary", "arbitrary", "arbitrary")),
        name="cmp_topk",
    )(tab, qT, kc, vcT, pool, gT)


def _slc_win_kernel(act_ref, tab_ref, qT_ref, ks_ref, vsT_ref, kw_ref, vwT_ref, bias_ref, ocT_ref, gT_ref, yT_ref,
                    qa_ref, s_ref, acc_ref, sw_ref, accw_ref, idx_ref, *, t, gb, nslc, nkv):
    gp = pl.program_id(1)
    i = pl.program_id(2)
    chains = range(gb * NSA_REP)
    feat0 = NSA_DH + nslc

    const = -jnp.full((KFEAT, t), i * t, jnp.int32).astype(F32)
    for c in chains:
        gl, r = divmod(c, NSA_REP)
        head = (gp * gb + gl) * NSA_REP + r
        qa_ref[c, 0:NSA_DH, :] = qT_ref[0, c * NSA_DH:(c + 1) * NSA_DH, :]
        qa_ref[c, NSA_DH:feat0, :] = bias_ref[0, gl].astype(BF16)
        qa_ref[c, feat0:feat0 + KFEAT, :] = _feature_rows(tab_ref, head, const, t)

    def scores_from(k_ref):
        def scores(c, j):
            k = k_ref[0, c // NSA_REP, pl.ds(pl.multiple_of(j * t, t), t), :]
            return jnp.dot(k, qa_ref[c], preferred_element_type=F32)
        return scores

    def vtile_from(vT_ref):
        def vtile(c, j):
            gl = c // NSA_REP
            return vT_ref[0, j, gl * V_ROWS:(gl + 1) * V_ROWS, :]
        return vtile

    kpos, qpos = _tile_pos(t, 0, 0)
    causal = jnp.where(kpos <= qpos, 0.0, NEG)

    w_scores = scores_from(kw_ref)
    w_vtile = vtile_from(vwT_ref)
    w_tiles = (i, jnp.maximum(i - 1, 0), jnp.maximum(i - 2, 0))
    in_seq1 = jnp.where((i - 1) * t + kpos >= 0, 0.0, NEG)
    in_seq2 = jnp.where((i - 2) * t + kpos >= 0, 0.0, NEG)
    w_extra = (causal, in_seq1, jnp.where(qpos < kpos, in_seq2, NEG))
    m = [jnp.full((1, t), NEG, F32) for _ in chains]
    mt = []
    for c in chains:
        accw_ref[c] = jnp.zeros((V_ROWS, t), F32)
        mt.append(_stage_scores(w_scores(c, w_tiles[0]) + w_extra[0], sw_ref.at[c]))
    for w in range(3):
        for c in chains:
            p, m[c], alpha = _stage_softmax(sw_ref.at[c], m[c], mt[c])
            if w + 1 < 3:
                mt[c] = _stage_scores(w_scores(c, w_tiles[w + 1]) + w_extra[w + 1], sw_ref.at[c])
            _stage_pv(accw_ref.at[c], w_vtile(c, w_tiles[w]), p, alpha)

    stride = nkv + 1
    counts, spares = [], []
    for gl in range(gb):
        g_abs = gp * gb + gl
        base = ((pl.program_id(0) * NSA_GROUPS + g_abs) * pl.num_programs(2) + i) * nkv
        n_act, spare = jnp.int32(0), jnp.int32(0)
        for j in range(nkv):
            idx_ref[gl * stride + n_act] = j
            keep = (j < i) & (act_ref[base + j] != 0)
            n_act = n_act + jnp.where(keep, 1, 0)
            spare = jnp.where((j < i) & jnp.logical_not(keep), j, spare)
        counts.append(n_act)
        spares.append(spare)
    n_rest = functools.reduce(jnp.maximum, counts)
    for gl in range(gb):
        for k in range(nkv):
            idx_ref[gl * stride + k] = jnp.where(k >= counts[gl], spares[gl], idx_ref[gl * stride + k])
    tile_fns = [functools.partial(lambda gl, n: idx_ref[gl * stride + n], gl) for gl in range(gb)]
    _causal_flash(i, n_rest, tile_fns, lambda c: c // NSA_REP, t, chains, scores_from(ks_ref),
                  vtile_from(vsT_ref), causal, s_ref, acc_ref)

    for c in chains:
        gl, r = divmod(c, NSA_REP)
        g0 = gl * GATE_ROWS
        rs = slice(c * NSA_DH, (c + 1) * NSA_DH)
        y = (ocT_ref[0, rs, :].astype(F32)
             + gT_ref[0, g0 + NSA_REP + r:g0 + NSA_REP + r + 1, :] * _normalized(acc_ref.at[c])
             + gT_ref[0, g0 + 2 * NSA_REP + r:g0 + 2 * NSA_REP + r + 1, :] * _normalized(accw_ref.at[c]))
        yT_ref[0, rs, :] = y.astype(BF16)


def _slc_win(act, tab, qT, ks, vsT, kw, vwT, bias, ocT, gT):
    b, _, s = qT.shape
    t = ATTN_TILE
    assert s % t == 0 and WINDOW == 2 * t
    nkv = s // t
    nslc = bias.shape[2]
    kaug = ks.shape[3]
    gb = NSA_GROUPS_PER_STEP
    nc = gb * NSA_REP
    gw = nc * NSA_DH
    kern = functools.partial(_slc_win_kernel, t=t, gb=gb, nslc=nslc, nkv=nkv)
    qspec = pl.BlockSpec((1, gw, t), lambda bi, g, i, *_: (bi, g, i))
    kspec = pl.BlockSpec((1, gb, s, kaug), lambda bi, g, i, *_: (bi, g, 0, 0))
    vspec = pl.BlockSpec((1, nkv, gb * V_ROWS, t), lambda bi, g, i, *_: (bi, 0, g, 0))
    grid_spec = pltpu.PrefetchScalarGridSpec(
        num_scalar_prefetch=2,
        grid=(b, NSA_GROUPS // gb, s // t),
        in_specs=[
            qspec, kspec, vspec, kspec, vspec,
            pl.BlockSpec((1, gb, nslc, t), lambda bi, g, i, *_: (bi, g, 0, i)),
            qspec,
            pl.BlockSpec((1, gb * GATE_ROWS, t), lambda bi, g, i, *_: (bi, g, i)),
        ],
        out_specs=qspec,
        scratch_shapes=[pltpu.VMEM((nc, kaug, t), BF16),
                        pltpu.VMEM((nc, t, t), F32), pltpu.VMEM((nc, V_ROWS, t), F32),
                        pltpu.VMEM((nc, t, t), F32), pltpu.VMEM((nc, V_ROWS, t), F32),
                        pltpu.SMEM((gb * (nkv + 1),), jnp.int32)],
    )
    return pl.pallas_call(
        kern,
        grid_spec=grid_spec,
        out_shape=jax.ShapeDtypeStruct((b, C_MIX, s), BF16),
        compiler_params=_cparams(("arbitrary", "arbitrary", "arbitrary")),
        name="slc_win",
    )(act, tab, qT, ks, vsT, kw, vwT, bias, ocT, gT)


def _row(v):
    return v.reshape(1, -1).astype(F32)


def _layer_a(x, g_pre, w_in, conv_w, q_norm, w_q_up, kv_norm, w_kv_up, w_out, g_post, g_mlp_pre, w1, w2, g_mlp_post):
    b, s, d = x.shape
    o3 = 3 * CONV_WIDTH
    o5 = o3 + MLA_Q_RANK + MLA_KV_RANK
    pad = MLA_SLOT - MLA_NOPE - MLA_ROPE
    w_main = w_in[:, :o5].astype(BF16)
    wkr = jnp.pad(w_in[:, o5:].T, ((MLA_NOPE, pad), (0, 0))).astype(BF16)
    wq = w_q_up.reshape(MLA_Q_RANK, MLA_HEADS, MLA_NOPE + MLA_ROPE)
    wq = jnp.pad(wq, ((0, 0), (0, 0), (0, pad))).reshape(MLA_Q_RANK, MLA_HEADS * MLA_SLOT).T.astype(BF16)
    wkv = w_kv_up.reshape(MLA_KV_RANK, MLA_HEADS, MLA_NOPE + MLA_V)
    wk = jnp.pad(wkv[:, :, :MLA_NOPE], ((0, 0), (0, 0), (0, MLA_SLOT - MLA_NOPE)))
    wk = wk.reshape(MLA_KV_RANK, MLA_HEADS * MLA_SLOT).astype(BF16)
    wv = wkv[:, :, MLA_NOPE:].reshape(MLA_KV_RANK, MLA_HEADS * MLA_V).T.astype(BF16)

    inv = ROPE_THETA ** (-jnp.arange(ROPE_HALF, dtype=F32) / ROPE_HALF)
    ang = inv[:, None] * jnp.arange(s, dtype=F32)[None, :]
    cos, sin = jnp.cos(ang), jnp.sin(ang)

    yconv, qT, k, vT = _a_pre(x, _row(g_pre), w_main, wkr, conv_w.astype(F32), _row(q_norm), wq, _row(kv_norm),
                              wk, wv, cos, sin)
    yT = _mla_attn(qT, k, vT)
    wo = w_out.astype(BF16)
    return _post(x, yconv, yT, wo[:CONV_WIDTH], wo[CONV_WIDTH:], _row(g_post), _row(g_mlp_pre),
                 w1.astype(BF16), w2.astype(BF16), _row(g_mlp_post))


def _layer_c(x, g_pre, w_in, pe_k, w1_k, w2_k, pe_v, w1_v, w2_v, w_out, g_post, g_mlp_pre, w1, w2, g_mlp_post):
    b, s, d = x.shape
    G, R, Dh = NSA_GROUPS, NSA_REP, NSA_DH
    wq = w_in[:, :C_MIX].T.astype(BF16)
    kv = w_in[:, C_MIX:C_MIX + 6 * KV_W].reshape(d, 6, KV_W)
    wtok = jnp.concatenate([kv[:, 0], kv[:, 1], kv[:, 2], kv[:, 4]], axis=1).astype(BF16)
    wg = w_in[:, C_MIX + 6 * KV_W:].reshape(d, 3, G, R).transpose(0, 2, 1, 3)
    wg = jnp.pad(wg.reshape(d, G, 3 * R), ((0, 0), (0, 0), (0, GATE_ROWS - 3 * R))).reshape(d, G * GATE_ROWS)
    wf = jnp.concatenate([kv[:, 3], kv[:, 5], wg], axis=1).T.astype(BF16)

    n_slc = s // SLC_LEN
    pos = np.arange(s)
    blk, within = pos // SLC_LEN, pos % SLC_LEN
    feats = np.zeros((s, KFEAT), np.float32)
    feats[:, 0:3] = (blk * SLC_LEN)[:, None]
    feats[:, 3:6] = within[:, None]
    feats[:, 6:9] = 1.0
    onehot = (blk[:, None] == np.arange(n_slc)[None, :]).astype(np.float32)
    kfs = jnp.asarray(np.concatenate([onehot, feats], axis=1), F32)
    kfw = jnp.asarray(np.concatenate([np.zeros_like(onehot), feats], axis=1), F32)

    qT, kc_z, vc_z, ks, kw, vsT, vwT, gT = _c_pre(x, _row(g_pre), wq, wtok, wf, kfs, kfw)

    nch = s // CMP_STRIDE
    half = CMP_LEN // 2
    eye = jnp.eye(G, dtype=F32)

    def w1_halves(w):
        bd = jnp.einsum('hlde,gk->hlgdke', w.reshape(2, half, Dh, Dh), eye)
        bd = bd.reshape(2, half * KV_W, KV_W).astype(BF16)
        return bd[0], bd[1]

    def pe_halves(pe):
        return jnp.tile(pe.reshape(2, half, 1, Dh), (1, 1, G, 1)).reshape(2, half * KV_W).astype(F32)

    def w2_bd(w):
        return jnp.einsum('de,gk->gdke', w, eye).reshape(KV_W, KV_W)

    n_idx = np.arange(nch)
    cfeat = np.zeros((nch, KFEAT), np.float32)
    cfeat[:, 0:3] = (1024 * (n_idx // 64))[:, None]
    cfeat[:, 3:6] = (CMP_STRIDE * (n_idx % 64))[:, None]
    cfeat[:, 6:9] = 1.0
    kc, vcT = _compress(kc_z, vc_z, pe_halves(pe_k), pe_halves(pe_v), *w1_halves(w1_k), *w1_halves(w1_v),
                        w2_bd(w2_k).astype(BF16), w2_bd(w2_v).T.astype(BF16), jnp.asarray(cfeat, F32))

    slopes_np = (2.0 ** (-8.0 * np.arange(1, NSA_HEADS + 1) / NSA_HEADS)).astype(np.float32)
    full = (slopes_np * np.float32(LOG2E)).astype(np.float32)
    pieces, rest = [], full.copy()
    for _ in range(3):
        piece = rest.astype(BF16).astype(np.float32)
        pieces.append(piece)
        rest = (rest - piece).astype(np.float32)
    tab = jnp.asarray(np.stack(pieces + [full], axis=1).reshape(-1), F32)

    ocT, bias, act = _cmp_topk(tab, qT, kc, vcT, gT, min(N_SEL, n_slc))
    act = act[:, :, :, 0, :s // ATTN_TILE].reshape(-1)
    yT = _slc_win(act, tab, qT, ks, vsT, kw, vwT, bias, ocT, gT)
    return _post(x, None, yT, None, w_out.astype(BF16), _row(g_post), _row(g_mlp_pre),
                 w1.astype(BF16), w2.astype(BF16), _row(g_mlp_post))


def kernel(x, norm_mix_pre, norm_mix_post, norm_mlp_pre, norm_mlp_post, mlp_w1, mlp_w2, a_w_in, a_conv_w, a_q_norm,
           a_w_q_up, a_kv_norm, a_w_kv_up, a_w_out, c_w_in, c_cmp_pe_k, c_cmp_w1_k, c_cmp_w2_k, c_cmp_pe_v,
           c_cmp_w1_v, c_cmp_w2_v, c_w_out):
    depth = norm_mix_pre.shape[0]
    for layer in range(depth):
        i = layer // 2
        common = (norm_mix_post[layer], norm_mlp_pre[layer], mlp_w1[layer], mlp_w2[layer], norm_mlp_post[layer])
        if layer % 2 == 0:
            x = _layer_a(x, norm_mix_pre[layer], a_w_in[i], a_conv_w[i], a_q_norm[i], a_w_q_up[i], a_kv_norm[i],
                         a_w_kv_up[i], a_w_out[i], *common)
        else:
            x = _layer_c(x, norm_mix_pre[layer], c_w_in[i], c_cmp_pe_k[i], c_cmp_w1_k[i], c_cmp_w2_k[i],
                         c_cmp_pe_v[i], c_cmp_w1_v[i], c_cmp_w2_v[i], c_w_out[i], *common)
    return x
```

```python
import functools
import math

import numpy as np
import jax
import jax.numpy as jnp
from jax import lax
from jax.experimental import pallas as pl
from jax.experimental.pallas import tpu as pltpu

F32 = jnp.float32
BF16 = jnp.bfloat16

D_MODEL = 1024
D_FF = 4 * D_MODEL
EPS = 1e-6
NEG = -1e30
LOG2E = math.log2(math.e)

CONV_WIDTH = 512
CONV_K = 3
MLA_HEADS = 8
MLA_NOPE = 64
MLA_ROPE = 32
MLA_V = 64
MLA_KV_RANK = 256
MLA_Q_RANK = 768
MLA_SLOT = 128
ROPE_THETA = 10000.0
ROPE_HALF = MLA_ROPE // 2

NSA_HEADS = 16
NSA_GROUPS = 4
NSA_REP = 4
NSA_DH = 64
CMP_LEN = 32
CMP_STRIDE = 16
SLC_LEN = 64
SLC_SHIFT = 6
N_SEL = 16
WINDOW = 512
FORCE_BONUS = 1e4
KV_W = NSA_GROUPS * NSA_DH
C_MIX = NSA_HEADS * NSA_DH
GATE_ROWS = 16
ONES_ROWS = 16
V_ROWS = 64 + ONES_ROWS
LANES = 128
SUBLANES = 8
BF16_ROWS = 16
KFEAT = 16

TOKEN_TILE = 512
POST_TILE = 512
ATTN_TILE = 256
MLA_HEADS_PER_STEP = 8
NSA_GROUPS_PER_STEP = 4
CMP_GROUPS_PER_STEP = 4
FF_CHUNK = 1024
VMEM_LIMIT = 60 * 1024 * 1024


def _cparams(sem):
    return pltpu.CompilerParams(dimension_semantics=sem, vmem_limit_bytes=VMEM_LIMIT)


def _const_spec(shape, single_buffer=False):
    nd = len(shape)
    if single_buffer:
        return pl.BlockSpec(shape, lambda *_: (0,) * nd, pipeline_mode=pl.Buffered(1))
    return pl.BlockSpec(shape, lambda *_: (0,) * nd)


def _rms(x, g):
    ms = jnp.mean(x * x, axis=-1, keepdims=True)
    return x * lax.rsqrt(ms + EPS) * g


def _nt(a, b):
    return lax.dot_general(a, b, (((1,), (1,)), ((), ())), preferred_element_type=F32)


def _tn(a, b):
    return lax.dot_general(a, b, (((0,), (0,)), ((), ())), preferred_element_type=F32)


def _ones_rows(n):
    return jnp.where(lax.broadcasted_iota(jnp.int32, (ONES_ROWS, n), 0) == 0, 1.0, 0.0).astype(F32)


def _rope_rows(r1, r2, cos, sin):
    return r1 * cos - r2 * sin, r2 * cos + r1 * sin


def _a_pre_kernel(x_ref, g_ref, w1_ref, wkr_ref, convw_ref, qn_ref, wq_ref, kvn_ref, wk_ref, wv_ref,
                  cos_ref, sin_ref, yconv_ref, qT_ref, k_ref, vT_ref, carry_ref, *, tm, q_scale):
    @pl.when(pl.program_id(1) == 0)
    def _():
        carry_ref[...] = jnp.zeros_like(carry_ref)

    sub = min(ATTN_TILE, tm)
    w = convw_ref[...]
    w0, w1, w2 = w[0:1], w[1:2], w[2:3]
    hd = BF16_ROWS
    tail = carry_ref[...]
    z32 = jnp.zeros((MLA_SLOT - MLA_NOPE - MLA_ROPE, sub), F32)
    ones = _ones_rows(sub)
    o1, o2, o3 = CONV_WIDTH, 2 * CONV_WIDTH, 3 * CONV_WIDTH
    o4 = o3 + MLA_Q_RANK
    for cidx in range(tm // sub):
        rows = slice(cidx * sub, (cidx + 1) * sub)
        xn = _rms(x_ref[0, rows, :], g_ref[...]).astype(BF16)
        p1 = jnp.dot(xn, w1_ref[...], preferred_element_type=F32)
        g_b, g_c, hv = p1[:, :o1], p1[:, o1:o2], p1[:, o2:o3]
        c_q, c_kv = p1[:, o3:o4], p1[:, o4:]

        u = g_c * hv
        y = w2 * u + w1 * pltpu.roll(u, 1, 0) + w0 * pltpu.roll(u, 2, 0)
        uh = u[0:hd]
        row = lax.broadcasted_iota(jnp.int32, (hd, CONV_WIDTH), 0)
        h1 = jnp.where(row < 1, pltpu.roll(tail, 1, 0), pltpu.roll(uh, 1, 0))
        h2 = jnp.where(row < 2, pltpu.roll(tail, 2, 0), pltpu.roll(uh, 2, 0))
        yh = w2 * uh + w1 * h1 + w0 * h2
        yconv_ref[0, cidx * sub + hd:(cidx + 1) * sub, :] = (g_b[hd:] * y[hd:]).astype(BF16)
        yconv_ref[0, cidx * sub:cidx * sub + hd, :] = (g_b[0:hd] * yh).astype(BF16)
        tail = u[sub - hd:sub]

        cos = cos_ref[:, rows]
        sin = sin_ref[:, rows]

        hq = _rms(c_q, qn_ref[...]).astype(BF16)
        qT = _nt(wq_ref[...], hq)
        for h in range(MLA_HEADS):
            b0 = MLA_SLOT * h
            r1 = qT[b0 + MLA_NOPE:b0 + MLA_NOPE + ROPE_HALF]
            r2 = qT[b0 + MLA_NOPE + ROPE_HALF:b0 + MLA_NOPE + MLA_ROPE]
            e1, e2 = _rope_rows(r1, r2, cos, sin)
            slot = jnp.concatenate([qT[b0:b0 + MLA_NOPE], e1, e2, z32], axis=0) * q_scale
            qT_ref[0, b0:b0 + MLA_SLOT, rows] = slot.astype(BF16)

        krT = _nt(wkr_ref[...], xn)
        r1 = krT[MLA_NOPE:MLA_NOPE + ROPE_HALF]
        r2 = krT[MLA_NOPE + ROPE_HALF:MLA_NOPE + MLA_ROPE]
        e1, e2 = _rope_rows(r1, r2, cos, sin)
        slotT = jnp.concatenate([jnp.zeros((MLA_NOPE, sub), F32), e1, e2, z32], axis=0)
        slot = slotT.T

        hkv = _rms(c_kv, kvn_ref[...]).astype(BF16)
        kn = jnp.dot(hkv, wk_ref[...], preferred_element_type=F32)
        k = kn + jnp.concatenate([slot] * MLA_HEADS, axis=1)
        k_ref[0, rows, :] = k.astype(BF16)
        vT = _nt(wv_ref[...], hkv)
        vT = jnp.concatenate([x for h in range(MLA_HEADS) for x in (vT[h * MLA_V:(h + 1) * MLA_V], ones)], axis=0)
        vT_ref[0, cidx] = vT.astype(BF16)
    carry_ref[...] = tail


def _a_pre(x, g, w1, wkr, convw, qn, wq, kvn, wk, wv, cos, sin):
    b, s, d = x.shape
    tm = min(TOKEN_TILE, s)
    hw = MLA_HEADS * MLA_SLOT
    vw = MLA_HEADS * V_ROWS
    q_scale = (MLA_NOPE + MLA_ROPE) ** -0.5 * LOG2E
    kern = functools.partial(_a_pre_kernel, tm=tm, q_scale=q_scale)
    return pl.pallas_call(
        kern,
        grid=(b, s // tm),
        in_specs=[
            pl.BlockSpec((1, tm, d), lambda i, j: (i, j, 0)),
            _const_spec(g.shape), _const_spec(w1.shape), _const_spec(wkr.shape), _const_spec(convw.shape),
            _const_spec(qn.shape), _const_spec(wq.shape), _const_spec(kvn.shape), _const_spec(wk.shape),
            _const_spec(wv.shape),
            pl.BlockSpec((ROPE_HALF, tm), lambda i, j: (0, j)),
            pl.BlockSpec((ROPE_HALF, tm), lambda i, j: (0, j)),
        ],
        out_specs=[
            pl.BlockSpec((1, tm, CONV_WIDTH), lambda i, j: (i, j, 0)),
            pl.BlockSpec((1, hw, tm), lambda i, j: (i, 0, j)),
            pl.BlockSpec((1, tm, hw), lambda i, j: (i, j, 0)),
            pl.BlockSpec((1, tm // ATTN_TILE, vw, ATTN_TILE), lambda i, j: (i, j, 0, 0)),
        ],
        out_shape=[
            jax.ShapeDtypeStruct((b, s, CONV_WIDTH), BF16),
            jax.ShapeDtypeStruct((b, hw, s), BF16),
            jax.ShapeDtypeStruct((b, s, hw), BF16),
            jax.ShapeDtypeStruct((b, s // ATTN_TILE, vw, ATTN_TILE), BF16),
        ],
        scratch_shapes=[pltpu.VMEM((16, CONV_WIDTH), F32)],
        compiler_params=_cparams(("arbitrary", "arbitrary")),
        name="a_pre",
    )(x, g, w1, wkr, convw, qn, wq, kvn, wk, wv, cos, sin)


def _tile_pos(t, k0, q0):
    kpos = k0 + lax.broadcasted_iota(jnp.int32, (t, t), 0)
    qpos = q0 + lax.broadcasted_iota(jnp.int32, (t, t), 1)
    return kpos, qpos


def _stage_scores(s, s_slot):
    s_slot[...] = s
    return jnp.max(s, axis=0, keepdims=True)


def _stage_softmax(s_slot, m, mt):
    m_new = jnp.maximum(m, mt)
    return jnp.exp2(s_slot[...] - m_new).astype(BF16), m_new, jnp.exp2(m - m_new)


def _stage_pv(acc_slot, vT, p, alpha):
    acc_slot[...] = alpha * acc_slot[...] + jnp.dot(vT, p, preferred_element_type=F32)


def _causal_flash(i, n_rest, tile_fns, list_of, t, chains, scores, vtile, diag_extra, s_ref, acc_ref):
    mt0 = []
    for c in chains:
        mt0.append(_stage_scores(scores(c, i) + diag_extra, s_ref.at[c]))
        acc_ref[c] = jnp.zeros((V_ROWS, t), F32)
    init = (tuple(mt0), tuple(jnp.full((1, t), NEG, F32) for _ in chains))
    last = jnp.maximum(n_rest - 1, 0)

    def body(n, carry):
        mt, m = carry
        j_cur = [jnp.where(n == 0, i, f(jnp.maximum(n - 1, 0))) for f in tile_fns]
        j_next = [f(jnp.minimum(n, last)) for f in tile_fns]
        m_new, mt_next = [], []
        for c in chains:
            p, mn, alpha = _stage_softmax(s_ref.at[c], m[c], mt[c])
            m_new.append(mn)
            mt_next.append(_stage_scores(scores(c, j_next[list_of(c)]), s_ref.at[c]))
            _stage_pv(acc_ref.at[c], vtile(c, j_cur[list_of(c)]), p, alpha)
        return tuple(mt_next), tuple(m_new)

    n_iter = n_rest + 1
    quads = lax.shift_right_logical(n_iter, 2)

    def quad(k, cr):
        for u in range(4):
            cr = body(4 * k + u, cr)
        return cr

    carry = lax.fori_loop(0, quads, quad, init)
    lax.fori_loop(4 * quads, n_iter, body, carry)


def _normalized(acc_slot):
    acc = acc_slot[...]
    return acc[:V_ROWS - ONES_ROWS] / acc[V_ROWS - ONES_ROWS:V_ROWS - ONES_ROWS + 1]


def _mla_attn_kernel(qT_ref, k_ref, vT_ref, oT_ref, s_ref, acc_ref, *, t, hb):
    i = pl.program_id(2)

    def scores(h, j):
        k = k_ref[0, pl.ds(pl.multiple_of(j * t, t), t), h * MLA_SLOT:(h + 1) * MLA_SLOT]
        return jnp.dot(k, qT_ref[0, h * MLA_SLOT:(h + 1) * MLA_SLOT, :], preferred_element_type=F32)

    def vtile(h, j):
        return vT_ref[0, j, h * V_ROWS:(h + 1) * V_ROWS, :]

    kpos, qpos = _tile_pos(t, 0, 0)
    causal = jnp.where(kpos <= qpos, 0.0, NEG)
    _causal_flash(i, i, [lambda n: n], lambda h: 0, t, range(hb), scores, vtile, causal, s_ref, acc_ref)
    for h in range(hb):
        oT_ref[0, h * MLA_V:(h + 1) * MLA_V, :] = _normalized(acc_ref.at[h]).astype(BF16)


def _mla_attn(qT, k, vT):
    b, hw, s = qT.shape
    t = min(ATTN_TILE, s)
    nkv = s // t
    hb = MLA_HEADS_PER_STEP
    kern = functools.partial(_mla_attn_kernel, t=t, hb=hb)
    return pl.pallas_call(
        kern,
        grid=(b, MLA_HEADS // hb, s // t),
        in_specs=[
            pl.BlockSpec((1, hb * MLA_SLOT, t), lambda bi, h, i: (bi, h, i)),
            pl.BlockSpec((1, s, hb * MLA_SLOT), lambda bi, h, i: (bi, 0, h)),
            pl.BlockSpec((1, nkv, hb * V_ROWS, t), lambda bi, h, i: (bi, 0, h, 0)),
        ],
        out_specs=pl.BlockSpec((1, hb * MLA_V, t), lambda bi, h, i: (bi, h, i)),
        out_shape=jax.ShapeDtypeStruct((b, MLA_HEADS * MLA_V, s), BF16),
        scratch_shapes=[pltpu.VMEM((hb, t, t), F32), pltpu.VMEM((hb, V_ROWS, t), F32)],
        compiler_params=_cparams(("arbitrary", "arbitrary", "arbitrary")),
        name="mla_attn",
    )(qT, k, vT)


def _post_kernel(*refs, has_tok):
    if has_tok:
        x_ref, ytok_ref, yT_ref, wo_tok_ref, wo_ref, gpost_ref, gpre_ref, w1_ref, w2_ref, gmlp_ref, o_ref = refs
    else:
        x_ref, yT_ref, wo_ref, gpost_ref, gpre_ref, w1_ref, w2_ref, gmlp_ref, o_ref = refs
    x = x_ref[0]
    mix = _tn(yT_ref[0], wo_ref[...])
    if has_tok:
        mix = mix + jnp.dot(ytok_ref[0], wo_tok_ref[...], preferred_element_type=F32)
    x1 = x + _rms(mix, gpost_ref[...])
    h = _rms(x1, gpre_ref[...]).astype(BF16)
    acc = jnp.zeros_like(x1)
    for c in range(D_FF // FF_CHUNK):
        a = jnp.dot(h, w1_ref[:, c * FF_CHUNK:(c + 1) * FF_CHUNK], preferred_element_type=F32)
        a = jnp.maximum(a, 0.0)
        a = (a * a).astype(BF16)
        acc = acc + jnp.dot(a, w2_ref[c * FF_CHUNK:(c + 1) * FF_CHUNK, :], preferred_element_type=F32)
    o_ref[0] = x1 + _rms(acc, gmlp_ref[...])


def _post(x, ytok, yT, wo_tok, wo, gpost, gpre, w1, w2, gmlp):
    b, s, d = x.shape
    tm = min(POST_TILE, s)
    has_tok = ytok is not None
    kern = functools.partial(_post_kernel, has_tok=has_tok)
    xspec = pl.BlockSpec((1, tm, d), lambda i, j: (i, j, 0))
    ins, specs = [x], [xspec]
    if has_tok:
        ins.append(ytok)
        specs.append(pl.BlockSpec((1, tm, ytok.shape[-1]), lambda i, j: (i, j, 0)))
    ins.append(yT)
    specs.append(pl.BlockSpec((1, yT.shape[1], tm), lambda i, j: (i, 0, j)))
    if has_tok:
        ins.append(wo_tok)
        specs.append(_const_spec(wo_tok.shape, single_buffer=True))
    for a in (wo, gpost, gpre, w1, w2, gmlp):
        ins.append(a)
        specs.append(_const_spec(a.shape, single_buffer=True))
    return pl.pallas_call(
        kern,
        grid=(b, s // tm),
        in_specs=specs,
        out_specs=xspec,
        out_shape=jax.ShapeDtypeStruct(x.shape, x.dtype),
        compiler_params=_cparams(("arbitrary", "arbitrary")),
        name="post_tok" if has_tok else "post",
    )(*ins)


def _c_pre_kernel(x_ref, g_ref, wq_ref, wtok_ref, wf_ref, kfs_ref, kfw_ref, qT_ref, kc_ref, vc_ref, ks_ref, kw_ref,
                  vsT_ref, vwT_ref, gT_ref, z_ref, *, tm, q_scale):
    xn = _rms(x_ref[0], g_ref[...]).astype(BF16)
    qT_ref[0] = (_nt(wq_ref[...], xn) * q_scale).astype(BF16)
    tok = jnp.dot(xn, wtok_ref[...], preferred_element_type=F32)
    halves = KV_W // LANES
    for lg in range(2 * halves):
        z_ref[lg] = tok[:, lg * LANES:(lg + 1) * LANES]
    for l in range(CMP_STRIDE):
        for lg in range(2 * halves):
            rows = z_ref[lg, pl.ds(l, tm // CMP_STRIDE, stride=CMP_STRIDE), :].astype(BF16)
            out_ref = kc_ref if lg < halves else vc_ref
            off = l * KV_W + (lg % halves) * LANES
            out_ref[0, :, off:off + LANES] = rows
    kfs = kfs_ref[...]
    kfw = kfw_ref[...]
    for g in range(NSA_GROUPS):
        a = 2 * KV_W + g * NSA_DH
        ks_ref[0, g] = jnp.concatenate([tok[:, a:a + NSA_DH], kfs], axis=1).astype(BF16)
        kw_ref[0, g] = jnp.concatenate([tok[:, a + KV_W:a + KV_W + NSA_DH], kfw], axis=1).astype(BF16)
    fT = _nt(wf_ref[...], xn)
    ones = _ones_rows(tm)

    def with_ones(v):
        return jnp.concatenate([x for g in range(NSA_GROUPS) for x in (v[g * NSA_DH:(g + 1) * NSA_DH], ones)],
                               axis=0)

    vs = with_ones(fT[:KV_W])
    vw = with_ones(fT[KV_W:2 * KV_W])
    for cidx in range(tm // ATTN_TILE):
        sl = slice(cidx * ATTN_TILE, (cidx + 1) * ATTN_TILE)
        vsT_ref[0, cidx] = vs[:, sl].astype(BF16)
        vwT_ref[0, cidx] = vw[:, sl].astype(BF16)
    gT_ref[0] = 1.0 / (1.0 + jnp.exp(-fT[2 * KV_W:]))


def _c_pre(x, g, wq, wtok, wf, kfs, kfw):
    b, s, d = x.shape
    tm = min(TOKEN_TILE, s)
    q_scale = NSA_DH ** -0.5 * LOG2E
    kern = functools.partial(_c_pre_kernel, tm=tm, q_scale=q_scale)
    nt = tm // ATTN_TILE
    grows = NSA_GROUPS * GATE_ROWS
    kaug = NSA_DH + kfs.shape[1]
    vrows = NSA_GROUPS * V_ROWS
    kf_spec = pl.BlockSpec((tm, kfs.shape[1]), lambda i, j: (j, 0))
    return pl.pallas_call(
        kern,
        grid=(b, s // tm),
        in_specs=[pl.BlockSpec((1, tm, d), lambda i, j: (i, j, 0)),
                  _const_spec(g.shape), _const_spec(wq.shape), _const_spec(wtok.shape), _const_spec(wf.shape),
                  kf_spec, kf_spec],
        out_specs=[
            pl.BlockSpec((1, C_MIX, tm), lambda i, j: (i, 0, j)),
            pl.BlockSpec((1, tm // CMP_STRIDE, CMP_STRIDE * KV_W), lambda i, j: (i, j, 0)),
            pl.BlockSpec((1, tm // CMP_STRIDE, CMP_STRIDE * KV_W), lambda i, j: (i, j, 0)),
            pl.BlockSpec((1, NSA_GROUPS, tm, kaug), lambda i, j: (i, 0, j, 0)),
            pl.BlockSpec((1, NSA_GROUPS, tm, kaug), lambda i, j: (i, 0, j, 0)),
            pl.BlockSpec((1, nt, vrows, ATTN_TILE), lambda i, j: (i, j, 0, 0)),
            pl.BlockSpec((1, nt, vrows, ATTN_TILE), lambda i, j: (i, j, 0, 0)),
            pl.BlockSpec((1, grows, tm), lambda i, j: (i, 0, j)),
        ],
        out_shape=[
            jax.ShapeDtypeStruct((b, C_MIX, s), BF16),
            jax.ShapeDtypeStruct((b, s // CMP_STRIDE, CMP_STRIDE * KV_W), BF16),
            jax.ShapeDtypeStruct((b, s // CMP_STRIDE, CMP_STRIDE * KV_W), BF16),
            jax.ShapeDtypeStruct((b, NSA_GROUPS, s, kaug), BF16),
            jax.ShapeDtypeStruct((b, NSA_GROUPS, s, kaug), BF16),
            jax.ShapeDtypeStruct((b, s // ATTN_TILE, vrows, ATTN_TILE), BF16),
            jax.ShapeDtypeStruct((b, s // ATTN_TILE, vrows, ATTN_TILE), BF16),
            jax.ShapeDtypeStruct((b, grows, s), F32),
        ],
        scratch_shapes=[pltpu.VMEM((2 * KV_W // LANES, tm, LANES), F32)],
        compiler_params=_cparams(("arbitrary", "arbitrary")),
        name="c_pre",
    )(x, g, wq, wtok, wf, kfs, kfw)


def _gelu_tanh(x):
    return 0.5 * x * (1.0 + jnp.tanh(math.sqrt(2.0 / math.pi) * (x + 0.044715 * (x * x * x))))


def _compress_kernel(zk_ref, zv_ref, pek_ref, pev_ref, wak_ref, wbk_ref, wav_ref, wbv_ref, w2k_ref, w2vT_ref, kf_ref,
                     kc_ref, vcT_ref, *, nch):
    def hidden(z_ref, pe_ref, wa_ref, wb_ref):
        z = z_ref[0].astype(F32)
        za = (z + pe_ref[0:1]).astype(BF16)
        zb = (z + pe_ref[1:2]).astype(BF16)
        a = jnp.dot(za, wa_ref[...], preferred_element_type=F32)
        bm = jnp.dot(zb, wb_ref[...], preferred_element_type=F32)
        return _gelu_tanh(a + pltpu.roll(bm, nch - 1, 0)).astype(BF16)

    hk = hidden(zk_ref, pek_ref, wak_ref, wbk_ref)
    kc = jnp.dot(hk, w2k_ref[...], preferred_element_type=F32)
    kf = kf_ref[...]
    hv = hidden(zv_ref, pev_ref, wav_ref, wbv_ref)
    vcT = _nt(w2vT_ref[...], hv)
    ones = _ones_rows(nch)
    for g in range(NSA_GROUPS):
        gs = slice(g * NSA_DH, (g + 1) * NSA_DH)
        kc_ref[0, g] = jnp.concatenate([kc[:, gs], kf], axis=1).astype(BF16)
        vcT_ref[0, g] = jnp.concatenate([vcT[gs], ones], axis=0).astype(BF16)


def _compress(zk, zv, pek, pev, wak, wbk, wav, wbv, w2k, w2vT, kf):
    b, nch, cw = zk.shape
    kern = functools.partial(_compress_kernel, nch=nch)
    zspec = pl.BlockSpec((1, nch, cw), lambda i: (i, 0, 0))
    consts = (pek, pev, wak, wbk, wav, wbv, w2k, w2vT, kf)
    return pl.pallas_call(
        kern,
        grid=(b,),
        in_specs=[zspec, zspec] + [_const_spec(a.shape) for a in consts],
        out_specs=[pl.BlockSpec((1, NSA_GROUPS, nch, NSA_DH + KFEAT), lambda i: (i, 0, 0, 0)),
                   pl.BlockSpec((1, NSA_GROUPS, V_ROWS, nch), lambda i: (i, 0, 0, 0))],
        out_shape=[jax.ShapeDtypeStruct((b, NSA_GROUPS, nch, NSA_DH + KFEAT), BF16),
                   jax.ShapeDtypeStruct((b, NSA_GROUPS, V_ROWS, nch), BF16)],
        compiler_params=_cparams(("arbitrary",)),
        name="compress",
    )(zk, zv, *consts)


def _split3(x):
    hi = x.astype(BF16)
    r = x - hi.astype(F32)
    mid = r.astype(BF16)
    lo = (r - mid.astype(F32)).astype(BF16)
    return hi, mid, lo


def _feature_rows(tab_ref, head, const, t):
    row = lax.broadcasted_iota(jnp.int32, (KFEAT, t), 0)
    r3 = row - 3 * (row >= 3).astype(jnp.int32) - 3 * (row >= 6).astype(jnp.int32)
    s_hi, s_mid, s_lo, slope = (tab_ref[head * 4 + k] for k in range(4))
    c_hi, c_mid, c_lo = (x.astype(F32) for x in _split3(slope * const))
    svals = jnp.where(r3 == 0, s_hi, jnp.where(r3 == 1, s_mid, s_lo))
    cvals = jnp.where(r3 == 0, c_hi, jnp.where(r3 == 1, c_mid, c_lo))
    return jnp.where(row < 6, svals, jnp.where(row < 9, cvals, 0.0)).astype(BF16)


def _cmp_topk_kernel(tab_ref, qT_ref, kc_ref, vcT_ref, pool_ref, gT_ref, ocT_ref, bias_ref, act_ref,
                     qa_ref, e_ref, acc_ref, imp_ref, rank_ref, ps_ref, *, t, gb, nch, nslc, nsel, ov_terms):
    gp = pl.program_id(1)
    i = pl.program_id(2)
    q0 = i * t
    chains = range(gb * NSA_REP)
    bpt = t // SLC_LEN

    const = (CMP_LEN - 1) / 2.0 - jnp.full((KFEAT, t), q0, jnp.int32).astype(F32)
    for c in chains:
        head = (gp * gb + c // NSA_REP) * NSA_REP + c % NSA_REP
        qa_ref[c, 0:NSA_DH, :] = qT_ref[0, c * NSA_DH:(c + 1) * NSA_DH, :]
        qa_ref[c, NSA_DH:NSA_DH + KFEAT, :] = _feature_rows(tab_ref, head, const, t)

    def branch(rows):
        n_i = lax.broadcasted_iota(jnp.int32, (rows, t), 0)
        t_i = q0 + lax.broadcasted_iota(jnp.int32, (rows, t), 1)
        ended = jnp.where(n_i * CMP_STRIDE + (CMP_LEN - 1) <= t_i, 0.0, NEG)

        def cmp_scores(c):
            s = jnp.dot(kc_ref[0, c // NSA_REP, 0:rows, :], qa_ref[c], preferred_element_type=F32) + ended
            return _stage_scores(s, e_ref.at[c, 0:rows])

        def cmp_softmax_pv(c, mt):
            m_eff = jnp.where(mt > 0.5 * NEG, mt, 0.0)
            e = jnp.exp2(e_ref[c, 0:rows] - m_eff)
            e_ref[c, 0:rows] = e
            acc_ref[c] = jnp.dot(vcT_ref[0, c // NSA_REP, :, 0:rows], e.astype(BF16), preferred_element_type=F32)

        mt = [cmp_scores(c) for c in chains]
        for c in chains:
            cmp_softmax_pv(c, mt[c])
        for gl in range(gb):
            psum = jnp.zeros((rows, t), F32)
            for r in range(NSA_REP):
                c = gl * NSA_REP + r
                acc = acc_ref[c]
                l = acc[NSA_DH:NSA_DH + 1]
                inv = 1.0 / jnp.where(l > 0.0, l, 1.0)
                gate = gT_ref[0, gl * GATE_ROWS + r:gl * GATE_ROWS + r + 1, :]
                ocT_ref[0, c * NSA_DH:(c + 1) * NSA_DH, :] = (gate * (acc[:NSA_DH] * inv)).astype(BF16)
                psum = psum + e_ref[c, 0:rows] * inv
            for h in range(t // LANES):
                ps_ref[gl, h, 0:rows] = psum[:, h * LANES:(h + 1) * LANES]
                if rows < nch:
                    ps_ref[gl, h, rows:nch] = jnp.zeros((nch - rows, LANES), F32)

    early = i < (nch // 2) // (t // CMP_STRIDE)
    pl.when(early)(functools.partial(branch, nch // 2))
    pl.when(jnp.logical_not(early))(functools.partial(branch, nch))

    j_i = lax.broadcasted_iota(jnp.int32, (nslc, t), 0)
    cur = jnp.right_shift(q0 + lax.broadcasted_iota(jnp.int32, (nslc, t), 1), SLC_SHIFT)
    forced = (j_i == 0) | (j_i == cur) | (j_i == cur - 1)
    per_blk = SLC_LEN // CMP_STRIDE
    for gl in range(gb):
        imp = jnp.zeros((nslc, t), F32)
        for k, wgt in ov_terms:
            rows = jnp.concatenate([ps_ref[gl, h, pl.ds(k % per_blk, nslc, stride=per_blk), :]
                                    for h in range(t // LANES)], axis=1)
            if k < 0:
                rows = jnp.where(j_i == 0, 0.0, pltpu.roll(rows, 1, 0))
            imp = imp + wgt * rows
        imp_ref[gl] = jnp.where(j_i > cur, NEG, imp + jnp.where(forced, FORCE_BONUS, 0.0))
        rank_ref[gl] = jnp.zeros((nslc, t), F32)

    for chunk in range(nslc // bpt):
        @pl.when(chunk <= i)
        def _():
            for gl in range(gb):
                imp = imp_ref[gl]
                rank = rank_ref[gl]
                for jp in range(chunk * bpt, (chunk + 1) * bpt):
                    row = imp[jp:jp + 1, :]
                    incs = []
                    for rg in range(nslc // SUBLANES):
                        r0 = SUBLANES * rg
                        blk = imp[r0:r0 + SUBLANES]
                        if r0 > jp:
                            incs.append(jnp.where(row >= blk, 1.0, 0.0))
                        elif r0 + SUBLANES - 1 <= jp:
                            incs.append(jnp.where(row > blk, 1.0, 0.0))
                        else:
                            above = lax.broadcasted_iota(jnp.int32, (SUBLANES, t), 0) > jp - r0
                            incs.append(jnp.where(above, jnp.where(row >= blk, 1.0, 0.0),
                                                  jnp.where(row > blk, 1.0, 0.0)))
                    rank = rank + jnp.concatenate(incs, axis=0)
                rank_ref[gl] = rank

    for gl in range(gb):
        bias = jnp.where(j_i > cur, NEG, jnp.where(rank_ref[gl] < nsel, 0.0, NEG))
        bias_ref[0, gl] = bias
        hit = jnp.where(jnp.max(bias, axis=1, keepdims=True) == 0.0, 1.0, 0.0)
        tiles = jnp.sum(hit * pool_ref[...], axis=0, keepdims=True)
        act_ref[0, gl, 0] = jnp.broadcast_to(jnp.where(tiles > 0.0, 1, 0), (SUBLANES, LANES)).astype(jnp.int32)


def _overlap_terms():
    per_blk = SLC_LEN // CMP_STRIDE
    terms = []
    for k in range(-(CMP_LEN // CMP_STRIDE) + 1, per_blk):
        lo, hi = max(k * CMP_STRIDE, 0), min(k * CMP_STRIDE + CMP_LEN, SLC_LEN)
        if hi > lo:
            terms.append((k, (hi - lo) / CMP_LEN))
    assert SLC_LEN % CMP_STRIDE == 0 and all(k >= -per_blk for k, _ in terms)
    return tuple(terms)


def _cmp_topk(tab, qT, kc, vcT, gT, nsel):
    b, _, s = qT.shape
    t = ATTN_TILE
    assert s % t == 0 and t % SLC_LEN == 0
    nch = kc.shape[2]
    nslc = s // SLC_LEN
    gb = CMP_GROUPS_PER_STEP
    nc = gb * NSA_REP
    kern = functools.partial(_cmp_topk_kernel, t=t, gb=gb, nch=nch, nslc=nslc, nsel=nsel,
                             ov_terms=_overlap_terms())
    gw = nc * NSA_DH
    assert s // t <= LANES
    pool = np.zeros((nslc, LANES), np.float32)
    pool[np.arange(nslc), np.arange(nslc) // (t // SLC_LEN)] = 1.0
    pool = jnp.asarray(pool, F32)
    return pl.pallas_call(
        kern,
        grid=(b, NSA_GROUPS // gb, s // t),
        in_specs=[
            pl.BlockSpec(memory_space=pltpu.SMEM),
            pl.BlockSpec((1, gw, t), lambda bi, g, i: (bi, g, i)),
            pl.BlockSpec((1, gb, nch, NSA_DH + KFEAT), lambda bi, g, i: (bi, g, 0, 0)),
            pl.BlockSpec((1, gb, V_ROWS, nch), lambda bi, g, i: (bi, g, 0, 0)),
            _const_spec(pool.shape),
            pl.BlockSpec((1, gb * GATE_ROWS, t), lambda bi, g, i: (bi, g, i)),
        ],
        out_specs=[pl.BlockSpec((1, gw, t), lambda bi, g, i: (bi, g, i)),
                   pl.BlockSpec((1, gb, nslc, t), lambda bi, g, i: (bi, g, 0, i)),
                   pl.BlockSpec((1, gb, 1, SUBLANES, LANES), lambda bi, g, i: (bi, g, i, 0, 0))],
        out_shape=[jax.ShapeDtypeStruct((b, C_MIX, s), BF16),
                   jax.ShapeDtypeStruct((b, NSA_GROUPS, nslc, s), F32),
                   jax.ShapeDtypeStruct((b, NSA_GROUPS, s // t, SUBLANES, LANES), jnp.int32)],
        scratch_shapes=[pltpu.VMEM((nc, NSA_DH + KFEAT, t), BF16),
                        pltpu.VMEM((nc, nch, t), F32),
                        pltpu.VMEM((nc, V_ROWS, t), F32),
                        pltpu.VMEM((gb, nslc, t), F32), pltpu.VMEM((gb, nslc, t), F32),
                        pltpu.VMEM((gb, t // LANES, nch, LANES), F32)],
        compiler_params=_cparams(("arbitrary", "arbitrary", "arbitrary")),
        name="cmp_topk",
    )(tab, qT, kc, vcT, pool, gT)


def _slc_win_kernel(act_ref, tab_ref, qT_ref, ks_ref, vsT_ref, kw_ref, vwT_ref, bias_ref, ocT_ref, gT_ref, yT_ref,
                    qa_ref, s_ref, acc_ref, sw_ref, accw_ref, idx_ref, *, t, gb, nslc, nkv):
    gp = pl.program_id(1)
    i = pl.program_id(2)
    chains = range(gb * NSA_REP)
    feat0 = NSA_DH + nslc

    const = -jnp.full((KFEAT, t), i * t, jnp.int32).astype(F32)
    for c in chains:
        gl, r = divmod(c, NSA_REP)
        head = (gp * gb + gl) * NSA_REP + r
        qa_ref[c, 0:NSA_DH, :] = qT_ref[0, c * NSA_DH:(c + 1) * NSA_DH, :]
        qa_ref[c, NSA_DH:feat0, :] = bias_ref[0, gl].astype(BF16)
        qa_ref[c, feat0:feat0 + KFEAT, :] = _feature_rows(tab_ref, head, const, t)

    def scores_from(k_ref):
        def scores(c, j):
            k = k_ref[0, c // NSA_REP, pl.ds(pl.multiple_of(j * t, t), t), :]
            return jnp.dot(k, qa_ref[c], preferred_element_type=F32)
        return scores

    def vtile_from(vT_ref):
        def vtile(c, j):
            gl = c // NSA_REP
            return vT_ref[0, j, gl * V_ROWS:(gl + 1) * V_ROWS, :]
        return vtile

    kpos, qpos = _tile_pos(t, 0, 0)
    causal = jnp.where(kpos <= qpos, 0.0, NEG)

    w_scores = scores_from(kw_ref)
    w_vtile = vtile_from(vwT_ref)
    w_tiles = (i, jnp.maximum(i - 1, 0), jnp.maximum(i - 2, 0))
    in_seq1 = jnp.where((i - 1) * t + kpos >= 0, 0.0, NEG)
    in_seq2 = jnp.where((i - 2) * t + kpos >= 0, 0.0, NEG)
    w_extra = (causal, in_seq1, jnp.where(qpos < kpos, in_seq2, NEG))
    m = [jnp.full((1, t), NEG, F32) for _ in chains]
    mt = []
    for c in chains:
        accw_ref[c] = jnp.zeros((V_ROWS, t), F32)
        mt.append(_stage_scores(w_scores(c, w_tiles[0]) + w_extra[0], sw_ref.at[c]))
    for w in range(3):
        for c in chains:
            p, m[c], alpha = _stage_softmax(sw_ref.at[c], m[c], mt[c])
            if w + 1 < 3:
                mt[c] = _stage_scores(w_scores(c, w_tiles[w + 1]) + w_extra[w + 1], sw_ref.at[c])
            _stage_pv(accw_ref.at[c], w_vtile(c, w_tiles[w]), p, alpha)

    stride = nkv + 1
    counts, spares = [], []
    for gl in range(gb):
        g_abs = gp * gb + gl
        base = ((pl.program_id(0) * NSA_GROUPS + g_abs) * pl.num_programs(2) + i) * nkv
        n_act, spare = jnp.int32(0), jnp.int32(0)
        for j in range(nkv):
            idx_ref[gl * stride + n_act] = j
            keep = (j < i) & (act_ref[base + j] != 0)
            n_act = n_act + jnp.where(keep, 1, 0)
            spare = jnp.where((j < i) & jnp.logical_not(keep), j, spare)
        counts.append(n_act)
        spares.append(spare)
    n_rest = functools.reduce(jnp.maximum, counts)
    for gl in range(gb):
        for k in range(nkv):
            idx_ref[gl * stride + k] = jnp.where(k >= counts[gl], spares[gl], idx_ref[gl * stride + k])
    tile_fns = [functools.partial(lambda gl, n: idx_ref[gl * stride + n], gl) for gl in range(gb)]
    _causal_flash(i, n_rest, tile_fns, lambda c: c // NSA_REP, t, chains, scores_from(ks_ref),
                  vtile_from(vsT_ref), causal, s_ref, acc_ref)

    for c in chains:
        gl, r = divmod(c, NSA_REP)
        g0 = gl * GATE_ROWS
        rs = slice(c * NSA_DH, (c + 1) * NSA_DH)
        y = (ocT_ref[0, rs, :].astype(F32)
             + gT_ref[0, g0 + NSA_REP + r:g0 + NSA_REP + r + 1, :] * _normalized(acc_ref.at[c])
             + gT_ref[0, g0 + 2 * NSA_REP + r:g0 + 2 * NSA_REP + r + 1, :] * _normalized(accw_ref.at[c]))
        yT_ref[0, rs, :] = y.astype(BF16)


def _slc_win(act, tab, qT, ks, vsT, kw, vwT, bias, ocT, gT):
    b, _, s = qT.shape
    t = ATTN_TILE
    assert s % t == 0 and WINDOW == 2 * t
    nkv = s // t
    nslc = bias.shape[2]
    kaug = ks.shape[3]
    gb = NSA_GROUPS_PER_STEP
    nc = gb * NSA_REP
    gw = nc * NSA_DH
    kern = functools.partial(_slc_win_kernel, t=t, gb=gb, nslc=nslc, nkv=nkv)
    qspec = pl.BlockSpec((1, gw, t), lambda bi, g, i, *_: (bi, g, i))
    kspec = pl.BlockSpec((1, gb, s, kaug), lambda bi, g, i, *_: (bi, g, 0, 0))
    vspec = pl.BlockSpec((1, nkv, gb * V_ROWS, t), lambda bi, g, i, *_: (bi, 0, g, 0))
    grid_spec = pltpu.PrefetchScalarGridSpec(
        num_scalar_prefetch=2,
        grid=(b, NSA_GROUPS // gb, s // t),
        in_specs=[
            qspec, kspec, vspec, kspec, vspec,
            pl.BlockSpec((1, gb, nslc, t), lambda bi, g, i, *_: (bi, g, 0, i)),
            qspec,
            pl.BlockSpec((1, gb * GATE_ROWS, t), lambda bi, g, i, *_: (bi, g, i)),
        ],
        out_specs=qspec,
        scratch_shapes=[pltpu.VMEM((nc, kaug, t), BF16),
                        pltpu.VMEM((nc, t, t), F32), pltpu.VMEM((nc, V_ROWS, t), F32),
                        pltpu.VMEM((nc, t, t), F32), pltpu.VMEM((nc, V_ROWS, t), F32),
                        pltpu.SMEM((gb * (nkv + 1),), jnp.int32)],
    )
    return pl.pallas_call(
        kern,
        grid_spec=grid_spec,
        out_shape=jax.ShapeDtypeStruct((b, C_MIX, s), BF16),
        compiler_params=_cparams(("arbitrary", "arbitrary", "arbitrary")),
        name="slc_win",
    )(act, tab, qT, ks, vsT, kw, vwT, bias, ocT, gT)


def _row(v):
    return v.reshape(1, -1).astype(F32)


def _layer_a(x, g_pre, w_in, conv_w, q_norm, w_q_up, kv_norm, w_kv_up, w_out, g_post, g_mlp_pre, w1, w2, g_mlp_post):
    b, s, d = x.shape
    o3 = 3 * CONV_WIDTH
    o5 = o3 + MLA_Q_RANK + MLA_KV_RANK
    pad = MLA_SLOT - MLA_NOPE - MLA_ROPE
    w_main = w_in[:, :o5].astype(BF16)
    wkr = jnp.pad(w_in[:, o5:].T, ((MLA_NOPE, pad), (0, 0))).astype(BF16)
    wq = w_q_up.reshape(MLA_Q_RANK, MLA_HEADS, MLA_NOPE + MLA_ROPE)
    wq = jnp.pad(wq, ((0, 0), (0, 0), (0, pad))).reshape(MLA_Q_RANK, MLA_HEADS * MLA_SLOT).T.astype(BF16)
    wkv = w_kv_up.reshape(MLA_KV_RANK, MLA_HEADS, MLA_NOPE + MLA_V)
    wk = jnp.pad(wkv[:, :, :MLA_NOPE], ((0, 0), (0, 0), (0, MLA_SLOT - MLA_NOPE)))
    wk = wk.reshape(MLA_KV_RANK, MLA_HEADS * MLA_SLOT).astype(BF16)
    wv = wkv[:, :, MLA_NOPE:].reshape(MLA_KV_RANK, MLA_HEADS * MLA_V).T.astype(BF16)

    inv = ROPE_THETA ** (-jnp.arange(ROPE_HALF, dtype=F32) / ROPE_HALF)
    ang = inv[:, None] * jnp.arange(s, dtype=F32)[None, :]
    cos, sin = jnp.cos(ang), jnp.sin(ang)

    yconv, qT, k, vT = _a_pre(x, _row(g_pre), w_main, wkr, conv_w.astype(F32), _row(q_norm), wq, _row(kv_norm),
                              wk, wv, cos, sin)
    yT = _mla_attn(qT, k, vT)
    wo = w_out.astype(BF16)
    return _post(x, yconv, yT, wo[:CONV_WIDTH], wo[CONV_WIDTH:], _row(g_post), _row(g_mlp_pre),
                 w1.astype(BF16), w2.astype(BF16), _row(g_mlp_post))


def _layer_c(x, g_pre, w_in, pe_k, w1_k, w2_k, pe_v, w1_v, w2_v, w_out, g_post, g_mlp_pre, w1, w2, g_mlp_post):
    b, s, d = x.shape
    G, R, Dh = NSA_GROUPS, NSA_REP, NSA_DH
    wq = w_in[:, :C_MIX].T.astype(BF16)
    kv = w_in[:, C_MIX:C_MIX + 6 * KV_W].reshape(d, 6, KV_W)
    wtok = jnp.concatenate([kv[:, 0], kv[:, 1], kv[:, 2], kv[:, 4]], axis=1).astype(BF16)
    wg = w_in[:, C_MIX + 6 * KV_W:].reshape(d, 3, G, R).transpose(0, 2, 1, 3)
    wg = jnp.pad(wg.reshape(d, G, 3 * R), ((0, 0), (0, 0), (0, GATE_ROWS - 3 * R))).reshape(d, G * GATE_ROWS)
    wf = jnp.concatenate([kv[:, 3], kv[:, 5], wg], axis=1).T.astype(BF16)

    n_slc = s // SLC_LEN
    pos = np.arange(s)
    blk, within = pos // SLC_LEN, pos % SLC_LEN
    feats = np.zeros((s, KFEAT), np.float32)
    feats[:, 0:3] = (blk * SLC_LEN)[:, None]
    feats[:, 3:6] = within[:, None]
    feats[:, 6:9] = 1.0
    onehot = (blk[:, None] == np.arange(n_slc)[None, :]).astype(np.float32)
    kfs = jnp.asarray(np.concatenate([onehot, feats], axis=1), F32)
    kfw = jnp.asarray(np.concatenate([np.zeros_like(onehot), feats], axis=1), F32)

    qT, kc_z, vc_z, ks, kw, vsT, vwT, gT = _c_pre(x, _row(g_pre), wq, wtok, wf, kfs, kfw)

    nch = s // CMP_STRIDE
    half = CMP_LEN // 2
    eye = jnp.eye(G, dtype=F32)

    def w1_halves(w):
        bd = jnp.einsum('hlde,gk->hlgdke', w.reshape(2, half, Dh, Dh), eye)
        bd = bd.reshape(2, half * KV_W, KV_W).astype(BF16)
        return bd[0], bd[1]

    def pe_halves(pe):
        return jnp.tile(pe.reshape(2, half, 1, Dh), (1, 1, G, 1)).reshape(2, half * KV_W).astype(F32)

    def w2_bd(w):
        return jnp.einsum('de,gk->gdke', w, eye).reshape(KV_W, KV_W)

    n_idx = np.arange(nch)
    cfeat = np.zeros((nch, KFEAT), np.float32)
    cfeat[:, 0:3] = (1024 * (n_idx // 64))[:, None]
    cfeat[:, 3:6] = (CMP_STRIDE * (n_idx % 64))[:, None]
    cfeat[:, 6:9] = 1.0
    kc, vcT = _compress(kc_z, vc_z, pe_halves(pe_k), pe_halves(pe_v), *w1_halves(w1_k), *w1_halves(w1_v),
                        w2_bd(w2_k).astype(BF16), w2_bd(w2_v).T.astype(BF16), jnp.asarray(cfeat, F32))

    slopes_np = (2.0 ** (-8.0 * np.arange(1, NSA_HEADS + 1) / NSA_HEADS)).astype(np.float32)
    full = (slopes_np * np.float32(LOG2E)).astype(np.float32)
    pieces, rest = [], full.copy()
    for _ in range(3):
        piece = rest.astype(BF16).astype(np.float32)
        pieces.append(piece)
        rest = (rest - piece).astype(np.float32)
    tab = jnp.asarray(np.stack(pieces + [full], axis=1).reshape(-1), F32)

    ocT, bias, act = _cmp_topk(tab, qT, kc, vcT, gT, min(N_SEL, n_slc))
    act = act[:, :, :, 0, :s // ATTN_TILE].reshape(-1)
    yT = _slc_win(act, tab, qT, ks, vsT, kw, vwT, bias, ocT, gT)
    return _post(x, None, yT, None, w_out.astype(BF16), _row(g_post), _row(g_mlp_pre),
                 w1.astype(BF16), w2.astype(BF16), _row(g_mlp_post))


def kernel(x, norm_mix_pre, norm_mix_post, norm_mlp_pre, norm_mlp_post, mlp_w1, mlp_w2, a_w_in, a_conv_w, a_q_norm,
           a_w_q_up, a_kv_norm, a_w_kv_up, a_w_out, c_w_in, c_cmp_pe_k, c_cmp_w1_k, c_cmp_w2_k, c_cmp_pe_v,
           c_cmp_w1_v, c_cmp_w2_v, c_w_out):
    depth = norm_mix_pre.shape[0]
    for layer in range(depth):
        i = layer // 2
        common = (norm_mix_post[layer], norm_mlp_pre[layer], mlp_w1[layer], mlp_w2[layer], norm_mlp_post[layer])
        if layer % 2 == 0:
            x = _layer_a(x, norm_mix_pre[layer], a_w_in[i], a_conv_w[i], a_q_norm[i], a_w_q_up[i], a_kv_norm[i],
                         a_w_kv_up[i], a_w_out[i], *common)
        else:
            x = _layer_c(x, norm_mix_pre[layer], c_w_in[i], c_cmp_pe_k[i], c_cmp_w1_k[i], c_cmp_w2_k[i],
                         c_cmp_pe_v[i], c_cmp_w1_v[i], c_cmp_w2_v[i], c_w_out[i], *common)
    return x
```

```python
import functools
import math

import numpy as np
import jax
import jax.numpy as jnp
from jax import lax
from jax.experimental import pallas as pl
from jax.experimental.pallas import tpu as pltpu

F32 = jnp.float32
BF16 = jnp.bfloat16

D_MODEL = 1024
D_FF = 4 * D_MODEL
EPS = 1e-6
NEG = -1e30
LOG2E = math.log2(math.e)

CONV_WIDTH = 512
CONV_K = 3
MLA_HEADS = 8
MLA_NOPE = 64
MLA_ROPE = 32
MLA_V = 64
MLA_KV_RANK = 256
MLA_Q_RANK = 768
MLA_SLOT = 128
ROPE_THETA = 10000.0
ROPE_HALF = MLA_ROPE // 2

NSA_HEADS = 16
NSA_GROUPS = 4
NSA_REP = 4
NSA_DH = 64
CMP_LEN = 32
CMP_STRIDE = 16
SLC_LEN = 64
SLC_SHIFT = 6
N_SEL = 16
WINDOW = 512
FORCE_BONUS = 1e4
KV_W = NSA_GROUPS * NSA_DH
C_MIX = NSA_HEADS * NSA_DH
GATE_ROWS = 16
ONES_ROWS = 16
V_ROWS = 64 + ONES_ROWS
LANES = 128
SUBLANES = 8
BF16_ROWS = 16
KFEAT = 16

TOKEN_TILE = 512
POST_TILE = 512
ATTN_TILE = 256
MLA_HEADS_PER_STEP = 8
NSA_GROUPS_PER_STEP = 4
CMP_GROUPS_PER_STEP = 4
FF_CHUNK = 1024
VMEM_LIMIT = 60 * 1024 * 1024


def _cparams(sem):
    return pltpu.CompilerParams(dimension_semantics=sem, vmem_limit_bytes=VMEM_LIMIT)


def _const_spec(shape, single_buffer=False):
    nd = len(shape)
    if single_buffer:
        return pl.BlockSpec(shape, lambda *_: (0,) * nd, pipeline_mode=pl.Buffered(1))
    return pl.BlockSpec(shape, lambda *_: (0,) * nd)


def _rms(x, g):
    ms = jnp.mean(x * x, axis=-1, keepdims=True)
    return x * lax.rsqrt(ms + EPS) * g


def _nt(a, b):
    return lax.dot_general(a, b, (((1,), (1,)), ((), ())), preferred_element_type=F32)


def _tn(a, b):
    return lax.dot_general(a, b, (((0,), (0,)), ((), ())), preferred_element_type=F32)


def _ones_rows(n):
    return jnp.where(lax.broadcasted_iota(jnp.int32, (ONES_ROWS, n), 0) == 0, 1.0, 0.0).astype(F32)


def _rope_rows(r1, r2, cos, sin):
    return r1 * cos - r2 * sin, r2 * cos + r1 * sin


def _a_pre_kernel(x_ref, g_ref, w1_ref, wkr_ref, convw_ref, qn_ref, wq_ref, kvn_ref, wk_ref, wv_ref,
                  cos_ref, sin_ref, yconv_ref, qT_ref, k_ref, vT_ref, carry_ref, *, tm, q_scale):
    @pl.when(pl.program_id(1) == 0)
    def _():
        carry_ref[...] = jnp.zeros_like(carry_ref)

    sub = min(ATTN_TILE, tm)
    w = convw_ref[...]
    w0, w1, w2 = w[0:1], w[1:2], w[2:3]
    hd = BF16_ROWS
    tail = carry_ref[...]
    z32 = jnp.zeros((MLA_SLOT - MLA_NOPE - MLA_ROPE, sub), F32)
    ones = _ones_rows(sub)
    o1, o2, o3 = CONV_WIDTH, 2 * CONV_WIDTH, 3 * CONV_WIDTH
    o4 = o3 + MLA_Q_RANK
    for cidx in range(tm // sub):
        rows = slice(cidx * sub, (cidx + 1) * sub)
        xn = _rms(x_ref[0, rows, :], g_ref[...]).astype(BF16)
        p1 = jnp.dot(xn, w1_ref[...], preferred_element_type=F32)
        g_b, g_c, hv = p1[:, :o1], p1[:, o1:o2], p1[:, o2:o3]
        c_q, c_kv = p1[:, o3:o4], p1[:, o4:]

        u = g_c * hv
        y = w2 * u + w1 * pltpu.roll(u, 1, 0) + w0 * pltpu.roll(u, 2, 0)
        uh = u[0:hd]
        row = lax.broadcasted_iota(jnp.int32, (hd, CONV_WIDTH), 0)
        h1 = jnp.where(row < 1, pltpu.roll(tail, 1, 0), pltpu.roll(uh, 1, 0))
        h2 = jnp.where(row < 2, pltpu.roll(tail, 2, 0), pltpu.roll(uh, 2, 0))
        yh = w2 * uh + w1 * h1 + w0 * h2
        yconv_ref[0, cidx * sub + hd:(cidx + 1) * sub, :] = (g_b[hd:] * y[hd:]).astype(BF16)
        yconv_ref[0, cidx * sub:cidx * sub + hd, :] = (g_b[0:hd] * yh).astype(BF16)
        tail = u[sub - hd:sub]

        cos = cos_ref[:, rows]
        sin = sin_ref[:, rows]

        hq = _rms(c_q, qn_ref[...]).astype(BF16)
        qT = _nt(wq_ref[...], hq)
        for h in range(MLA_HEADS):
            b0 = MLA_SLOT * h
            r1 = qT[b0 + MLA_NOPE:b0 + MLA_NOPE + ROPE_HALF]
            r2 = qT[b0 + MLA_NOPE + ROPE_HALF:b0 + MLA_NOPE + MLA_ROPE]
            e1, e2 = _rope_rows(r1, r2, cos, sin)
            slot = jnp.concatenate([qT[b0:b0 + MLA_NOPE], e1, e2, z32], axis=0) * q_scale
            qT_ref[0, b0:b0 + MLA_SLOT, rows] = slot.astype(BF16)

        krT = _nt(wkr_ref[...], xn)
        r1 = krT[MLA_NOPE:MLA_NOPE + ROPE_HALF]
        r2 = krT[MLA_NOPE + ROPE_HALF:MLA_NOPE + MLA_ROPE]
        e1, e2 = _rope_rows(r1, r2, cos, sin)
        slotT = jnp.concatenate([jnp.zeros((MLA_NOPE, sub), F32), e1, e2, z32], axis=0)
        slot = slotT.T

        hkv = _rms(c_kv, kvn_ref[...]).astype(BF16)
        kn = jnp.dot(hkv, wk_ref[...], preferred_element_type=F32)
        k = kn + jnp.concatenate([slot] * MLA_HEADS, axis=1)
        k_ref[0, rows, :] = k.astype(BF16)
        vT = _nt(wv_ref[...], hkv)
        vT = jnp.concatenate([x for h in range(MLA_HEADS) for x in (vT[h * MLA_V:(h + 1) * MLA_V], ones)], axis=0)
        vT_ref[0, cidx] = vT.astype(BF16)
    carry_ref[...] = tail


def _a_pre(x, g, w1, wkr, convw, qn, wq, kvn, wk, wv, cos, sin):
    b, s, d = x.shape
    tm = min(TOKEN_TILE, s)
    hw = MLA_HEADS * MLA_SLOT
    vw = MLA_HEADS * V_ROWS
    q_scale = (MLA_NOPE + MLA_ROPE) ** -0.5 * LOG2E
    kern = functools.partial(_a_pre_kernel, tm=tm, q_scale=q_scale)
    return pl.pallas_call(
        kern,
        grid=(b, s // tm),
        in_specs=[
            pl.BlockSpec((1, tm, d), lambda i, j: (i, j, 0)),
            _const_spec(g.shape), _const_spec(w1.shape), _const_spec(wkr.shape), _const_spec(convw.shape),
            _const_spec(qn.shape), _const_spec(wq.shape), _const_spec(kvn.shape), _const_spec(wk.shape),
            _const_spec(wv.shape),
            pl.BlockSpec((ROPE_HALF, tm), lambda i, j: (0, j)),
            pl.BlockSpec((ROPE_HALF, tm), lambda i, j: (0, j)),
        ],
        out_specs=[
            pl.BlockSpec((1, tm, CONV_WIDTH), lambda i, j: (i, j, 0)),
            pl.BlockSpec((1, hw, tm), lambda i, j: (i, 0, j)),
            pl.BlockSpec((1, tm, hw), lambda i, j: (i, j, 0)),
            pl.BlockSpec((1, tm // ATTN_TILE, vw, ATTN_TILE), lambda i, j: (i, j, 0, 0)),
        ],
        out_shape=[
            jax.ShapeDtypeStruct((b, s, CONV_WIDTH), BF16),
            jax.ShapeDtypeStruct((b, hw, s), BF16),
            jax.ShapeDtypeStruct((b, s, hw), BF16),
            jax.ShapeDtypeStruct((b, s // ATTN_TILE, vw, ATTN_TILE), BF16),
        ],
        scratch_shapes=[pltpu.VMEM((16, CONV_WIDTH), F32)],
        compiler_params=_cparams(("arbitrary", "arbitrary")),
        name="a_pre",
    )(x, g, w1, wkr, convw, qn, wq, kvn, wk, wv, cos, sin)


def _tile_pos(t, k0, q0):
    kpos = k0 + lax.broadcasted_iota(jnp.int32, (t, t), 0)
    qpos = q0 + lax.broadcasted_iota(jnp.int32, (t, t), 1)
    return kpos, qpos


def _stage_scores(s, s_slot):
    s_slot[...] = s
    return jnp.max(s, axis=0, keepdims=True)


def _stage_softmax(s_slot, m, mt):
    m_new = jnp.maximum(m, mt)
    return jnp.exp2(s_slot[...] - m_new).astype(BF16), m_new, jnp.exp2(m - m_new)


def _stage_pv(acc_slot, vT, p, alpha):
    acc_slot[...] = alpha * acc_slot[...] + jnp.dot(vT, p, preferred_element_type=F32)


def _causal_flash(i, n_rest, tile_fns, list_of, t, chains, scores, vtile, diag_extra, s_ref, acc_ref):
    mt0 = []
    for c in chains:
        mt0.append(_stage_scores(scores(c, i) + diag_extra, s_ref.at[c]))
        acc_ref[c] = jnp.zeros((V_ROWS, t), F32)
    init = (tuple(mt0), tuple(jnp.full((1, t), NEG, F32) for _ in chains))
    last = jnp.maximum(n_rest - 1, 0)

    def body(n, carry):
        mt, m = carry
        j_cur = [jnp.where(n == 0, i, f(jnp.maximum(n - 1, 0))) for f in tile_fns]
        j_next = [f(jnp.minimum(n, last)) for f in tile_fns]
        m_new, mt_next = [], []
        for c in chains:
            p, mn, alpha = _stage_softmax(s_ref.at[c], m[c], mt[c])
            m_new.append(mn)
            mt_next.append(_stage_scores(scores(c, j_next[list_of(c)]), s_ref.at[c]))
            _stage_pv(acc_ref.at[c], vtile(c, j_cur[list_of(c)]), p, alpha)
        return tuple(mt_next), tuple(m_new)

    n_iter = n_rest + 1
    quads = lax.shift_right_logical(n_iter, 2)

    def quad(k, cr):
        for u in range(4):
            cr = body(4 * k + u, cr)
        return cr

    carry = lax.fori_loop(0, quads, quad, init)
    lax.fori_loop(4 * quads, n_iter, body, carry)


def _normalized(acc_slot):
    acc = acc_slot[...]
    return acc[:V_ROWS - ONES_ROWS] / acc[V_ROWS - ONES_ROWS:V_ROWS - ONES_ROWS + 1]


def _mla_attn_kernel(qT_ref, k_ref, vT_ref, oT_ref, s_ref, acc_ref, *, t, hb):
    i = pl.program_id(2)

    def scores(h, j):
        k = k_ref[0, pl.ds(pl.multiple_of(j * t, t), t), h * MLA_SLOT:(h + 1) * MLA_SLOT]
        return jnp.dot(k, qT_ref[0, h * MLA_SLOT:(h + 1) * MLA_SLOT, :], preferred_element_type=F32)

    def vtile(h, j):
        return vT_ref[0, j, h * V_ROWS:(h + 1) * V_ROWS, :]

    kpos, qpos = _tile_pos(t, 0, 0)
    causal = jnp.where(kpos <= qpos, 0.0, NEG)
    _causal_flash(i, i, [lambda n: n], lambda h: 0, t, range(hb), scores, vtile, causal, s_ref, acc_ref)
    for h in range(hb):
        oT_ref[0, h * MLA_V:(h + 1) * MLA_V, :] = _normalized(acc_ref.at[h]).astype(BF16)


def _mla_attn(qT, k, vT):
    b, hw, s = qT.shape
    t = min(ATTN_TILE, s)
    nkv = s // t
    hb = MLA_HEADS_PER_STEP
    kern = functools.partial(_mla_attn_kernel, t=t, hb=hb)
    return pl.pallas_call(
        kern,
        grid=(b, MLA_HEADS // hb, s // t),
        in_specs=[
            pl.BlockSpec((1, hb * MLA_SLOT, t), lambda bi, h, i: (bi, h, i)),
            pl.BlockSpec((1, s, hb * MLA_SLOT), lambda bi, h, i: (bi, 0, h)),
            pl.BlockSpec((1, nkv, hb * V_ROWS, t), lambda bi, h, i: (bi, 0, h, 0)),
        ],
        out_specs=pl.BlockSpec((1, hb * MLA_V, t), lambda bi, h, i: (bi, h, i)),
        out_shape=jax.ShapeDtypeStruct((b, MLA_HEADS * MLA_V, s), BF16),
        scratch_shapes=[pltpu.VMEM((hb, t, t), F32), pltpu.VMEM((hb, V_ROWS, t), F32)],
        compiler_params=_cparams(("arbitrary", "arbitrary", "arbitrary")),
        name="mla_attn",
    )(qT, k, vT)


def _post_kernel(*refs, has_tok):
    if has_tok:
        x_ref, ytok_ref, yT_ref, wo_tok_ref, wo_ref, gpost_ref, gpre_ref, w1_ref, w2_ref, gmlp_ref, o_ref = refs
    else:
        x_ref, yT_ref, wo_ref, gpost_ref, gpre_ref, w1_ref, w2_ref, gmlp_ref, o_ref = refs
    x = x_ref[0]
    mix = _tn(yT_ref[0], wo_ref[...])
    if has_tok:
        mix = mix + jnp.dot(ytok_ref[0], wo_tok_ref[...], preferred_element_type=F32)
    x1 = x + _rms(mix, gpost_ref[...])
    h = _rms(x1, gpre_ref[...]).astype(BF16)
    acc = jnp.zeros_like(x1)
    for c in range(D_FF // FF_CHUNK):
        a = jnp.dot(h, w1_ref[:, c * FF_CHUNK:(c + 1) * FF_CHUNK], preferred_element_type=F32)
        a = jnp.maximum(a, 0.0)
        a = (a * a).astype(BF16)
        acc = acc + jnp.dot(a, w2_ref[c * FF_CHUNK:(c + 1) * FF_CHUNK, :], preferred_element_type=F32)
    o_ref[0] = x1 + _rms(acc, gmlp_ref[...])


def _post(x, ytok, yT, wo_tok, wo, gpost, gpre, w1, w2, gmlp):
    b, s, d = x.shape
    tm = min(POST_TILE, s)
    has_tok = ytok is not None
    kern = functools.partial(_post_kernel, has_tok=has_tok)
    xspec = pl.BlockSpec((1, tm, d), lambda i, j: (i, j, 0))
    ins, specs = [x], [xspec]
    if has_tok:
        ins.append(ytok)
        specs.append(pl.BlockSpec((1, tm, ytok.shape[-1]), lambda i, j: (i, j, 0)))
    ins.append(yT)
    specs.append(pl.BlockSpec((1, yT.shape[1], tm), lambda i, j: (i, 0, j)))
    if has_tok:
        ins.append(wo_tok)
        specs.append(_const_spec(wo_tok.shape, single_buffer=True))
    for a in (wo, gpost, gpre, w1, w2, gmlp):
        ins.append(a)
        specs.append(_const_spec(a.shape, single_buffer=True))
    return pl.pallas_call(
        kern,
        grid=(b, s // tm),
        in_specs=specs,
        out_specs=xspec,
        out_shape=jax.ShapeDtypeStruct(x.shape, x.dtype),
        compiler_params=_cparams(("arbitrary", "arbitrary")),
        name="post_tok" if has_tok else "post",
    )(*ins)


def _c_pre_kernel(x_ref, g_ref, wq_ref, wtok_ref, wf_ref, kfs_ref, kfw_ref, qT_ref, kc_ref, vc_ref, ks_ref, kw_ref,
                  vsT_ref, vwT_ref, gT_ref, z_ref, *, tm, q_scale):
    xn = _rms(x_ref[0], g_ref[...]).astype(BF16)
    qT_ref[0] = (_nt(wq_ref[...], xn) * q_scale).astype(BF16)
    tok = jnp.dot(xn, wtok_ref[...], preferred_element_type=F32)
    halves = KV_W // LANES
    for lg in range(2 * halves):
        z_ref[lg] = tok[:, lg * LANES:(lg + 1) * LANES]
    for l in range(CMP_STRIDE):
        for lg in range(2 * halves):
            rows = z_ref[lg, pl.ds(l, tm // CMP_STRIDE, stride=CMP_STRIDE), :].astype(BF16)
            out_ref = kc_ref if lg < halves else vc_ref
            off = l * KV_W + (lg % halves) * LANES
            out_ref[0, :, off:off + LANES] = rows
    kfs = kfs_ref[...]
    kfw = kfw_ref[...]
    for g in range(NSA_GROUPS):
        a = 2 * KV_W + g * NSA_DH
        ks_ref[0, g] = jnp.concatenate([tok[:, a:a + NSA_DH], kfs], axis=1).astype(BF16)
        kw_ref[0, g] = jnp.concatenate([tok[:, a + KV_W:a + KV_W + NSA_DH], kfw], axis=1).astype(BF16)
    fT = _nt(wf_ref[...], xn)
    ones = _ones_rows(tm)

    def with_ones(v):
        return jnp.concatenate([x for g in range(NSA_GROUPS) for x in (v[g * NSA_DH:(g + 1) * NSA_DH], ones)],
                               axis=0)

    vs = with_ones(fT[:KV_W])
    vw = with_ones(fT[KV_W:2 * KV_W])
    for cidx in range(tm // ATTN_TILE):
        sl = slice(cidx * ATTN_TILE, (cidx + 1) * ATTN_TILE)
        vsT_ref[0, cidx] = vs[:, sl].astype(BF16)
        vwT_ref[0, cidx] = vw[:, sl].astype(BF16)
    gT_ref[0] = 1.0 / (1.0 + jnp.exp(-fT[2 * KV_W:]))


def _c_pre(x, g, wq, wtok, wf, kfs, kfw):
    b, s, d = x.shape
    tm = min(TOKEN_TILE, s)
    q_scale = NSA_DH ** -0.5 * LOG2E
    kern = functools.partial(_c_pre_kernel, tm=tm, q_scale=q_scale)
    nt = tm // ATTN_TILE
    grows = NSA_GROUPS * GATE_ROWS
    kaug = NSA_DH + kfs.shape[1]
    vrows = NSA_GROUPS * V_ROWS
    kf_spec = pl.BlockSpec((tm, kfs.shape[1]), lambda i, j: (j, 0))
    return pl.pallas_call(
        kern,
        grid=(b, s // tm),
        in_specs=[pl.BlockSpec((1, tm, d), lambda i, j: (i, j, 0)),
                  _const_spec(g.shape), _const_spec(wq.shape), _const_spec(wtok.shape), _const_spec(wf.shape),
                  kf_spec, kf_spec],
        out_specs=[
            pl.BlockSpec((1, C_MIX, tm), lambda i, j: (i, 0, j)),
            pl.BlockSpec((1, tm // CMP_STRIDE, CMP_STRIDE * KV_W), lambda i, j: (i, j, 0)),
            pl.BlockSpec((1, tm // CMP_STRIDE, CMP_STRIDE * KV_W), lambda i, j: (i, j, 0)),
            pl.BlockSpec((1, NSA_GROUPS, tm, kaug), lambda i, j: (i, 0, j, 0)),
            pl.BlockSpec((1, NSA_GROUPS, tm, kaug), lambda i, j: (i, 0, j, 0)),
            pl.BlockSpec((1, nt, vrows, ATTN_TILE), lambda i, j: (i, j, 0, 0)),
            pl.BlockSpec((1, nt, vrows, ATTN_TILE), lambda i, j: (i, j, 0, 0)),
            pl.BlockSpec((1, grows, tm), lambda i, j: (i, 0, j)),
        ],
        out_shape=[
            jax.ShapeDtypeStruct((b, C_MIX, s), BF16),
            jax.ShapeDtypeStruct((b, s // CMP_STRIDE, CMP_STRIDE * KV_W), BF16),
            jax.ShapeDtypeStruct((b, s // CMP_STRIDE, CMP_STRIDE * KV_W), BF16),
            jax.ShapeDtypeStruct((b, NSA_GROUPS, s, kaug), BF16),
            jax.ShapeDtypeStruct((b, NSA_GROUPS, s, kaug), BF16),
            jax.ShapeDtypeStruct((b, s // ATTN_TILE, vrows, ATTN_TILE), BF16),
            jax.ShapeDtypeStruct((b, s // ATTN_TILE, vrows, ATTN_TILE), BF16),
            jax.ShapeDtypeStruct((b, grows, s), F32),
        ],
        scratch_shapes=[pltpu.VMEM((2 * KV_W // LANES, tm, LANES), F32)],
        compiler_params=_cparams(("arbitrary", "arbitrary")),
        name="c_pre",
    )(x, g, wq, wtok, wf, kfs, kfw)


def _gelu_tanh(x):
    return 0.5 * x * (1.0 + jnp.tanh(math.sqrt(2.0 / math.pi) * (x + 0.044715 * (x * x * x))))


def _compress_kernel(zk_ref, zv_ref, pek_ref, pev_ref, wak_ref, wbk_ref, wav_ref, wbv_ref, w2k_ref, w2vT_ref, kf_ref,
                     kc_ref, vcT_ref, *, nch):
    def hidden(z_ref, pe_ref, wa_ref, wb_ref):
        z = z_ref[0].astype(F32)
        za = (z + pe_ref[0:1]).astype(BF16)
        zb = (z + pe_ref[1:2]).astype(BF16)
        a = jnp.dot(za, wa_ref[...], preferred_element_type=F32)
        bm = jnp.dot(zb, wb_ref[...], preferred_element_type=F32)
        return _gelu_tanh(a + pltpu.roll(bm, nch - 1, 0)).astype(BF16)

    hk = hidden(zk_ref, pek_ref, wak_ref, wbk_ref)
    kc = jnp.dot(hk, w2k_ref[...], preferred_element_type=F32)
    kf = kf_ref[...]
    hv = hidden(zv_ref, pev_ref, wav_ref, wbv_ref)
    vcT = _nt(w2vT_ref[...], hv)
    ones = _ones_rows(nch)
    for g in range(NSA_GROUPS):
        gs = slice(g * NSA_DH, (g + 1) * NSA_DH)
        kc_ref[0, g] = jnp.concatenate([kc[:, gs], kf], axis=1).astype(BF16)
        vcT_ref[0, g] = jnp.concatenate([vcT[gs], ones], axis=0).astype(BF16)


def _compress(zk, zv, pek, pev, wak, wbk, wav, wbv, w2k, w2vT, kf):
    b, nch, cw = zk.shape
    kern = functools.partial(_compress_kernel, nch=nch)
    zspec = pl.BlockSpec((1, nch, cw), lambda i: (i, 0, 0))
    consts = (pek, pev, wak, wbk, wav, wbv, w2k, w2vT, kf)
    return pl.pallas_call(
        kern,
        grid=(b,),
        in_specs=[zspec, zspec] + [_const_spec(a.shape) for a in consts],
        out_specs=[pl.BlockSpec((1, NSA_GROUPS, nch, NSA_DH + KFEAT), lambda i: (i, 0, 0, 0)),
                   pl.BlockSpec((1, NSA_GROUPS, V_ROWS, nch), lambda i: (i, 0, 0, 0))],
        out_shape=[jax.ShapeDtypeStruct((b, NSA_GROUPS, nch, NSA_DH + KFEAT), BF16),
                   jax.ShapeDtypeStruct((b, NSA_GROUPS, V_ROWS, nch), BF16)],
        compiler_params=_cparams(("arbitrary",)),
        name="compress",
    )(zk, zv, *consts)


def _split3(x):
    hi = x.astype(BF16)
    r = x - hi.astype(F32)
    mid = r.astype(BF16)
    lo = (r - mid.astype(F32)).astype(BF16)
    return hi, mid, lo


def _feature_rows(tab_ref, head, const, t):
    row = lax.broadcasted_iota(jnp.int32, (KFEAT, t), 0)
    r3 = row - 3 * (row >= 3).astype(jnp.int32) - 3 * (row >= 6).astype(jnp.int32)
    s_hi, s_mid, s_lo, slope = (tab_ref[head * 4 + k] for k in range(4))
    c_hi, c_mid, c_lo = (x.astype(F32) for x in _split3(slope * const))
    svals = jnp.where(r3 == 0, s_hi, jnp.where(r3 == 1, s_mid, s_lo))
    cvals = jnp.where(r3 == 0, c_hi, jnp.where(r3 == 1, c_mid, c_lo))
    return jnp.where(row < 6, svals, jnp.where(row < 9, cvals, 0.0)).astype(BF16)


def _cmp_topk_kernel(tab_ref, qT_ref, kc_ref, vcT_ref, pool_ref, gT_ref, ocT_ref, bias_ref, act_ref,
                     qa_ref, e_ref, acc_ref, imp_ref, rank_ref, ps_ref, *, t, gb, nch, nslc, nsel, ov_terms):
    gp = pl.program_id(1)
    i = pl.program_id(2)
    q0 = i * t
    chains = range(gb * NSA_REP)
    bpt = t // SLC_LEN

    const = (CMP_LEN - 1) / 2.0 - jnp.full((KFEAT, t), q0, jnp.int32).astype(F32)
    for c in chains:
        head = (gp * gb + c // NSA_REP) * NSA_REP + c % NSA_REP
        qa_ref[c, 0:NSA_DH, :] = qT_ref[0, c * NSA_DH:(c + 1) * NSA_DH, :]
        qa_ref[c, NSA_DH:NSA_DH + KFEAT, :] = _feature_rows(tab_ref, head, const, t)

    def branch(rows):
        n_i = lax.broadcasted_iota(jnp.int32, (rows, t), 0)
        t_i = q0 + lax.broadcasted_iota(jnp.int32, (rows, t), 1)
        ended = jnp.where(n_i * CMP_STRIDE + (CMP_LEN - 1) <= t_i, 0.0, NEG)

        def cmp_scores(c):
            s = jnp.dot(kc_ref[0, c // NSA_REP, 0:rows, :], qa_ref[c], preferred_element_type=F32) + ended
            return _stage_scores(s, e_ref.at[c, 0:rows])

        def cmp_softmax_pv(c, mt):
            m_eff = jnp.where(mt > 0.5 * NEG, mt, 0.0)
            e = jnp.exp2(e_ref[c, 0:rows] - m_eff)
            e_ref[c, 0:rows] = e
            acc_ref[c] = jnp.dot(vcT_ref[0, c // NSA_REP, :, 0:rows], e.astype(BF16), preferred_element_type=F32)

        mt = [cmp_scores(c) for c in chains]
        for c in chains:
            cmp_softmax_pv(c, mt[c])
        for gl in range(gb):
            psum = jnp.zeros((rows, t), F32)
            for r in range(NSA_REP):
                c = gl * NSA_REP + r
                acc = acc_ref[c]
                l = acc[NSA_DH:NSA_DH + 1]
                inv = 1.0 / jnp.where(l > 0.0, l, 1.0)
                gate = gT_ref[0, gl * GATE_ROWS + r:gl * GATE_ROWS + r + 1, :]
                ocT_ref[0, c * NSA_DH:(c + 1) * NSA_DH, :] = (gate * (acc[:NSA_DH] * inv)).astype(BF16)
                psum = psum + e_ref[c, 0:rows] * inv
            for h in range(t // LANES):
                ps_ref[gl, h, 0:rows] = psum[:, h * LANES:(h + 1) * LANES]
                if rows < nch:
                    ps_ref[gl, h, rows:nch] = jnp.zeros((nch - rows, LANES), F32)

    per_tile = t // CMP_STRIDE
    for quarter in range(1, 5):
        rows = quarter * nch // 4
        lo, hi = (quarter - 1) * nch // 4 // per_tile, rows // per_tile
        pl.when((i >= lo) & (i < hi))(functools.partial(branch, rows))

    j_i = lax.broadcasted_iota(jnp.int32, (nslc, t), 0)
    cur = jnp.right_shift(q0 + lax.broadcasted_iota(jnp.int32, (nslc, t), 1), SLC_SHIFT)
    forced = (j_i == 0) | (j_i == cur) | (j_i == cur - 1)
    per_blk = SLC_LEN // CMP_STRIDE
    for gl in range(gb):
        imp = jnp.zeros((nslc, t), F32)
        for k, wgt in ov_terms:
            rows = jnp.concatenate([ps_ref[gl, h, pl.ds(k % per_blk, nslc, stride=per_blk), :]
                                    for h in range(t // LANES)], axis=1)
            if k < 0:
                rows = jnp.where(j_i == 0, 0.0, pltpu.roll(rows, 1, 0))
            imp = imp + wgt * rows
        imp_ref[gl] = jnp.where(j_i > cur, NEG, imp + jnp.where(forced, FORCE_BONUS, 0.0))
        rank_ref[gl] = jnp.zeros((nslc, t), F32)

    for chunk in range(nslc // bpt):
        @pl.when(chunk <= i)
        def _():
            for gl in range(gb):
                imp = imp_ref[gl]
                rank = rank_ref[gl]
                for jp in range(chunk * bpt, (chunk + 1) * bpt):
                    row = imp[jp:jp + 1, :]
                    incs = []
                    for rg in range(nslc // SUBLANES):
                        r0 = SUBLANES * rg
                        blk = imp[r0:r0 + SUBLANES]
                        if r0 > jp:
                            incs.append(jnp.where(row >= blk, 1.0, 0.0))
                        elif r0 + SUBLANES - 1 <= jp:
                            incs.append(jnp.where(row > blk, 1.0, 0.0))
                        else:
                            above = lax.broadcasted_iota(jnp.int32, (SUBLANES, t), 0) > jp - r0
                            incs.append(jnp.where(above, jnp.where(row >= blk, 1.0, 0.0),
                                                  jnp.where(row > blk, 1.0, 0.0)))
                    rank = rank + jnp.concatenate(incs, axis=0)
                rank_ref[gl] = rank

    for gl in range(gb):
        bias = jnp.where(j_i > cur, NEG, jnp.where(rank_ref[gl] < nsel, 0.0, NEG))
        bias_ref[0, gl] = bias
        hit = jnp.where(jnp.max(bias, axis=1, keepdims=True) == 0.0, 1.0, 0.0)
        tiles = jnp.sum(hit * pool_ref[...], axis=0, keepdims=True)
        act_ref[0, gl, 0] = jnp.broadcast_to(jnp.where(tiles > 0.0, 1, 0), (SUBLANES, LANES)).astype(jnp.int32)


def _overlap_terms():
    per_blk = SLC_LEN // CMP_STRIDE
    terms = []
    for k in range(-(CMP_LEN // CMP_STRIDE) + 1, per_blk):
        lo, hi = max(k * CMP_STRIDE, 0), min(k * CMP_STRIDE + CMP_LEN, SLC_LEN)
        if hi > lo:
            terms.append((k, (hi - lo) / CMP_LEN))
    assert SLC_LEN % CMP_STRIDE == 0 and all(k >= -per_blk for k, _ in terms)
    return tuple(terms)


def _cmp_topk(tab, qT, kc, vcT, gT, nsel):
    b, _, s = qT.shape
    t = ATTN_TILE
    assert s % t == 0 and t % SLC_LEN == 0
    nch = kc.shape[2]
    nslc = s // SLC_LEN
    gb = CMP_GROUPS_PER_STEP
    nc = gb * NSA_REP
    kern = functools.partial(_cmp_topk_kernel, t=t, gb=gb, nch=nch, nslc=nslc, nsel=nsel,
                             ov_terms=_overlap_terms())
    gw = nc * NSA_DH
    assert s // t <= LANES
    pool = np.zeros((nslc, LANES), np.float32)
    pool[np.arange(nslc), np.arange(nslc) // (t // SLC_LEN)] = 1.0
    pool = jnp.asarray(pool, F32)
    return pl.pallas_call(
        kern,
        grid=(b, NSA_GROUPS // gb, s // t),
        in_specs=[
            pl.BlockSpec(memory_space=pltpu.SMEM),
            pl.BlockSpec((1, gw, t), lambda bi, g, i: (bi, g, i)),
            pl.BlockSpec((1, gb, nch, NSA_DH + KFEAT), lambda bi, g, i: (bi, g, 0, 0)),
            pl.BlockSpec((1, gb, V_ROWS, nch), lambda bi, g, i: (bi, g, 0, 0)),
            _const_spec(pool.shape),
            pl.BlockSpec((1, gb * GATE_ROWS, t), lambda bi, g, i: (bi, g, i)),
        ],
        out_specs=[pl.BlockSpec((1, gw, t), lambda bi, g, i: (bi, g, i)),
                   pl.BlockSpec((1, gb, nslc, t), lambda bi, g, i: (bi, g, 0, i)),
                   pl.BlockSpec((1, gb, 1, SUBLANES, LANES), lambda bi, g, i: (bi, g, i, 0, 0))],
        out_shape=[jax.ShapeDtypeStruct((b, C_MIX, s), BF16),
                   jax.ShapeDtypeStruct((b, NSA_GROUPS, nslc, s), F32),
                   jax.ShapeDtypeStruct((b, NSA_GROUPS, s // t, SUBLANES, LANES), jnp.int32)],
        scratch_shapes=[pltpu.VMEM((nc, NSA_DH + KFEAT, t), BF16),
                        pltpu.VMEM((nc, nch, t), F32),
                        pltpu.VMEM((nc, V_ROWS, t), F32),
                        pltpu.VMEM((gb, nslc, t), F32), pltpu.VMEM((gb, nslc, t), F32),
                        pltpu.VMEM((gb, t // LANES, nch, LANES), F32)],
        compiler_params=_cparams(("arbitrary", "arbitrary", "arbitrary")),
        name="cmp_topk",
    )(tab, qT, kc, vcT, pool, gT)


def _slc_win_kernel(act_ref, tab_ref, qT_ref, ks_ref, vsT_ref, kw_ref, vwT_ref, bias_ref, ocT_ref, gT_ref, yT_ref,
                    qa_ref, s_ref, acc_ref, sw_ref, accw_ref, idx_ref, *, t, gb, nslc, nkv):
    gp = pl.program_id(1)
    i = pl.program_id(2)
    chains = range(gb * NSA_REP)
    feat0 = NSA_DH + nslc

    const = -jnp.full((KFEAT, t), i * t, jnp.int32).astype(F32)
    for c in chains:
        gl, r = divmod(c, NSA_REP)
        head = (gp * gb + gl) * NSA_REP + r
        qa_ref[c, 0:NSA_DH, :] = qT_ref[0, c * NSA_DH:(c + 1) * NSA_DH, :]
        qa_ref[c, NSA_DH:feat0, :] = bias_ref[0, gl].astype(BF16)
        qa_ref[c, feat0:feat0 + KFEAT, :] = _feature_rows(tab_ref, head, const, t)

    def scores_from(k_ref):
        def scores(c, j):
            k = k_ref[0, c // NSA_REP, pl.ds(pl.multiple_of(j * t, t), t), :]
            return jnp.dot(k, qa_ref[c], preferred_element_type=F32)
        return scores

    def vtile_from(vT_ref):
        def vtile(c, j):
            gl = c // NSA_REP
            return vT_ref[0, j, gl * V_ROWS:(gl + 1) * V_ROWS, :]
        return vtile

    kpos, qpos = _tile_pos(t, 0, 0)
    causal = jnp.where(kpos <= qpos, 0.0, NEG)

    w_scores = scores_from(kw_ref)
    w_vtile = vtile_from(vwT_ref)
    w_tiles = (i, jnp.maximum(i - 1, 0), jnp.maximum(i - 2, 0))
    in_seq1 = jnp.where((i - 1) * t + kpos >= 0, 0.0, NEG)
    in_seq2 = jnp.where((i - 2) * t + kpos >= 0, 0.0, NEG)
    w_extra = (causal, in_seq1, jnp.where(qpos < kpos, in_seq2, NEG))
    m = [jnp.full((1, t), NEG, F32) for _ in chains]
    mt = []
    for c in chains:
        accw_ref[c] = jnp.zeros((V_ROWS, t), F32)
        mt.append(_stage_scores(w_scores(c, w_tiles[0]) + w_extra[0], sw_ref.at[c]))
    for w in range(3):
        for c in chains:
            p, m[c], alpha = _stage_softmax(sw_ref.at[c], m[c], mt[c])
            if w + 1 < 3:
                mt[c] = _stage_scores(w_scores(c, w_tiles[w + 1]) + w_extra[w + 1], sw_ref.at[c])
            _stage_pv(accw_ref.at[c], w_vtile(c, w_tiles[w]), p, alpha)

    stride = nkv + 1
    counts, spares = [], []
    for gl in range(gb):
        g_abs = gp * gb + gl
        base = ((pl.program_id(0) * NSA_GROUPS + g_abs) * pl.num_programs(2) + i) * nkv
        n_act, spare = jnp.int32(0), jnp.int32(0)
        for j in range(nkv):
            idx_ref[gl * stride + n_act] = j
            keep = (j < i) & (act_ref[base + j] != 0)
            n_act = n_act + jnp.where(keep, 1, 0)
            spare = jnp.where((j < i) & jnp.logical_not(keep), j, spare)
        counts.append(n_act)
        spares.append(spare)
    n_rest = functools.reduce(jnp.maximum, counts)
    for gl in range(gb):
        for k in range(nkv):
            idx_ref[gl * stride + k] = jnp.where(k >= counts[gl], spares[gl], idx_ref[gl * stride + k])
    tile_fns = [functools.partial(lambda gl, n: idx_ref[gl * stride + n], gl) for gl in range(gb)]
    _causal_flash(i, n_rest, tile_fns, lambda c: c // NSA_REP, t, chains, scores_from(ks_ref),
                  vtile_from(vsT_ref), causal, s_ref, acc_ref)

    for c in chains:
        gl, r = divmod(c, NSA_REP)
        g0 = gl * GATE_ROWS
        rs = slice(c * NSA_DH, (c + 1) * NSA_DH)
        y = (ocT_ref[0, rs, :].astype(F32)
             + gT_ref[0, g0 + NSA_REP + r:g0 + NSA_REP + r + 1, :] * _normalized(acc_ref.at[c])
             + gT_ref[0, g0 + 2 * NSA_REP + r:g0 + 2 * NSA_REP + r + 1, :] * _normalized(accw_ref.at[c]))
        yT_ref[0, rs, :] = y.astype(BF16)


def _slc_win(act, tab, qT, ks, vsT, kw, vwT, bias, ocT, gT):
    b, _, s = qT.shape
    t = ATTN_TILE
    assert s % t == 0 and WINDOW == 2 * t
    nkv = s // t
    nslc = bias.shape[2]
    kaug = ks.shape[3]
    gb = NSA_GROUPS_PER_STEP
    nc = gb * NSA_REP
    gw = nc * NSA_DH
    kern = functools.partial(_slc_win_kernel, t=t, gb=gb, nslc=nslc, nkv=nkv)
    qspec = pl.BlockSpec((1, gw, t), lambda bi, g, i, *_: (bi, g, i))
    kspec = pl.BlockSpec((1, gb, s, kaug), lambda bi, g, i, *_: (bi, g, 0, 0))
    vspec = pl.BlockSpec((1, nkv, gb * V_ROWS, t), lambda bi, g, i, *_: (bi, 0, g, 0))
    grid_spec = pltpu.PrefetchScalarGridSpec(
        num_scalar_prefetch=2,
        grid=(b, NSA_GROUPS // gb, s // t),
        in_specs=[
            qspec, kspec, vspec, kspec, vspec,
            pl.BlockSpec((1, gb, nslc, t), lambda bi, g, i, *_: (bi, g, 0, i)),
            qspec,
            pl.BlockSpec((1, gb * GATE_ROWS, t), lambda bi, g, i, *_: (bi, g, i)),
        ],
        out_specs=qspec,
        scratch_shapes=[pltpu.VMEM((nc, kaug, t), BF16),
                        pltpu.VMEM((nc, t, t), F32), pltpu.VMEM((nc, V_ROWS, t), F32),
                        pltpu.VMEM((nc, t, t), F32), pltpu.VMEM((nc, V_ROWS, t), F32),
                        pltpu.SMEM((gb * (nkv + 1),), jnp.int32)],
    )
    return pl.pallas_call(
        kern,
        grid_spec=grid_spec,
        out_shape=jax.ShapeDtypeStruct((b, C_MIX, s), BF16),
        compiler_params=_cparams(("arbitrary", "arbitrary", "arbitrary")),
        name="slc_win",
    )(act, tab, qT, ks, vsT, kw, vwT, bias, ocT, gT)


def _row(v):
    return v.reshape(1, -1).astype(F32)


def _layer_a(x, g_pre, w_in, conv_w, q_norm, w_q_up, kv_norm, w_kv_up, w_out, g_post, g_mlp_pre, w1, w2, g_mlp_post):
    b, s, d = x.shape
    o3 = 3 * CONV_WIDTH
    o5 = o3 + MLA_Q_RANK + MLA_KV_RANK
    pad = MLA_SLOT - MLA_NOPE - MLA_ROPE
    w_main = w_in[:, :o5].astype(BF16)
    wkr = jnp.pad(w_in[:, o5:].T, ((MLA_NOPE, pad), (0, 0))).astype(BF16)
    wq = w_q_up.reshape(MLA_Q_RANK, MLA_HEADS, MLA_NOPE + MLA_ROPE)
    wq = jnp.pad(wq, ((0, 0), (0, 0), (0, pad))).reshape(MLA_Q_RANK, MLA_HEADS * MLA_SLOT).T.astype(BF16)
    wkv = w_kv_up.reshape(MLA_KV_RANK, MLA_HEADS, MLA_NOPE + MLA_V)
    wk = jnp.pad(wkv[:, :, :MLA_NOPE], ((0, 0), (0, 0), (0, MLA_SLOT - MLA_NOPE)))
    wk = wk.reshape(MLA_KV_RANK, MLA_HEADS * MLA_SLOT).astype(BF16)
    wv = wkv[:, :, MLA_NOPE:].reshape(MLA_KV_RANK, MLA_HEADS * MLA_V).T.astype(BF16)

    inv = ROPE_THETA ** (-jnp.arange(ROPE_HALF, dtype=F32) / ROPE_HALF)
    ang = inv[:, None] * jnp.arange(s, dtype=F32)[None, :]
    cos, sin = jnp.cos(ang), jnp.sin(ang)

    yconv, qT, k, vT = _a_pre(x, _row(g_pre), w_main, wkr, conv_w.astype(F32), _row(q_norm), wq, _row(kv_norm),
                              wk, wv, cos, sin)
    yT = _mla_attn(qT, k, vT)
    wo = w_out.astype(BF16)
    return _post(x, yconv, yT, wo[:CONV_WIDTH], wo[CONV_WIDTH:], _row(g_post), _row(g_mlp_pre),
                 w1.astype(BF16), w2.astype(BF16), _row(g_mlp_post))


def _layer_c(x, g_pre, w_in, pe_k, w1_k, w2_k, pe_v, w1_v, w2_v, w_out, g_post, g_mlp_pre, w1, w2, g_mlp_post):
    b, s, d = x.shape
    G, R, Dh = NSA_GROUPS, NSA_REP, NSA_DH
    wq = w_in[:, :C_MIX].T.astype(BF16)
    kv = w_in[:, C_MIX:C_MIX + 6 * KV_W].reshape(d, 6, KV_W)
    wtok = jnp.concatenate([kv[:, 0], kv[:, 1], kv[:, 2], kv[:, 4]], axis=1).astype(BF16)
    wg = w_in[:, C_MIX + 6 * KV_W:].reshape(d, 3, G, R).transpose(0, 2, 1, 3)
    wg = jnp.pad(wg.reshape(d, G, 3 * R), ((0, 0), (0, 0), (0, GATE_ROWS - 3 * R))).reshape(d, G * GATE_ROWS)
    wf = jnp.concatenate([kv[:, 3], kv[:, 5], wg], axis=1).T.astype(BF16)

    n_slc = s // SLC_LEN
    pos = np.arange(s)
    blk, within = pos // SLC_LEN, pos % SLC_LEN
    feats = np.zeros((s, KFEAT), np.float32)
    feats[:, 0:3] = (blk * SLC_LEN)[:, None]
    feats[:, 3:6] = within[:, None]
    feats[:, 6:9] = 1.0
    onehot = (blk[:, None] == np.arange(n_slc)[None, :]).astype(np.float32)
    kfs = jnp.asarray(np.concatenate([onehot, feats], axis=1), F32)
    kfw = jnp.asarray(np.concatenate([np.zeros_like(onehot), feats], axis=1), F32)

    qT, kc_z, vc_z, ks, kw, vsT, vwT, gT = _c_pre(x, _row(g_pre), wq, wtok, wf, kfs, kfw)

    nch = s // CMP_STRIDE
    half = CMP_LEN // 2
    eye = jnp.eye(G, dtype=F32)

    def w1_halves(w):
        bd = jnp.einsum('hlde,gk->hlgdke', w.reshape(2, half, Dh, Dh), eye)
        bd = bd.reshape(2, half * KV_W, KV_W).astype(BF16)
        return bd[0], bd[1]

    def pe_halves(pe):
        return jnp.tile(pe.reshape(2, half, 1, Dh), (1, 1, G, 1)).reshape(2, half * KV_W).astype(F32)

    def w2_bd(w):
        return jnp.einsum('de,gk->gdke', w, eye).reshape(KV_W, KV_W)

    n_idx = np.arange(nch)
    cfeat = np.zeros((nch, KFEAT), np.float32)
    cfeat[:, 0:3] = (1024 * (n_idx // 64))[:, None]
    cfeat[:, 3:6] = (CMP_STRIDE * (n_idx % 64))[:, None]
    cfeat[:, 6:9] = 1.0
    kc, vcT = _compress(kc_z, vc_z, pe_halves(pe_k), pe_halves(pe_v), *w1_halves(w1_k), *w1_halves(w1_v),
                        w2_bd(w2_k).astype(BF16), w2_bd(w2_v).T.astype(BF16), jnp.asarray(cfeat, F32))

    slopes_np = (2.0 ** (-8.0 * np.arange(1, NSA_HEADS + 1) / NSA_HEADS)).astype(np.float32)
    full = (slopes_np * np.float32(LOG2E)).astype(np.float32)
    pieces, rest = [], full.copy()
    for _ in range(3):
        piece = rest.astype(BF16).astype(np.float32)
        pieces.append(piece)
        rest = (rest - piece).astype(np.float32)
    tab = jnp.asarray(np.stack(pieces + [full], axis=1).reshape(-1), F32)

    ocT, bias, act = _cmp_topk(tab, qT, kc, vcT, gT, min(N_SEL, n_slc))
    act = act[:, :, :, 0, :s // ATTN_TILE].reshape(-1)
    yT = _slc_win(act, tab, qT, ks, vsT, kw, vwT, bias, ocT, gT)
    return _post(x, None, yT, None, w_out.astype(BF16), _row(g_post), _row(g_mlp_pre),
                 w1.astype(BF16), w2.astype(BF16), _row(g_mlp_post))


def kernel(x, norm_mix_pre, norm_mix_post, norm_mlp_pre, norm_mlp_post, mlp_w1, mlp_w2, a_w_in, a_conv_w, a_q_norm,
           a_w_q_up, a_kv_norm, a_w_kv_up, a_w_out, c_w_in, c_cmp_pe_k, c_cmp_w1_k, c_cmp_w2_k, c_cmp_pe_v,
           c_cmp_w1_v, c_cmp_w2_v, c_w_out):
    depth = norm_mix_pre.shape[0]
    for layer in range(depth):
        i = layer // 2
        common = (norm_mix_post[layer], norm_mlp_pre[layer], mlp_w1[layer], mlp_w2[layer], norm_mlp_post[layer])
        if layer % 2 == 0:
            x = _layer_a(x, norm_mix_pre[layer], a_w_in[i], a_conv_w[i], a_q_norm[i], a_w_q_up[i], a_kv_norm[i],
                         a_w_kv_up[i], a_w_out[i], *common)
        else:
            x = _layer_c(x, norm_mix_pre[layer], c_w_in[i], c_cmp_pe_k[i], c_cmp_w1_k[i], c_cmp_w2_k[i],
                         c_cmp_pe_v[i], c_cmp_w1_v[i], c_cmp_w2_v[i], c_w_out[i], *common)
    return x
```
